```python
import jax, jax.numpy as jnp
from jax import lax
import numpy as np

D_MODEL = 1024
BATCH = 8
SEQ = 4096
DEPTH = 2

CHUNK = 64
Q_BLOCK = 2 * CHUNK
HEAD_DIM = 64
CONV_CH = 256
CONV_W = 3
SB_HEADS = 6
FOX_HEADS = 6
SB_DIM = SB_HEADS * HEAD_DIM
FOX_DIM = FOX_HEADS * HEAD_DIM
D_MIX = CONV_CH + SB_DIM + FOX_DIM
D_IN = 3 * CONV_CH + 3 * SB_DIM + 3 * FOX_DIM + FOX_HEADS
N_MOD = 6
N_EXPERTS = 32
TOP_K = 4
D_EXPERT = D_MODEL
SWIGLU_ALPHA = 1.702
SWIGLU_LIMIT = 7.0
EPS = 1e-6

kernel_name = 'hybrid_conv_stickbreak_fox_moe_adaln'


def _rms(x):
    xf = x.astype(jnp.float32)
    return (xf * lax.rsqrt(jnp.mean(xf * xf, axis=-1, keepdims=True) + EPS)).astype(x.dtype)


def rms_norm(x, g):
    return _rms(x) * g


def short_conv_mixer(gb, gc, u, conv_w):
    v = gc * u
    y = lax.conv_general_dilated(v, conv_w[:, None, :], window_strides=(1,), padding=[(CONV_W - 1, 0)],
                                 dimension_numbers=('NWC', 'WIO', 'NWC'), feature_group_count=CONV_CH)
    return gb * y


def _query_blocks(t):
    b, h, s = t.shape[:3]
    t = t.reshape((b, h, s // Q_BLOCK, Q_BLOCK) + t.shape[3:])
    return jnp.moveaxis(t, 2, 0)


def _merge_blocks(o):
    nb, b, h, qb, d = o.shape
    return o.transpose(1, 0, 3, 2, 4).reshape(b, nb * qb, h * d)


def stick_breaking_attention(q, k, v):
    s = q.shape[2]
    scale = q.shape[-1] ** -0.5
    kpos = jnp.arange(s)

    def block(args):
        qb, start = args
        z = jnp.einsum('bhqd,bhkd->bhqk', qb, k, preferred_element_type=jnp.float32) * scale
        qpos = start + jnp.arange(Q_BLOCK)
        strict = kpos[None, :] < qpos[:, None]
        log_beta = jax.nn.log_sigmoid(z)
        log_1mb = jnp.where(strict, jax.nn.log_sigmoid(-z), 0.0)
        rest = lax.cumsum(log_1mb, axis=3, reverse=True) - log_1mb
        w = jnp.where(strict, jnp.exp(log_beta + rest), 0.0)
        return jnp.einsum('bhqk,bhkd->bhqd', w.astype(v.dtype), v)

    starts = jnp.arange(s // Q_BLOCK, dtype=jnp.int32) * Q_BLOCK
    return _merge_blocks(lax.map(block, (_query_blocks(q), starts)))


def forgetting_attention(q, k, v, log_f):
    s = q.shape[2]
    scale = q.shape[-1] ** -0.5
    kpos = jnp.arange(s)
    cum = lax.cumsum(log_f, axis=2)

    def block(args):
        qb, fq, start = args
        logits = jnp.einsum('bhqd,bhkd->bhqk', qb, k, preferred_element_type=jnp.float32) * scale
        logits = logits + (fq[..., None] - cum[:, :, None, :])
        qpos = start + jnp.arange(Q_BLOCK)
        causal = kpos[None, :] <= qpos[:, None]
        p = jax.nn.softmax(jnp.where(causal, logits, -jnp.inf), axis=-1)
        return jnp.einsum('bhqk,bhkd->bhqd', p.astype(v.dtype), v)

    starts = jnp.arange(s // Q_BLOCK, dtype=jnp.int32) * Q_BLOCK
    return _merge_blocks(lax.map(block, (_query_blocks(q), _query_blocks(cum), starts)))


def moe_ffn(h, router_w, router_b, w_mlp1, b_mlp1, w_mlp2, b_mlp2):
    b, s, d = h.shape
    xt = h.reshape(b * s, d)
    logits = (xt @ router_w + router_b).astype(jnp.float32)
    top_vals, top_idx = lax.top_k(logits, TOP_K)
    top_w = jax.nn.softmax(top_vals, axis=-1)
    combine = jnp.einsum('tk,tke->et', top_w, jax.nn.one_hot(top_idx, N_EXPERTS, dtype=jnp.float32))

    def expert_step(acc, params):
        w1, b1, w2, b2, gate_e = params
        u = xt @ w1 + b1
        glu = jnp.minimum(u[:, ::2], SWIGLU_LIMIT)
        lin = jnp.clip(u[:, 1::2], -SWIGLU_LIMIT, SWIGLU_LIMIT)
        a = glu * jax.nn.sigmoid(SWIGLU_ALPHA * glu) * (lin + 1)
        out = a @ w2 + b2
        return acc + gate_e[:, None] * out.astype(jnp.float32), None

    acc0 = jnp.zeros((b * s, d), jnp.float32)
    acc, _ = lax.scan(expert_step, acc0, (w_mlp1, b_mlp1, w_mlp2, b_mlp2, combine))
    return acc.astype(h.dtype).reshape(b, s, d)


def hybrid_layer(x, c, norm1_g, w_ada, b_ada, w_in, conv_w, sb_q_g, sb_k_g, fox_q_g, fox_k_g, fox_f_b,
                 out_norm_g, w_out, norm2_g, router_w, router_b, w_mlp1, b_mlp1, w_mlp2, b_mlp2):
    b, s, d = x.shape
    mod = (jax.nn.silu(c) @ w_ada + b_ada).reshape(b, N_MOD, 1, d)
    shift1, scale1, gate1, shift2, scale2, gate2 = (mod[:, i] for i in range(N_MOD))

    h = rms_norm(x, norm1_g) * (1 + scale1) + shift1
    proj = h @ w_in
    cuts = [int(v) for v in np.cumsum([CONV_CH] * 3 + [SB_DIM] * 3 + [FOX_DIM] * 3)]
    cb, cc, cu, sq, sk, sv, fq, fk, fv, flog = jnp.split(proj, cuts, axis=-1)

    def heads(t, n):
        return t.reshape(b, s, n, HEAD_DIM).transpose(0, 2, 1, 3)

    y_conv = short_conv_mixer(cb, cc, cu, conv_w)
    y_sb = stick_breaking_attention(rms_norm(heads(sq, SB_HEADS), sb_q_g),
                                    rms_norm(heads(sk, SB_HEADS), sb_k_g), heads(sv, SB_HEADS))
    log_f = jax.nn.log_sigmoid((flog + fox_f_b).astype(jnp.float32)).transpose(0, 2, 1)
    y_fox = forgetting_attention(rms_norm(heads(fq, FOX_HEADS), fox_q_g),
                                 rms_norm(heads(fk, FOX_HEADS), fox_k_g), heads(fv, FOX_HEADS), log_f)
    y = jnp.concatenate([y_conv, y_sb, y_fox], axis=-1)
    y = _rms(y.reshape(b, s, D_MIX // HEAD_DIM, HEAD_DIM)).reshape(b, s, D_MIX) * out_norm_g
    x = x + gate1 * (y @ w_out)

    h2 = rms_norm(x, norm2_g) * (1 + scale2) + shift2
    return x + gate2 * moe_ffn(h2, router_w, router_b, w_mlp1, b_mlp1, w_mlp2, b_mlp2)


def setup_inputs(seed: int = 0) -> dict:
    key = jax.random.key(seed)
    ks = jax.random.split(key, 24)
    nrm = jax.random.normal
    f32 = jnp.float32
    L, D, E = DEPTH, D_MODEL, N_EXPERTS
    return {
        'x': nrm(ks[0], (BATCH, SEQ, D), f32),
        'c': nrm(ks[1], (BATCH, D), f32),
        'norm1_g': 1.0 + 0.05 * nrm(ks[2], (L, D), f32),
        'w_ada': 0.5 * D ** -0.5 * nrm(ks[3], (L, D, N_MOD * D), f32),
        'b_ada': 0.02 * nrm(ks[4], (L, N_MOD * D), f32),
        'w_in': D ** -0.5 * nrm(ks[5], (L, D, D_IN), f32),
        'conv_w': CONV_W ** -0.5 * nrm(ks[6], (L, CONV_W, CONV_CH), f32),
        'sb_q_g': 1.0 + 0.05 * nrm(ks[7], (L, HEAD_DIM), f32),
        'sb_k_g': 1.0 + 0.05 * nrm(ks[8], (L, HEAD_DIM), f32),
        'fox_q_g': 1.0 + 0.05 * nrm(ks[9], (L, HEAD_DIM), f32),
        'fox_k_g': 1.0 + 0.05 * nrm(ks[10], (L, HEAD_DIM), f32),
        'fox_f_b': 2.0 + 0.5 * nrm(ks[11], (L, FOX_HEADS), f32),
        'out_norm_g': 1.0 + 0.05 * nrm(ks[12], (L, D_MIX), f32),
        'w_out': D_MIX ** -0.5 * nrm(ks[13], (L, D_MIX, D), f32),
        'norm2_g': 1.0 + 0.05 * nrm(ks[14], (L, D), f32),
        'router_w': D ** -0.5 * nrm(ks[15], (L, D, E), f32),
        'router_b': 0.01 * nrm(ks[16], (L, E), f32),
        'w_mlp1': D ** -0.5 * nrm(ks[17], (L, E, D, 2 * D_EXPERT), f32),
        'b_mlp1': 0.01 * nrm(ks[18], (L, E, 2 * D_EXPERT), f32),
        'w_mlp2': D_EXPERT ** -0.5 * nrm(ks[19], (L, E, D_EXPERT, D), f32),
        'b_mlp2': 0.01 * nrm(ks[20], (L, E, D), f32),
    }


def reference(x, c, norm1_g, w_ada, b_ada, w_in, conv_w, sb_q_g, sb_k_g, fox_q_g, fox_k_g, fox_f_b,
              out_norm_g, w_out, norm2_g, router_w, router_b, w_mlp1, b_mlp1, w_mlp2, b_mlp2):
    for l in range(DEPTH):
        x = hybrid_layer(x, c, norm1_g[l], w_ada[l], b_ada[l], w_in[l], conv_w[l], sb_q_g[l], sb_k_g[l],
                         fox_q_g[l], fox_k_g[l], fox_f_b[l], out_norm_g[l], w_out[l], norm2_g[l],
                         router_w[l], router_b[l], w_mlp1[l], b_mlp1[l], w_mlp2[l], b_mlp2[l])
    return x
```

```python
import functools

import numpy as np
import jax
import jax.numpy as jnp
from jax import lax
from jax.experimental import pallas as pl
from jax.experimental.pallas import tpu as pltpu

F32 = jnp.float32
BF16 = jnp.bfloat16

D_MODEL = 1024
HEAD_DIM = 64
CONV_CH = 256
CONV_W = 3
SB_DIM = 384
FOX_DIM = 384
FOX_HEADS = 6
D_MIX = 1024
N_MOD = 6
N_EXPERTS = 32
TOP_K = 4
D_EXPERT = 1024
SWIGLU_ALPHA = 1.702
SWIGLU_LIMIT = 7.0
EPS = 1e-6

LANES = 128
SUBLANES = 8
ROW_TILES = D_MODEL // LANES
D_IN_PAD = 3 * CONV_CH + 3 * SB_DIM + 3 * FOX_DIM + LANES
COL_SBQ = 3 * CONV_CH
COL_SBK = COL_SBQ + SB_DIM
COL_SBV = COL_SBK + SB_DIM
COL_FQ = COL_SBV + SB_DIM
COL_FK = COL_FQ + FOX_DIM
COL_FV = COL_FK + FOX_DIM
COL_FLOG = COL_FV + FOX_DIM

TM_PROJ = 512
TQ = 256
TM_ROWS = 256
TM_GROUP = 512
EXP_ZERO = -104.0
VMEM_LIMIT = 48 * 1024 * 1024


def _cparams(n_axes, vmem=VMEM_LIMIT):
    return pltpu.CompilerParams(dimension_semantics=("arbitrary",) * n_axes, vmem_limit_bytes=vmem)


def _dot(a, b):
    return jnp.dot(a, b, preferred_element_type=F32)


def _dot_nt(a, b):
    return lax.dot_general(a, b, (((1,), (1,)), ((), ())), preferred_element_type=F32)


def _split2(x):
    hi = x.astype(BF16)
    lo = (x - hi.astype(F32)).astype(BF16)
    return hi, lo


def _log_sigmoid(x):
    return jnp.minimum(x, 0.0) - jnp.log(1.0 + jnp.exp(-jnp.abs(x)))


def _mod_kernel(c_ref, w_ref, b_ref, o_ref):
    c = c_ref[...]
    s = c / (1.0 + jnp.exp(-c))
    o_ref[...] = _dot(s.astype(BF16), w_ref[...].astype(BF16)) + b_ref[...]


def _modulation(c, w_ada, b_ada):
    b, d = c.shape
    n = w_ada.shape[1]
    tn = 1024
    return pl.pallas_call(
        _mod_kernel,
        grid=(n // tn,),
        in_specs=[pl.BlockSpec((b, d), lambda j: (0, 0)),
                  pl.BlockSpec((d, tn), lambda j: (0, j)),
                  pl.BlockSpec((1, tn), lambda j: (0, j))],
        out_specs=pl.BlockSpec((b, tn), lambda j: (0, j)),
        out_shape=jax.ShapeDtypeStruct((b, n), F32),
        compiler_params=_cparams(1),
        name="modulation",
    )(c, w_ada, b_ada.reshape(1, n))


def _inproj_kernel(tiles_per_seq, x_ref, mod_ref, g1_ref, w_ref, bd_ref, tri_ref, fb_ref, gains_ref,
                   cb_ref, v_ref, sbq_ref, sbk_ref, sbv_ref, fq_ref, fk_ref, fv_ref, cum_ref, carry_ref):
    tm = x_ref.shape[0]
    x = x_ref[...]
    ms = jnp.mean(x * x, axis=-1, keepdims=True)
    h = (x * lax.rsqrt(ms + EPS)) * g1_ref[...]
    h = h * (1.0 + mod_ref[0, 1:2, :]) + mod_ref[0, 0:1, :]
    hb = h.astype(BF16)

    pc = _dot(hb, w_ref[:, 0:3 * CONV_CH])
    cb_ref[...] = pc[:, 0:CONV_CH]
    v_ref[...] = pc[:, CONV_CH:2 * CONV_CH] * pc[:, 2 * CONV_CH:3 * CONV_CH]

    def normed(col, row):
        p = _dot(hb, w_ref[:, col:col + SB_DIM])
        ss = _dot((p * p).astype(BF16), bd_ref[...])
        return (p * lax.rsqrt(ss * (1.0 / HEAD_DIM) + EPS) * gains_ref[row:row + 1, :]).astype(BF16)

    sbq_ref[...] = normed(COL_SBQ, 0)
    sbk_ref[...] = normed(COL_SBK, 1)
    sbv_ref[...] = _dot(hb, w_ref[:, COL_SBV:COL_SBV + SB_DIM]).astype(BF16)
    fq_ref[...] = normed(COL_FQ, 2)
    fk_ref[...] = normed(COL_FK, 3)
    fv_ref[...] = _dot(hb, w_ref[:, COL_FV:COL_FV + FOX_DIM]).astype(BF16)

    lf = _log_sigmoid(_dot(hb, w_ref[:, COL_FLOG:COL_FLOG + LANES]) + fb_ref[...])
    t0 = lf.astype(BF16)
    r1 = lf - t0.astype(F32)
    t1 = r1.astype(BF16)
    t2 = (r1 - t1.astype(F32)).astype(BF16)
    tri = tri_ref[...]
    cs = _dot(tri, t0) + _dot(tri, t1) + _dot(tri, t2)

    @pl.when(pl.program_id(0) % tiles_per_seq == 0)
    def _():
        carry_ref[...] = jnp.zeros_like(carry_ref)

    cum = cs + carry_ref[0:1, :]
    carry_ref[0:1, :] = cum[tm - 1:tm, :]
    cum_ref[0] = cum.T[0:SUBLANES, :]


def _inproj(x2, mod3, g1, w_in_p, bd, tri, fb, gains, seq):
    t = x2.shape[0]
    tm = TM_PROJ
    tiles_per_seq = seq // tm
    nb = t // seq
    row = lambda i: (i, 0)
    const = lambda i: (0, 0)
    outs = [jax.ShapeDtypeStruct((t, CONV_CH), F32), jax.ShapeDtypeStruct((t, CONV_CH), F32)]
    outs += [jax.ShapeDtypeStruct((t, SB_DIM), BF16)] * 6
    outs += [jax.ShapeDtypeStruct((nb, SUBLANES, seq), F32)]
    out_specs = [pl.BlockSpec((tm, CONV_CH), row)] * 2 + [pl.BlockSpec((tm, SB_DIM), row)] * 6
    out_specs += [pl.BlockSpec((1, SUBLANES, tm), lambda i: (i // tiles_per_seq, 0, i % tiles_per_seq))]
    return pl.pallas_call(
        functools.partial(_inproj_kernel, tiles_per_seq),
        grid=(t // tm,),
        in_specs=[pl.BlockSpec((tm, D_MODEL), row),
                  pl.BlockSpec((1, N_MOD, D_MODEL), lambda i: (i // tiles_per_seq, 0, 0)),
                  pl.BlockSpec((1, D_MODEL), const),
                  pl.BlockSpec((D_MODEL, D_IN_PAD), const),
                  pl.BlockSpec((SB_DIM, SB_DIM), const),
                  pl.BlockSpec((tm, tm), const),
                  pl.BlockSpec((1, LANES), const),
                  pl.BlockSpec((SUBLANES, SB_DIM), const)],
        out_specs=out_specs,
        out_shape=outs,
        scratch_shapes=[pltpu.VMEM((SUBLANES, LANES), F32)],
        compiler_params=_cparams(1),
        name="inproj",
    )(x2, mod3, g1, w_in_p, bd, tri, fb, gains)


def _mask_heads(k_ref, v_ref, km_ref, vm_ref):
    lane = lax.broadcasted_iota(jnp.int32, k_ref.shape[1:], 1)
    for h in range(2):
        keep = (lane // HEAD_DIM) == h
        km_ref[h] = jnp.where(keep, k_ref[0], jnp.zeros_like(k_ref[0]))
        vm_ref[h] = jnp.where(keep, v_ref[0], jnp.zeros_like(v_ref[0]))


def _sb_kernel(q_ref, k_ref, v_ref, u_ref, o_ref, km_ref, vm_ref):
    tq = q_ref.shape[1]
    qi = pl.program_id(2)

    @pl.when(qi == 0)
    def _():
        _mask_heads(k_ref, v_ref, km_ref, vm_ref)

    q = q_ref[0]
    u = u_ref[...]
    row = lax.broadcasted_iota(jnp.int32, (tq, tq), 0)
    col = lax.broadcasted_iota(jnp.int32, (tq, tq), 1)
    strict = col < row
    out = jnp.zeros((tq, LANES), F32)

    for h in range(2):
        def block(kb, rsum, acc, diag):
            ks = pl.multiple_of(kb * tq, tq)
            z = _dot_nt(q, km_ref[h, pl.ds(ks, tq), :])
            sp = jnp.log(1.0 + jnp.exp(-jnp.abs(z)))
            log_beta = jnp.minimum(z, 0.0) - sp
            log_1mb = -jnp.maximum(z, 0.0) - sp
            if diag:
                log_1mb = jnp.where(strict, log_1mb, 0.0)
            hi, lo = _split2(log_1mb)
            rest = _dot(hi, u) + _dot(lo, u) + rsum
            w = jnp.exp(log_beta + rest)
            if diag:
                w = jnp.where(strict, w, 0.0)
            acc = acc + _dot(w.astype(BF16), vm_ref[h, pl.ds(ks, tq), :])
            rsum = rsum + jnp.sum(log_1mb, axis=-1, keepdims=True)
            return rsum, acc

        rsum, acc = block(qi, jnp.zeros((tq, 1), F32), jnp.zeros((tq, LANES), F32), True)

        def cond(c):
            return jnp.logical_and(c[0] >= 0, c[3] > 0)

        def body(c):
            kb, rsum, acc, _ = c
            rsum, acc = block(kb, rsum, acc, False)
            go = (jnp.max(rsum) > EXP_ZERO).astype(jnp.int32)
            return kb - 1, rsum, acc, go

        go0 = (jnp.max(rsum) > EXP_ZERO).astype(jnp.int32)
        _, _, acc, _ = lax.while_loop(cond, body, (qi - 1, rsum, acc, go0))
        out = out + acc

    o_ref[0] = out


def _sb_attention(q, k, v, u):
    b, s, _ = q.shape
    tq = TQ
    qspec = pl.BlockSpec((1, tq, LANES), lambda bi, hp, qi: (bi, qi, hp))
    kvspec = pl.BlockSpec((1, s, LANES), lambda bi, hp, qi: (bi, 0, hp))
    return pl.pallas_call(
        _sb_kernel,
        grid=(b, SB_DIM // LANES, s // tq),
        in_specs=[qspec, kvspec, kvspec, pl.BlockSpec((tq, tq), lambda bi, hp, qi: (0, 0))],
        out_specs=qspec,
        out_shape=jax.ShapeDtypeStruct((b, s, SB_DIM), F32),
        scratch_shapes=[pltpu.VMEM((2, s, LANES), BF16), pltpu.VMEM((2, s, LANES), BF16)],
        compiler_params=_cparams(3),
        name="sb_attention",
    )(q, k, v, u)


def _fox_kernel(n_kb, fend_ref, zb_ref, q_ref, k_ref, v_ref, cum_ref, o_ref, km_ref, vm_ref):
    tq = q_ref.shape[1]
    bi = pl.program_id(0)
    hp = pl.program_id(1)
    qi = pl.program_id(2)

    @pl.when(qi == 0)
    def _():
        _mask_heads(k_ref, v_ref, km_ref, vm_ref)

    q = q_ref[0]
    row = lax.broadcasted_iota(jnp.int32, (tq, tq), 0)
    col = lax.broadcasted_iota(jnp.int32, (tq, tq), 1)
    causal = col <= row
    zbound = zb_ref[0]
    out = jnp.zeros((tq, LANES), F32)

    for h in range(2):
        base = (bi * FOX_HEADS + hp * 2 + h) * n_kb

        def block(kb, m, l, acc, diag):
            ks = pl.multiple_of(kb * tq, tq)
            s = _dot_nt(q, km_ref[h, pl.ds(ks, tq), :]) - cum_ref[0, 0, h:h + 1, pl.ds(ks, tq)]
            if diag:
                s = jnp.where(causal, s, -1e30)
            m_new = jnp.maximum(m, jnp.max(s, axis=-1, keepdims=True))
            p = jnp.exp(s - m_new)
            alpha = jnp.exp(m - m_new)
            l = alpha * l + jnp.sum(p, axis=-1, keepdims=True)
            acc = alpha * acc + _dot(p.astype(BF16), vm_ref[h, pl.ds(ks, tq), :])
            return m_new, l, acc

        m, l, acc = block(qi, jnp.full((tq, 1), -1e30, F32), jnp.zeros((tq, 1), F32),
                          jnp.zeros((tq, LANES), F32), True)

        f_q = fend_ref[base + jnp.maximum(qi - 1, 0)]

        def keep(kb):
            return jnp.logical_and(kb > 0, zbound + f_q - fend_ref[base + jnp.maximum(kb - 1, 0)] >= EXP_ZERO)

        kb_lo = lax.while_loop(keep, lambda kb: kb - 1, qi)

        def body(kb, c):
            return block(kb, *c, False)

        m, l, acc = lax.fori_loop(kb_lo, qi, body, (m, l, acc))
        out = out + acc / l

    o_ref[0] = out


def _fox_attention(q, k, v, cum4, fend, zbound):
    b, s, _ = q.shape
    tq = TQ
    n_kb = s // tq
    qspec = pl.BlockSpec((1, tq, LANES), lambda bi, hp, qi, *_: (bi, qi, hp))
    kvspec = pl.BlockSpec((1, s, LANES), lambda bi, hp, qi, *_: (bi, 0, hp))
    grid_spec = pltpu.PrefetchScalarGridSpec(
        num_scalar_prefetch=2,
        grid=(b, FOX_DIM // LANES, n_kb),
        in_specs=[qspec, kvspec, kvspec,
                  pl.BlockSpec((1, 1, SUBLANES, s), lambda bi, hp, qi, *_: (bi, hp, 0, 0))],
        out_specs=qspec,
        scratch_shapes=[pltpu.VMEM((2, s, LANES), BF16), pltpu.VMEM((2, s, LANES), BF16)],
    )
    return pl.pallas_call(
        functools.partial(_fox_kernel, n_kb),
        grid_spec=grid_spec,
        out_shape=jax.ShapeDtypeStruct((b, s, FOX_DIM), F32),
        compiler_params=_cparams(3),
        name="fox_attention",
    )(fend, zbound, q, k, v, cum4)


def _outproj_kernel(tiles_per_seq, x_ref, cb_ref, v_ref, vprev_ref, ysb_ref, yfox_ref, cw_ref, og_ref, wout_ref,
                    mod_ref, g2_ref, rw_ref, rb_ref, bd_ref, ltri_ref,
                    x1_ref, h2t_ref, sel_ref, gate_ref, cnt_ref, vext_ref, carry_ref):
    tm = x_ref.shape[0]
    i = pl.program_id(0)

    @pl.when(i == 0)
    def _():
        carry_ref[...] = jnp.zeros_like(carry_ref)

    first = (i % tiles_per_seq) == 0
    vext_ref[0:SUBLANES, :] = jnp.where(first, 0.0, vprev_ref[...])
    vext_ref[SUBLANES:SUBLANES + tm, :] = v_ref[...]
    conv = (cw_ref[0:1, :] * vext_ref[SUBLANES - 2:SUBLANES - 2 + tm, :]
            + cw_ref[1:2, :] * vext_ref[SUBLANES - 1:SUBLANES - 1 + tm, :]
            + cw_ref[2:3, :] * v_ref[...])
    y = jnp.concatenate([cb_ref[...] * conv, ysb_ref[...], yfox_ref[...]], axis=-1)

    ysq = (y * y).astype(BF16)
    bd = bd_ref[...]
    ss = jnp.concatenate([_dot(ysq[:, c:c + 2 * LANES], bd) for c in range(0, D_MIX, 2 * LANES)], axis=-1)
    yn = y * lax.rsqrt(ss * (1.0 / HEAD_DIM) + EPS) * og_ref[...]
    x1 = x_ref[...] + mod_ref[0, 2:3, :] * _dot(yn.astype(BF16), wout_ref[...])
    x1_ref[...] = x1

    ms = jnp.mean(x1 * x1, axis=-1, keepdims=True)
    h2 = (x1 * lax.rsqrt(ms + EPS)) * g2_ref[...]
    h2 = h2 * (1.0 + mod_ref[0, 4:5, :]) + mod_ref[0, 3:4, :]
    for j in range(ROW_TILES):
        h2t_ref[pl.ds(j, tm, stride=SUBLANES), :] = h2[:, j * LANES:(j + 1) * LANES]

    logits = jnp.dot(h2, rw_ref[...], precision=lax.Precision.HIGHEST, preferred_element_type=F32) + rb_ref[...]
    lane = lax.broadcasted_iota(jnp.int32, (tm, LANES), 1)
    lane_f = lane.astype(F32)
    neg = jnp.float32(-jnp.inf)
    cur = jnp.where(lane < N_EXPERTS, logits, neg)
    tops, sels = [], []
    for _ in range(TOP_K):
        mk = jnp.max(cur, axis=-1, keepdims=True)
        ik = jnp.min(jnp.where(cur == mk, lane_f, float(LANES)), axis=-1, keepdims=True)
        sel = lane_f == ik
        cur = jnp.where(sel, neg, cur)
        tops.append(mk)
        sels.append((sel, ik.astype(jnp.int32)))
    es = [jnp.exp(t - tops[0]) for t in tops]
    inv = 1.0 / (es[0] + es[1] + es[2] + es[3])

    multi = jnp.zeros((tm, LANES), F32)
    for sel, _ in sels:
        multi = jnp.where(sel, 1.0, multi)
    before = _dot(ltri_ref[...], multi.astype(BF16)) + carry_ref[0:1, :]
    carry_ref[0:1, :] = before[tm - 1:tm, :] + multi[tm - 1:tm, :]

    sel_out = jnp.zeros((tm, LANES), jnp.int32)
    gate_out = jnp.zeros((tm, LANES), F32)
    for k, (sel, ik) in enumerate(sels):
        rank = jnp.sum(jnp.where(sel, before, 0.0), axis=-1, keepdims=True).astype(jnp.int32)
        sel_out = jnp.where(lane == k, ik, sel_out)
        sel_out = jnp.where(lane == TOP_K + k, rank, sel_out)
        gate_out = jnp.where(lane == k, es[k] * inv, gate_out)
    sel_ref[...] = sel_out
    gate_ref[...] = gate_out
    cnt_ref[...] = jnp.broadcast_to(carry_ref[0:1, :], cnt_ref.shape)


def _outproj(x2, cb, v, ysb, yfox, conv_w, og, w_out_b, mod3, g2, rw, rb, bd2, ltri, seq):
    t = x2.shape[0]
    tm = TM_PROJ
    tiles_per_seq = seq // tm
    row = lambda i: (i, 0)
    const = lambda i: (0, 0)
    return pl.pallas_call(
        functools.partial(_outproj_kernel, tiles_per_seq),
        grid=(t // tm,),
        in_specs=[pl.BlockSpec((tm, D_MODEL), row),
                  pl.BlockSpec((tm, CONV_CH), row),
                  pl.BlockSpec((tm, CONV_CH), row),
                  pl.BlockSpec((SUBLANES, CONV_CH), lambda i: (jnp.maximum(i * (tm // SUBLANES) - 1, 0), 0)),
                  pl.BlockSpec((tm, SB_DIM), row),
                  pl.BlockSpec((tm, FOX_DIM), row),
                  pl.BlockSpec((CONV_W, CONV_CH), const),
                  pl.BlockSpec((1, D_MIX), const),
                  pl.BlockSpec((D_MIX, D_MODEL), const),
                  pl.BlockSpec((1, N_MOD, D_MODEL), lambda i: (i // tiles_per_seq, 0, 0)),
                  pl.BlockSpec((1, D_MODEL), const),
                  pl.BlockSpec((D_MODEL, LANES), const),
                  pl.BlockSpec((1, LANES), const),
                  pl.BlockSpec((2 * LANES, 2 * LANES), const),
                  pl.BlockSpec((tm, tm), const)],
        out_specs=[pl.BlockSpec((tm, D_MODEL), row),
                   pl.BlockSpec((tm * SUBLANES, LANES), row),
                   pl.BlockSpec((tm, LANES), row),
                   pl.BlockSpec((tm, LANES), row),
                   pl.BlockSpec((SUBLANES, LANES), const)],
        out_shape=[jax.ShapeDtypeStruct((t, D_MODEL), F32),
                   jax.ShapeDtypeStruct((t * SUBLANES, LANES), F32),
                   jax.ShapeDtypeStruct((t, LANES), jnp.int32),
                   jax.ShapeDtypeStruct((t, LANES), F32),
                   jax.ShapeDtypeStruct((SUBLANES, LANES), F32)],
        scratch_shapes=[pltpu.VMEM((tm + SUBLANES, CONV_CH), F32), pltpu.VMEM((SUBLANES, LANES), F32)],
        compiler_params=_cparams(1),
        name="outproj_router",
    )(x2, cb, v, v, ysb, yfox, conv_w, og, w_out_b, mod3, g2, rw, rb, bd2, ltri)


def _row_copy(src, src_row, dst, dst_row, sem):
    return pltpu.make_async_copy(src.at[pl.ds(pl.multiple_of(src_row * SUBLANES, SUBLANES), SUBLANES), :],
                                 dst.at[pl.ds(pl.multiple_of(dst_row * SUBLANES, SUBLANES), SUBLANES), :], sem)


def _zero_fill(meta_ref, zero_ref, xg_ref, zsem):
    n_tiles = xg_ref.shape[0] // (TM_GROUP * SUBLANES)
    zero_ref[...] = jnp.zeros_like(zero_ref)

    def sweep(wait):
        def per_expert(e, c):
            pos = meta_ref[N_EXPERTS + e]
            n = meta_ref[2 * N_EXPERTS + e]
            bit = TM_GROUP // 2
            while bit:
                @pl.when((n & bit) != 0)
                def _(pos=pos, bit=bit):
                    cp = pltpu.make_async_copy(
                        zero_ref.at[pl.ds(0, bit * SUBLANES), :],
                        xg_ref.at[pl.ds(pl.multiple_of(pos * SUBLANES, SUBLANES), bit * SUBLANES), :], zsem)
                    cp.wait() if wait else cp.start()
                pos = pos + (n & bit)
                bit //= 2
            return c

        lax.fori_loop(0, N_EXPERTS, per_expert, 0)

        def per_tile(j, c):
            rows = TM_GROUP * SUBLANES
            cp = pltpu.make_async_copy(zero_ref, xg_ref.at[pl.ds(pl.multiple_of(j * rows, rows), rows), :], zsem)
            cp.wait() if wait else cp.start()
            return c

        lax.fori_loop(meta_ref[3 * N_EXPERTS], n_tiles, per_tile, 0)

    sweep(False)
    sweep(True)


def _dispatch_kernel(off_ref, eidx_ref, rank_ref, h_ref, xg_ref, zero_ref, sem, zsem):
    tm = h_ref.shape[0] // SUBLANES

    @pl.when(pl.program_id(0) == 0)
    def _():
        _zero_fill(off_ref, zero_ref, xg_ref, zsem)

    def issue(r, c):
        for k in range(TOP_K):
            p = off_ref[eidx_ref[r * TOP_K + k]] + rank_ref[r * TOP_K + k]
            _row_copy(h_ref, r, xg_ref, p, sem).start()
        return c

    lax.fori_loop(0, tm, issue, 0)
    n = tm * TOP_K * SUBLANES
    pltpu.make_async_copy(xg_ref.at[pl.ds(0, n), :], xg_ref.at[pl.ds(0, n), :], sem).wait()


def _dispatch(off, eidx, rank, h2t, n_rows):
    t = h2t.shape[0] // SUBLANES
    tm = TM_ROWS
    grid_spec = pltpu.PrefetchScalarGridSpec(
        num_scalar_prefetch=1,
        grid=(t // tm,),
        in_specs=[pl.BlockSpec((tm * TOP_K,), lambda i, *_: (i,), memory_space=pltpu.SMEM),
                  pl.BlockSpec((tm * TOP_K,), lambda i, *_: (i,), memory_space=pltpu.SMEM),
                  pl.BlockSpec((tm * SUBLANES, LANES), lambda i, *_: (i, 0))],
        out_specs=pl.BlockSpec(memory_space=pl.ANY),
        scratch_shapes=[pltpu.VMEM((TM_GROUP * SUBLANES, LANES), F32),
                        pltpu.SemaphoreType.DMA(()), pltpu.SemaphoreType.DMA(())],
    )
    return pl.pallas_call(
        _dispatch_kernel,
        grid_spec=grid_spec,
        out_shape=jax.ShapeDtypeStruct((n_rows * SUBLANES, LANES), F32),
        compiler_params=_cparams(1),
        name="moe_dispatch",
    )(off, eidx, rank, h2t)


def _expert_kernel(te_ref, nv_ref, xg_ref, w1_ref, b1_ref, w2_ref, b2_ref, yg_ref):
    tmg = xg_ref.shape[0] // SUBLANES
    nv = nv_ref[pl.program_id(0)]

    @pl.when(nv == 0)
    def _():
        yg_ref[...] = jnp.zeros_like(yg_ref)

    @pl.when(nv > 0)
    def _():
        x = jnp.concatenate([xg_ref[pl.ds(c, tmg, stride=SUBLANES), :] for c in range(ROW_TILES)], axis=-1)
        u = _dot(x.astype(BF16), w1_ref[0]) + b1_ref[0]
        glu = jnp.minimum(u[:, :D_EXPERT], SWIGLU_LIMIT)
        lin = jnp.clip(u[:, D_EXPERT:], -SWIGLU_LIMIT, SWIGLU_LIMIT)
        a = glu * (1.0 / (1.0 + jnp.exp(-SWIGLU_ALPHA * glu))) * (lin + 1.0)
        y = _dot(a.astype(BF16), w2_ref[0]) + b2_ref[0]
        for c in range(ROW_TILES):
            yg_ref[pl.ds(c, tmg, stride=SUBLANES), :] = y[:, c * LANES:(c + 1) * LANES]


def _experts(te, nv, xg, w1, b1, w2, b2):
    tmg = TM_GROUP
    n_tiles = xg.shape[0] // (tmg * SUBLANES)
    grid_spec = pltpu.PrefetchScalarGridSpec(
        num_scalar_prefetch=2,
        grid=(n_tiles,),
        in_specs=[pl.BlockSpec((tmg * SUBLANES, LANES), lambda j, te, nv: (j, 0)),
                  pl.BlockSpec((1, D_MODEL, 2 * D_EXPERT), lambda j, te, nv: (te[j], 0, 0)),
                  pl.BlockSpec((1, 1, 2 * D_EXPERT), lambda j, te, nv: (te[j], 0, 0)),
                  pl.BlockSpec((1, D_EXPERT, D_MODEL), lambda j, te, nv: (te[j], 0, 0)),
                  pl.BlockSpec((1, 1, D_MODEL), lambda j, te, nv: (te[j], 0, 0))],
        out_specs=pl.BlockSpec((tmg * SUBLANES, LANES), lambda j, te, nv: (j, 0)),
    )
    return pl.pallas_call(
        _expert_kernel,
        grid_spec=grid_spec,
        out_shape=jax.ShapeDtypeStruct(xg.shape, F32),
        compiler_params=_cparams(1),
        name="moe_experts",
    )(te, nv, xg, w1, b1, w2, b2)


def _combine_kernel(tiles_per_seq, off_ref, eidx_ref, rank_ref, gate_ref, x1_ref, mod_ref, yg_ref, o_ref, buf_ref, sem):
    tm = x1_ref.shape[0]

    def issue(r, c):
        for k in range(TOP_K):
            p = off_ref[eidx_ref[r * TOP_K + k]] + rank_ref[r * TOP_K + k]
            _row_copy(yg_ref, p, buf_ref.at[k], r, sem).start()
        return c

    lax.fori_loop(0, tm, issue, 0)
    for k in range(TOP_K):
        pltpu.make_async_copy(buf_ref.at[k], buf_ref.at[k], sem).wait()

    gates = gate_ref[...]
    g2 = mod_ref[0, 5:6, :]
    for c in range(ROW_TILES):
        acc = jnp.zeros((tm, LANES), F32)
        for k in range(TOP_K):
            acc = acc + gates[:, k:k + 1] * buf_ref[k, pl.ds(c, tm, stride=SUBLANES), :]
        cols = slice(c * LANES, (c + 1) * LANES)
        o_ref[:, cols] = x1_ref[:, cols] + g2[:, cols] * acc


def _combine(off, eidx, rank, gates, x1, mod3, yg, seq):
    t = x1.shape[0]
    tm = TM_ROWS
    tiles_per_seq = seq // tm
    grid_spec = pltpu.PrefetchScalarGridSpec(
        num_scalar_prefetch=1,
        grid=(t // tm,),
        in_specs=[pl.BlockSpec((tm * TOP_K,), lambda i, *_: (i,), memory_space=pltpu.SMEM),
                  pl.BlockSpec((tm * TOP_K,), lambda i, *_: (i,), memory_space=pltpu.SMEM),
                  pl.BlockSpec((tm, LANES), lambda i, *_: (i, 0)),
                  pl.BlockSpec((tm, D_MODEL), lambda i, *_: (i, 0)),
                  pl.BlockSpec((1, N_MOD, D_MODEL), lambda i, *_: (i // tiles_per_seq, 0, 0)),
                  pl.BlockSpec(memory_space=pl.ANY)],
        out_specs=pl.BlockSpec((tm, D_MODEL), lambda i, *_: (i, 0)),
        scratch_shapes=[pltpu.VMEM((TOP_K, tm * SUBLANES, LANES), F32), pltpu.SemaphoreType.DMA(())],
    )
    return pl.pallas_call(
        functools.partial(_combine_kernel, tiles_per_seq),
        grid_spec=grid_spec,
        out_shape=jax.ShapeDtypeStruct((t, D_MODEL), F32),
        compiler_params=_cparams(1),
        name="moe_combine",
    )(off, eidx, rank, gates, x1, mod3, yg)


def _block_diag_ones(n):
    i = np.arange(n)
    return jnp.asarray((i[:, None] // HEAD_DIM) == (i[None, :] // HEAD_DIM), dtype=BF16)


def _tri(n, strict, upper):
    i = np.arange(n)
    if upper:
        m = (i[:, None] > i[None, :]) if strict else (i[:, None] >= i[None, :])
        return jnp.asarray(m, dtype=BF16)
    m = (i[None, :] < i[:, None]) if strict else (i[None, :] <= i[:, None])
    return jnp.asarray(m, dtype=BF16)


def _tile2(g):
    return jnp.tile(g, SB_DIM // HEAD_DIM)


def _layer(x2, c, seq, norm1_g, w_ada, b_ada, w_in, conv_w, sb_q_g, sb_k_g, fox_q_g, fox_k_g, fox_f_b,
           out_norm_g, w_out, norm2_g, router_w, router_b, w_mlp1, b_mlp1, w_mlp2, b_mlp2):
    t = x2.shape[0]
    nb = t // seq
    scale = HEAD_DIM ** -0.5

    w_in_p = jnp.pad(w_in, ((0, 0), (0, D_IN_PAD - w_in.shape[1]))).astype(BF16)
    fb = jnp.pad(fox_f_b, (0, LANES - FOX_HEADS)).reshape(1, LANES)
    gains = jnp.stack([_tile2(sb_q_g) * scale, _tile2(sb_k_g), _tile2(fox_q_g) * scale, _tile2(fox_k_g)])
    gains = jnp.pad(gains, ((0, SUBLANES - 4), (0, 0)))
    rw = jnp.pad(router_w, ((0, 0), (0, LANES - N_EXPERTS)))
    rb = jnp.pad(router_b, (0, LANES - N_EXPERTS)).reshape(1, LANES)
    w1 = jnp.concatenate([w_mlp1[:, :, 0::2], w_mlp1[:, :, 1::2]], axis=-1).astype(BF16)
    b1 = jnp.concatenate([b_mlp1[:, 0::2], b_mlp1[:, 1::2]], axis=-1).reshape(N_EXPERTS, 1, 2 * D_EXPERT)
    w2 = w_mlp2.astype(BF16)
    b2 = b_mlp2.reshape(N_EXPERTS, 1, D_MODEL)

    mod3 = _modulation(c, w_ada, b_ada).reshape(nb, N_MOD, D_MODEL)

    cb, v, sbq, sbk, sbv, fq, fk, fv, cum = _inproj(
        x2, mod3, norm1_g.reshape(1, D_MODEL), w_in_p, _block_diag_ones(SB_DIM),
        _tri(TM_PROJ, strict=False, upper=False), fb, gains, seq)

    r3 = lambda a: a.reshape(nb, seq, SB_DIM)
    ysb = _sb_attention(r3(sbq), r3(sbk), r3(sbv), _tri(TQ, strict=True, upper=True))

    n_kb = seq // TQ
    cum4 = jnp.pad(cum[:, :FOX_HEADS, :].reshape(nb, FOX_DIM // LANES, 2, seq),
                   ((0, 0), (0, 0), (0, SUBLANES - 2), (0, 0)))
    fend = cum[:, :FOX_HEADS, TQ - 1::TQ].reshape(nb * FOX_HEADS * n_kb)
    zbound = (2.0 * 1.02 * HEAD_DIM * scale * jnp.max(jnp.abs(fox_q_g)) * jnp.max(jnp.abs(fox_k_g))).reshape(1)
    yfox = _fox_attention(r3(fq), r3(fk), r3(fv), cum4, fend, zbound)

    x1, h2t, sel, gates, cnt = _outproj(
        x2, cb, v, ysb.reshape(t, SB_DIM), yfox.reshape(t, FOX_DIM), conv_w, out_norm_g.reshape(1, D_MIX),
        w_out.astype(BF16), mod3, norm2_g.reshape(1, D_MODEL), rw, rb, _block_diag_ones(2 * LANES),
        _tri(TM_PROJ, strict=True, upper=False), seq)

    counts = cnt[0, :N_EXPERTS].astype(jnp.int32)
    padded = ((counts + TM_GROUP - 1) // TM_GROUP) * TM_GROUP
    off_end = jnp.cumsum(padded)
    off = off_end - padded
    n_rows = t * TOP_K + N_EXPERTS * TM_GROUP
    n_tiles = n_rows // TM_GROUP
    start = jnp.arange(n_tiles, dtype=jnp.int32) * TM_GROUP
    te = jnp.minimum(jnp.searchsorted(off_end, start, side="right"), N_EXPERTS - 1).astype(jnp.int32)
    nv = jnp.clip(off[te] + counts[te] - start, 0, TM_GROUP).astype(jnp.int32)
    eidx = sel[:, :TOP_K].reshape(t * TOP_K)
    rank = sel[:, TOP_K:2 * TOP_K].reshape(t * TOP_K)
    meta = jnp.concatenate([off, off + counts, padded - counts, off_end[-1:] // TM_GROUP]).astype(jnp.int32)

    xg = _dispatch(meta, eidx, rank, h2t, n_rows)
    yg = _experts(te, nv, xg, w1, b1, w2, b2)
    return _combine(meta, eidx, rank, gates, x1, mod3, yg, seq)


def kernel(x, c, norm1_g, w_ada, b_ada, w_in, conv_w, sb_q_g, sb_k_g, fox_q_g, fox_k_g, fox_f_b, out_norm_g, w_out,
           norm2_g, router_w, router_b, w_mlp1, b_mlp1, w_mlp2, b_mlp2):
    b, s, d = x.shape
    x2 = x.reshape(b * s, d)
    params = (norm1_g, w_ada, b_ada, w_in, conv_w, sb_q_g, sb_k_g, fox_q_g, fox_k_g, fox_f_b, out_norm_g, w_out,
              norm2_g, router_w, router_b, w_mlp1, b_mlp1, w_mlp2, b_mlp2)
    for layer in range(norm1_g.shape[0]):
        x2 = _layer(x2, c, s, *(p[layer] for p in params))
    return x2.reshape(b, s, d)
```

```python
import functools

import numpy as np
import jax
import jax.numpy as jnp
from jax import lax
from jax.experimental import pallas as pl
from jax.experimental.pallas import tpu as pltpu

F32 = jnp.float32
BF16 = jnp.bfloat16

D_MODEL = 1024
HEAD_DIM = 64
CONV_CH = 256
CONV_W = 3
SB_DIM = 384
FOX_DIM = 384
FOX_HEADS = 6
D_MIX = 1024
N_MOD = 6
N_EXPERTS = 32
TOP_K = 4
D_EXPERT = 1024
SWIGLU_ALPHA = 1.702
SWIGLU_LIMIT = 7.0
EPS = 1e-6

LANES = 128
SUBLANES = 8
ROW_TILES = D_MODEL // LANES
D_IN_PAD = 3 * CONV_CH + 3 * SB_DIM + 3 * FOX_DIM + LANES
COL_SBQ = 3 * CONV_CH
COL_SBK = COL_SBQ + SB_DIM
COL_SBV = COL_SBK + SB_DIM
COL_FQ = COL_SBV + SB_DIM
COL_FK = COL_FQ + FOX_DIM
COL_FV = COL_FK + FOX_DIM
COL_FLOG = COL_FV + FOX_DIM

TM_PROJ = 512
TQ = 256
TM_ROWS = 256
TM_GROUP = 512
LOG2_E = 1.4426950408889634
LOG2_NEGLIGIBLE = -64.0
VMEM_LIMIT = 48 * 1024 * 1024
EXPERT_VMEM_LIMIT = 58 * 1024 * 1024


def _cparams(n_axes, vmem=VMEM_LIMIT):
    return pltpu.CompilerParams(dimension_semantics=("arbitrary",) * n_axes, vmem_limit_bytes=vmem)


def _dot(a, b):
    return jnp.dot(a, b, preferred_element_type=F32)


def _dot_nt(a, b):
    return lax.dot_general(a, b, (((1,), (1,)), ((), ())), preferred_element_type=F32)


def _split2(x):
    hi = x.astype(BF16)
    lo = (x - hi.astype(F32)).astype(BF16)
    return hi, lo


def _log_sigmoid(x):
    return jnp.minimum(x, 0.0) - jnp.log(1.0 + jnp.exp(-jnp.abs(x)))


def _mod_kernel(c_ref, w_ref, b_ref, o_ref):
    c = c_ref[...]
    s = c / (1.0 + jnp.exp(-c))
    o_ref[...] = _dot(s.astype(BF16), w_ref[...].astype(BF16)) + b_ref[...]


def _modulation(c, w_ada, b_ada):
    b, d = c.shape
    n = w_ada.shape[1]
    tn = 1024
    return pl.pallas_call(
        _mod_kernel,
        grid=(n // tn,),
        in_specs=[pl.BlockSpec((b, d), lambda j: (0, 0)),
                  pl.BlockSpec((d, tn), lambda j: (0, j)),
                  pl.BlockSpec((1, tn), lambda j: (0, j))],
        out_specs=pl.BlockSpec((b, tn), lambda j: (0, j)),
        out_shape=jax.ShapeDtypeStruct((b, n), F32),
        compiler_params=_cparams(1),
        name="modulation",
    )(c, w_ada, b_ada.reshape(1, n))


def _inproj_kernel(tiles_per_seq, x_ref, mod_ref, g1_ref, w_ref, bd_ref, tri_ref, fb_ref, gains_ref,
                   cb_ref, v_ref, sbq_ref, sbk_ref, sbv_ref, fq_ref, fk_ref, fv_ref, cum_ref, carry_ref):
    tm = x_ref.shape[0]
    x = x_ref[...]
    ms = jnp.mean(x * x, axis=-1, keepdims=True)
    h = (x * lax.rsqrt(ms + EPS)) * g1_ref[...]
    h = h * (1.0 + mod_ref[0, 1:2, :]) + mod_ref[0, 0:1, :]
    hb = h.astype(BF16)

    pc = _dot(hb, w_ref[:, 0:3 * CONV_CH])
    cb_ref[...] = pc[:, 0:CONV_CH]
    v_ref[...] = pc[:, CONV_CH:2 * CONV_CH] * pc[:, 2 * CONV_CH:3 * CONV_CH]

    def normed(col, row):
        p = _dot(hb, w_ref[:, col:col + SB_DIM])
        ss = _dot((p * p).astype(BF16), bd_ref[...])
        return (p * lax.rsqrt(ss * (1.0 / HEAD_DIM) + EPS) * gains_ref[row:row + 1, :]).astype(BF16)

    sbq_ref[...] = normed(COL_SBQ, 0)
    sbk_ref[...] = normed(COL_SBK, 1)
    sbv_ref[...] = _dot(hb, w_ref[:, COL_SBV:COL_SBV + SB_DIM]).astype(BF16)
    fq_ref[...] = normed(COL_FQ, 2)
    fk_ref[...] = normed(COL_FK, 3)
    fv_ref[...] = _dot(hb, w_ref[:, COL_FV:COL_FV + FOX_DIM]).astype(BF16)

    lf = _log_sigmoid(_dot(hb, w_ref[:, COL_FLOG:COL_FLOG + LANES]) + fb_ref[...]) * LOG2_E
    t0 = lf.astype(BF16)
    r1 = lf - t0.astype(F32)
    t1 = r1.astype(BF16)
    t2 = (r1 - t1.astype(F32)).astype(BF16)
    tri = tri_ref[...]
    cs = _dot(tri, t0) + _dot(tri, t1) + _dot(tri, t2)

    @pl.when(pl.program_id(0) % tiles_per_seq == 0)
    def _():
        carry_ref[...] = jnp.zeros_like(carry_ref)

    cum = cs + carry_ref[0:1, :]
    carry_ref[0:1, :] = cum[tm - 1:tm, :]
    cum_ref[0] = cum.T[0:SUBLANES, :]


def _inproj(x2, mod3, g1, w_in_p, bd, tri, fb, gains, seq):
    t = x2.shape[0]
    tm = TM_PROJ
    tiles_per_seq = seq // tm
    nb = t // seq
    row = lambda i: (i, 0)
    const = lambda i: (0, 0)
    outs = [jax.ShapeDtypeStruct((t, CONV_CH), F32), jax.ShapeDtypeStruct((t, CONV_CH), F32)]
    outs += [jax.ShapeDtypeStruct((t, SB_DIM), BF16)] * 6
    outs += [jax.ShapeDtypeStruct((nb, SUBLANES, seq), F32)]
    out_specs = [pl.BlockSpec((tm, CONV_CH), row)] * 2 + [pl.BlockSpec((tm, SB_DIM), row)] * 6
    out_specs += [pl.BlockSpec((1, SUBLANES, tm), lambda i: (i // tiles_per_seq, 0, i % tiles_per_seq))]
    return pl.pallas_call(
        functools.partial(_inproj_kernel, tiles_per_seq),
        grid=(t // tm,),
        in_specs=[pl.BlockSpec((tm, D_MODEL), row),
                  pl.BlockSpec((1, N_MOD, D_MODEL), lambda i: (i // tiles_per_seq, 0, 0)),
                  pl.BlockSpec((1, D_MODEL), const),
                  pl.BlockSpec((D_MODEL, D_IN_PAD), const),
                  pl.BlockSpec((SB_DIM, SB_DIM), const),
                  pl.BlockSpec((tm, tm), const),
                  pl.BlockSpec((1, LANES), const),
                  pl.BlockSpec((SUBLANES, SB_DIM), const)],
        out_specs=out_specs,
        out_shape=outs,
        scratch_shapes=[pltpu.VMEM((SUBLANES, LANES), F32)],
        compiler_params=_cparams(1),
        name="inproj",
    )(x2, mod3, g1, w_in_p, bd, tri, fb, gains)


def _mask_heads(k_ref, v_ref, km_ref, vm_ref):
    lane = lax.broadcasted_iota(jnp.int32, k_ref.shape[1:], 1)
    for h in range(2):
        keep = (lane // HEAD_DIM) == h
        km_ref[h] = jnp.where(keep, k_ref[0], jnp.zeros_like(k_ref[0]))
        vm_ref[h] = jnp.where(keep, v_ref[0], jnp.zeros_like(v_ref[0]))


def _sb_kernel(q_ref, k_ref, v_ref, u_ref, o_ref, km_ref, vm_ref):
    tq = q_ref.shape[1]
    qi = pl.program_id(2)

    @pl.when(qi == 0)
    def _():
        _mask_heads(k_ref, v_ref, km_ref, vm_ref)

    q = q_ref[0]
    u = u_ref[...]
    row = lax.broadcasted_iota(jnp.int32, (tq, tq), 0)
    col = lax.broadcasted_iota(jnp.int32, (tq, tq), 1)
    strict = col < row

    def block(kb, rsums, acc, diag):
        ks = pl.multiple_of(kb * tq, tq)
        new_rsums = []
        for h in range(2):
            z = _dot_nt(q, km_ref[h, pl.ds(ks, tq), :])
            log_beta = jnp.minimum(z, 0.0) - jnp.log2(1.0 + jnp.exp2(-jnp.abs(z)))
            log_1mb = log_beta - z
            if diag:
                log_1mb = jnp.where(strict, log_1mb, 0.0)
            hi, lo = _split2(log_1mb)
            rest = _dot(hi, u) + _dot(lo, u) + rsums[h]
            w = jnp.exp2(log_beta + rest)
            if diag:
                w = jnp.where(strict, w, 0.0)
            acc = acc + _dot(w.astype(BF16), vm_ref[h, pl.ds(ks, tq), :])
            new_rsums.append(rsums[h] + jnp.sum(log_1mb, axis=-1, keepdims=True))
        return tuple(new_rsums), acc

    def more(rsums):
        return (jnp.max(jnp.maximum(rsums[0], rsums[1])) > LOG2_NEGLIGIBLE).astype(jnp.int32)

    zero = jnp.zeros((tq, 1), F32)
    rsums, acc = block(qi, (zero, zero), jnp.zeros((tq, LANES), F32), True)

    def cond(c):
        return jnp.logical_and(c[0] >= 0, c[3] > 0)

    def body(c):
        kb, rsums, acc, _ = c
        rsums, acc = block(kb, rsums, acc, False)
        return kb - 1, rsums, acc, more(rsums)

    _, _, acc, _ = lax.while_loop(cond, body, (qi - 1, rsums, acc, more(rsums)))
    o_ref[0] = acc


def _sb_attention(q, k, v, u):
    b, s, _ = q.shape
    tq = TQ
    qspec = pl.BlockSpec((1, tq, LANES), lambda bi, hp, qi: (bi, qi, hp))
    kvspec = pl.BlockSpec((1, s, LANES), lambda bi, hp, qi: (bi, 0, hp))
    return pl.pallas_call(
        _sb_kernel,
        grid=(b, SB_DIM // LANES, s // tq),
        in_specs=[qspec, kvspec, kvspec, pl.BlockSpec((tq, tq), lambda bi, hp, qi: (0, 0))],
        out_specs=qspec,
        out_shape=jax.ShapeDtypeStruct((b, s, SB_DIM), F32),
        scratch_shapes=[pltpu.VMEM((2, s, LANES), BF16), pltpu.VMEM((2, s, LANES), BF16)],
        compiler_params=_cparams(3),
        name="sb_attention",
    )(q, k, v, u)


def _fox_kernel(n_kb, fend_ref, zb_ref, q_ref, k_ref, v_ref, cum_ref, o_ref, km_ref, vm_ref):
    tq = q_ref.shape[1]
    bi = pl.program_id(0)
    hp = pl.program_id(1)
    qi = pl.program_id(2)

    @pl.when(qi == 0)
    def _():
        _mask_heads(k_ref, v_ref, km_ref, vm_ref)

    q = q_ref[0]
    row = lax.broadcasted_iota(jnp.int32, (tq, tq), 0)
    col = lax.broadcasted_iota(jnp.int32, (tq, tq), 1)
    causal = col <= row

    def block(kb, state, diag):
        ks = pl.multiple_of(kb * tq, tq)
        new_state = []
        for h in range(2):
            m, l, acc = state[h]
            s = _dot_nt(q, km_ref[h, pl.ds(ks, tq), :]) - cum_ref[0, 0, h:h + 1, pl.ds(ks, tq)]
            if diag:
                s = jnp.where(causal, s, -1e30)
            m_new = jnp.maximum(m, jnp.max(s, axis=-1, keepdims=True))
            p = jnp.exp2(s - m_new)
            alpha = jnp.exp2(m - m_new)
            l = alpha * l + jnp.sum(p, axis=-1, keepdims=True)
            acc = alpha * acc + _dot(p.astype(BF16), vm_ref[h, pl.ds(ks, tq), :])
            new_state.append((m_new, l, acc))
        return tuple(new_state)

    init = (jnp.full((tq, 1), -1e30, F32), jnp.zeros((tq, 1), F32), jnp.zeros((tq, LANES), F32))
    state = block(qi, (init, init), True)

    zbound = zb_ref[0]

    def first_block(h):
        base = (bi * FOX_HEADS + hp * 2 + h) * n_kb
        f_q = fend_ref[base + jnp.maximum(qi - 1, 0)]

        def keep(kb):
            gap = zbound + f_q - fend_ref[base + jnp.maximum(kb - 1, 0)]
            return jnp.logical_and(kb > 0, gap >= LOG2_NEGLIGIBLE)

        return lax.while_loop(keep, lambda kb: kb - 1, qi)

    kb_lo = jnp.minimum(first_block(0), first_block(1))
    state = lax.fori_loop(kb_lo, qi, lambda kb, st: block(kb, st, False), state)
    o_ref[0] = state[0][2] / state[0][1] + state[1][2] / state[1][1]


def _fox_attention(q, k, v, cum4, fend, zbound):
    b, s, _ = q.shape
    tq = TQ
    n_kb = s // tq
    qspec = pl.BlockSpec((1, tq, LANES), lambda bi, hp, qi, *_: (bi, qi, hp))
    kvspec = pl.BlockSpec((1, s, LANES), lambda bi, hp, qi, *_: (bi, 0, hp))
    grid_spec = pltpu.PrefetchScalarGridSpec(
        num_scalar_prefetch=2,
        grid=(b, FOX_DIM // LANES, n_kb),
        in_specs=[qspec, kvspec, kvspec,
                  pl.BlockSpec((1, 1, SUBLANES, s), lambda bi, hp, qi, *_: (bi, hp, 0, 0))],
        out_specs=qspec,
        scratch_shapes=[pltpu.VMEM((2, s, LANES), BF16), pltpu.VMEM((2, s, LANES), BF16)],
    )
    return pl.pallas_call(
        functools.partial(_fox_kernel, n_kb),
        grid_spec=grid_spec,
        out_shape=jax.ShapeDtypeStruct((b, s, FOX_DIM), F32),
        compiler_params=_cparams(3),
        name="fox_attention",
    )(fend, zbound, q, k, v, cum4)


def _outproj_kernel(tiles_per_seq, x_ref, cb_ref, v_ref, vprev_ref, ysb_ref, yfox_ref, cw_ref, og_ref, wout_ref,
                    mod_ref, g2_ref, rw_ref, rb_ref, bd_ref, ltri_ref,
                    x1_ref, h2t_ref, sel_ref, gate_ref, cnt_ref, vext_ref, carry_ref):
    tm = x_ref.shape[0]
    i = pl.program_id(0)

    @pl.when(i == 0)
    def _():
        carry_ref[...] = jnp.zeros_like(carry_ref)

    first = (i % tiles_per_seq) == 0
    vext_ref[0:SUBLANES, :] = jnp.where(first, 0.0, vprev_ref[...])
    vext_ref[SUBLANES:SUBLANES + tm, :] = v_ref[...]
    conv = (cw_ref[0:1, :] * vext_ref[SUBLANES - 2:SUBLANES - 2 + tm, :]
            + cw_ref[1:2, :] * vext_ref[SUBLANES - 1:SUBLANES - 1 + tm, :]
            + cw_ref[2:3, :] * v_ref[...])
    y = jnp.concatenate([cb_ref[...] * conv, ysb_ref[...], yfox_ref[...]], axis=-1)

    ysq = (y * y).astype(BF16)
    bd = bd_ref[...]
    ss = jnp.concatenate([_dot(ysq[:, c:c + 2 * LANES], bd) for c in range(0, D_MIX, 2 * LANES)], axis=-1)
    yn = y * lax.rsqrt(ss * (1.0 / HEAD_DIM) + EPS) * og_ref[...]
    x1 = x_ref[...] + mod_ref[0, 2:3, :] * _dot(yn.astype(BF16), wout_ref[...])
    x1_ref[...] = x1

    ms = jnp.mean(x1 * x1, axis=-1, keepdims=True)
    h2 = (x1 * lax.rsqrt(ms + EPS)) * g2_ref[...]
    h2 = h2 * (1.0 + mod_ref[0, 4:5, :]) + mod_ref[0, 3:4, :]
    for j in range(ROW_TILES):
        h2t_ref[pl.ds(j, tm, stride=SUBLANES), :] = h2[:, j * LANES:(j + 1) * LANES]

    logits = jnp.dot(h2, rw_ref[...], precision=lax.Precision.HIGHEST, preferred_element_type=F32) + rb_ref[...]
    lane = lax.broadcasted_iota(jnp.int32, (tm, LANES), 1)
    lane_f = lane.astype(F32)
    neg = jnp.float32(-jnp.inf)
    cur = jnp.where(lane < N_EXPERTS, logits, neg)
    tops, sels = [], []
    for _ in range(TOP_K):
        mk = jnp.max(cur, axis=-1, keepdims=True)
        ik = jnp.min(jnp.where(cur == mk, lane_f, float(LANES)), axis=-1, keepdims=True)
        sel = lane_f == ik
        cur = jnp.where(sel, neg, cur)
        tops.append(mk)
        sels.append((sel, ik.astype(jnp.int32)))
    es = [jnp.exp(t - tops[0]) for t in tops]
    inv = 1.0 / (es[0] + es[1] + es[2] + es[3])

    multi = jnp.zeros((tm, LANES), F32)
    for sel, _ in sels:
        multi = jnp.where(sel, 1.0, multi)
    before = _dot(ltri_ref[...], multi.astype(BF16)) + carry_ref[0:1, :]
    carry_ref[0:1, :] = before[tm - 1:tm, :] + multi[tm - 1:tm, :]

    sel_out = jnp.zeros((tm, LANES), jnp.int32)
    gate_out = jnp.zeros((tm, LANES), F32)
    for k, (sel, ik) in enumerate(sels):
        rank = jnp.sum(jnp.where(sel, before, 0.0), axis=-1, keepdims=True).astype(jnp.int32)
        sel_out = jnp.where(lane == k, ik, sel_out)
        sel_out = jnp.where(lane == TOP_K + k, rank, sel_out)
        gate_out = jnp.where(lane == k, es[k] * inv, gate_out)
    sel_ref[...] = sel_out
    gate_ref[...] = gate_out
    cnt_ref[...] = jnp.broadcast_to(carry_ref[0:1, :], cnt_ref.shape)


def _outproj(x2, cb, v, ysb, yfox, conv_w, og, w_out_b, mod3, g2, rw, rb, bd2, ltri, seq):
    t = x2.shape[0]
    tm = TM_PROJ
    tiles_per_seq = seq // tm
    row = lambda i: (i, 0)
    const = lambda i: (0, 0)
    return pl.pallas_call(
        functools.partial(_outproj_kernel, tiles_per_seq),
        grid=(t // tm,),
        in_specs=[pl.BlockSpec((tm, D_MODEL), row),
                  pl.BlockSpec((tm, CONV_CH), row),
                  pl.BlockSpec((tm, CONV_CH), row),
                  pl.BlockSpec((SUBLANES, CONV_CH), lambda i: (jnp.maximum(i * (tm // SUBLANES) - 1, 0), 0)),
                  pl.BlockSpec((tm, SB_DIM), row),
                  pl.BlockSpec((tm, FOX_DIM), row),
                  pl.BlockSpec((CONV_W, CONV_CH), const),
                  pl.BlockSpec((1, D_MIX), const),
                  pl.BlockSpec((D_MIX, D_MODEL), const),
                  pl.BlockSpec((1, N_MOD, D_MODEL), lambda i: (i // tiles_per_seq, 0, 0)),
                  pl.BlockSpec((1, D_MODEL), const),
                  pl.BlockSpec((D_MODEL, LANES), const),
                  pl.BlockSpec((1, LANES), const),
                  pl.BlockSpec((2 * LANES, 2 * LANES), const),
                  pl.BlockSpec((tm, tm), const)],
        out_specs=[pl.BlockSpec((tm, D_MODEL), row),
                   pl.BlockSpec((tm * SUBLANES, LANES), row),
                   pl.BlockSpec((tm, LANES), row),
                   pl.BlockSpec((tm, LANES), row),
                   pl.BlockSpec((SUBLANES, LANES), const)],
        out_shape=[jax.ShapeDtypeStruct((t, D_MODEL), F32),
                   jax.ShapeDtypeStruct((t * SUBLANES, LANES), F32),
                   jax.ShapeDtypeStruct((t, LANES), jnp.int32),
                   jax.ShapeDtypeStruct((t, LANES), F32),
                   jax.ShapeDtypeStruct((SUBLANES, LANES), F32)],
        scratch_shapes=[pltpu.VMEM((tm + SUBLANES, CONV_CH), F32), pltpu.VMEM((SUBLANES, LANES), F32)],
        compiler_params=_cparams(1),
        name="outproj_router",
    )(x2, cb, v, v, ysb, yfox, conv_w, og, w_out_b, mod3, g2, rw, rb, bd2, ltri)


def _row_copy(src, src_row, dst, dst_row, sem):
    return pltpu.make_async_copy(src.at[pl.ds(pl.multiple_of(src_row * SUBLANES, SUBLANES), SUBLANES), :],
                                 dst.at[pl.ds(pl.multiple_of(dst_row * SUBLANES, SUBLANES), SUBLANES), :], sem)


def _zero_fill(meta_ref, zero_ref, xg_ref, zsem):
    n_tiles = xg_ref.shape[0] // (TM_GROUP * SUBLANES)
    zero_ref[...] = jnp.zeros_like(zero_ref)

    def sweep(wait):
        def per_expert(e, c):
            pos = meta_ref[N_EXPERTS + e]
            n = meta_ref[2 * N_EXPERTS + e]
            bit = TM_GROUP // 2
            while bit:
                @pl.when((n & bit) != 0)
                def _(pos=pos, bit=bit):
                    cp = pltpu.make_async_copy(
                        zero_ref.at[pl.ds(0, bit * SUBLANES), :],
                        xg_ref.at[pl.ds(pl.multiple_of(pos * SUBLANES, SUBLANES), bit * SUBLANES), :], zsem)
                    cp.wait() if wait else cp.start()
                pos = pos + (n & bit)
                bit //= 2
            return c

        lax.fori_loop(0, N_EXPERTS, per_expert, 0)

        def per_tile(j, c):
            rows = TM_GROUP * SUBLANES
            cp = pltpu.make_async_copy(zero_ref, xg_ref.at[pl.ds(pl.multiple_of(j * rows, rows), rows), :], zsem)
            cp.wait() if wait else cp.start()
            return c

        lax.fori_loop(meta_ref[3 * N_EXPERTS], n_tiles, per_tile, 0)

    sweep(False)
    sweep(True)


def _dispatch_kernel(off_ref, eidx_ref, rank_ref, h_ref, xg_ref, zero_ref, sem, zsem):
    tm = h_ref.shape[0] // SUBLANES

    @pl.when(pl.program_id(0) == 0)
    def _():
        _zero_fill(off_ref, zero_ref, xg_ref, zsem)

    def issue(r, c):
        for k in range(TOP_K):
            p = off_ref[eidx_ref[r * TOP_K + k]] + rank_ref[r * TOP_K + k]
            _row_copy(h_ref, r, xg_ref, p, sem).start()
        return c

    lax.fori_loop(0, tm, issue, 0)
    n = tm * TOP_K * SUBLANES
    pltpu.make_async_copy(xg_ref.at[pl.ds(0, n), :], xg_ref.at[pl.ds(0, n), :], sem).wait()


def _dispatch(off, eidx, rank, h2t, n_rows):
    t = h2t.shape[0] // SUBLANES
    tm = TM_ROWS
    grid_spec = pltpu.PrefetchScalarGridSpec(
        num_scalar_prefetch=1,
        grid=(t // tm,),
        in_specs=[pl.BlockSpec((tm * TOP_K,), lambda i, *_: (i,), memory_space=pltpu.SMEM),
                  pl.BlockSpec((tm * TOP_K,), lambda i, *_: (i,), memory_space=pltpu.SMEM),
                  pl.BlockSpec((tm * SUBLANES, LANES), lambda i, *_: (i, 0))],
        out_specs=pl.BlockSpec(memory_space=pl.ANY),
        scratch_shapes=[pltpu.VMEM((TM_GROUP * SUBLANES, LANES), F32),
                        pltpu.SemaphoreType.DMA(()), pltpu.SemaphoreType.DMA(())],
    )
    return pl.pallas_call(
        _dispatch_kernel,
        grid_spec=grid_spec,
        out_shape=jax.ShapeDtypeStruct((n_rows * SUBLANES, LANES), F32),
        compiler_params=_cparams(1),
        name="moe_dispatch",
    )(off, eidx, rank, h2t)


def _expert_kernel(te_ref, nv_ref, xg_ref, w1_ref, b1_ref, w2_ref, b2_ref, yg_ref, wg_ref, wl_ref, w2b_ref, wt_ref):
    tmg = xg_ref.shape[0] // SUBLANES
    j = pl.program_id(0)
    nv = nv_ref[j]

    @pl.when(jnp.logical_or(j == 0, te_ref[j] != te_ref[jnp.maximum(j - 1, 0)]))
    def _():
        ncol = wt_ref.shape[0] // ROW_TILES
        half = ncol // 2
        for n0 in range(0, 2 * D_EXPERT, ncol):
            wt = w1_ref[0, :, n0:n0 + ncol].T
            for c in range(ROW_TILES):
                wt_ref[pl.ds(c, ncol, stride=ROW_TILES), :] = wt[:, c * LANES:(c + 1) * LANES]
            rows = slice(n0 // 2, n0 // 2 + half)
            for c in range(ROW_TILES):
                cols = slice(c * LANES, (c + 1) * LANES)
                wg_ref[rows, cols] = wt_ref[pl.ds(c, half, stride=2 * ROW_TILES), :].astype(BF16)
                wl_ref[rows, cols] = wt_ref[pl.ds(ROW_TILES + c, half, stride=2 * ROW_TILES), :].astype(BF16)
            w2b_ref[rows, :] = w2_ref[0, rows, :].astype(BF16)

    @pl.when(nv == 0)
    def _():
        yg_ref[...] = jnp.zeros_like(yg_ref)

    @pl.when(nv > 0)
    def _():
        x = jnp.concatenate([xg_ref[pl.ds(c, tmg, stride=SUBLANES), :] for c in range(ROW_TILES)], axis=-1)
        x = x.astype(BF16)
        glu = jnp.minimum(_dot_nt(x, wg_ref[...]) + b1_ref[0, 0:1, :], SWIGLU_LIMIT)
        lin = jnp.clip(_dot_nt(x, wl_ref[...]) + b1_ref[0, 1:2, :], -SWIGLU_LIMIT, SWIGLU_LIMIT)
        a = glu * (1.0 / (1.0 + jnp.exp(-SWIGLU_ALPHA * glu))) * (lin + 1.0)
        y = _dot(a.astype(BF16), w2b_ref[...]) + b2_ref[0]
        for c in range(ROW_TILES):
            yg_ref[pl.ds(c, tmg, stride=SUBLANES), :] = y[:, c * LANES:(c + 1) * LANES]


def _experts(te, nv, xg, w1, b1, w2, b2):
    tmg = TM_GROUP
    n_tiles = xg.shape[0] // (tmg * SUBLANES)
    grid_spec = pltpu.PrefetchScalarGridSpec(
        num_scalar_prefetch=2,
        grid=(n_tiles,),
        in_specs=[pl.BlockSpec((tmg * SUBLANES, LANES), lambda j, te, nv: (j, 0)),
                  pl.BlockSpec((1, D_MODEL, 2 * D_EXPERT), lambda j, te, nv: (te[j], 0, 0)),
                  pl.BlockSpec((1, 2, D_EXPERT), lambda j, te, nv: (te[j], 0, 0)),
                  pl.BlockSpec((1, D_EXPERT, D_MODEL), lambda j, te, nv: (te[j], 0, 0)),
                  pl.BlockSpec((1, 1, D_MODEL), lambda j, te, nv: (te[j], 0, 0))],
        out_specs=pl.BlockSpec((tmg * SUBLANES, LANES), lambda j, te, nv: (j, 0)),
        scratch_shapes=[pltpu.VMEM((D_EXPERT, D_MODEL), BF16), pltpu.VMEM((D_EXPERT, D_MODEL), BF16),
                        pltpu.VMEM((D_EXPERT, D_MODEL), BF16), pltpu.VMEM((2 * LANES * ROW_TILES, LANES), F32)],
    )
    return pl.pallas_call(
        _expert_kernel,
        grid_spec=grid_spec,
        out_shape=jax.ShapeDtypeStruct(xg.shape, F32),
        compiler_params=_cparams(1, EXPERT_VMEM_LIMIT),
        name="moe_experts",
    )(te, nv, xg, w1, b1, w2, b2)


def _combine_kernel(tiles_per_seq, off_ref, eidx_ref, rank_ref, gate_ref, x1_ref, mod_ref, yg_ref, o_ref, buf_ref, sem):
    tm = x1_ref.shape[0]

    def issue(r, c):
        for k in range(TOP_K):
            p = off_ref[eidx_ref[r * TOP_K + k]] + rank_ref[r * TOP_K + k]
            _row_copy(yg_ref, p, buf_ref.at[k], r, sem).start()
        return c

    lax.fori_loop(0, tm, issue, 0)
    for k in range(TOP_K):
        pltpu.make_async_copy(buf_ref.at[k], buf_ref.at[k], sem).wait()

    gates = gate_ref[...]
    g2 = mod_ref[0, 5:6, :]
    for c in range(ROW_TILES):
        acc = jnp.zeros((tm, LANES), F32)
        for k in range(TOP_K):
            acc = acc + gates[:, k:k + 1] * buf_ref[k, pl.ds(c, tm, stride=SUBLANES), :]
        cols = slice(c * LANES, (c + 1) * LANES)
        o_ref[:, cols] = x1_ref[:, cols] + g2[:, cols] * acc


def _combine(off, eidx, rank, gates, x1, mod3, yg, seq):
    t = x1.shape[0]
    tm = TM_ROWS
    tiles_per_seq = seq // tm
    grid_spec = pltpu.PrefetchScalarGridSpec(
        num_scalar_prefetch=1,
        grid=(t // tm,),
        in_specs=[pl.BlockSpec((tm * TOP_K,), lambda i, *_: (i,), memory_space=pltpu.SMEM),
                  pl.BlockSpec((tm * TOP_K,), lambda i, *_: (i,), memory_space=pltpu.SMEM),
                  pl.BlockSpec((tm, LANES), lambda i, *_: (i, 0)),
                  pl.BlockSpec((tm, D_MODEL), lambda i, *_: (i, 0)),
                  pl.BlockSpec((1, N_MOD, D_MODEL), lambda i, *_: (i // tiles_per_seq, 0, 0)),
                  pl.BlockSpec(memory_space=pl.ANY)],
        out_specs=pl.BlockSpec((tm, D_MODEL), lambda i, *_: (i, 0)),
        scratch_shapes=[pltpu.VMEM((TOP_K, tm * SUBLANES, LANES), F32), pltpu.SemaphoreType.DMA(())],
    )
    return pl.pallas_call(
        functools.partial(_combine_kernel, tiles_per_seq),
        grid_spec=grid_spec,
        out_shape=jax.ShapeDtypeStruct((t, D_MODEL), F32),
        compiler_params=_cparams(1),
        name="moe_combine",
    )(off, eidx, rank, gates, x1, mod3, yg)


def _block_diag_ones(n):
    i = np.arange(n)
    return jnp.asarray((i[:, None] // HEAD_DIM) == (i[None, :] // HEAD_DIM), dtype=BF16)


def _tri(n, strict, upper):
    i = np.arange(n)
    if upper:
        m = (i[:, None] > i[None, :]) if strict else (i[:, None] >= i[None, :])
        return jnp.asarray(m, dtype=BF16)
    m = (i[None, :] < i[:, None]) if strict else (i[None, :] <= i[:, None])
    return jnp.asarray(m, dtype=BF16)


def _tile2(g):
    return jnp.tile(g, SB_DIM // HEAD_DIM)


def _layer(x2, c, seq, norm1_g, w_ada, b_ada, w_in, conv_w, sb_q_g, sb_k_g, fox_q_g, fox_k_g, fox_f_b,
           out_norm_g, w_out, norm2_g, router_w, router_b, w_mlp1, b_mlp1, w_mlp2, b_mlp2):
    t = x2.shape[0]
    nb = t // seq
    scale = HEAD_DIM ** -0.5

    w_in_p = jnp.pad(w_in, ((0, 0), (0, D_IN_PAD - w_in.shape[1]))).astype(BF16)
    fb = jnp.pad(fox_f_b, (0, LANES - FOX_HEADS)).reshape(1, LANES)
    qscale = scale * LOG2_E
    gains = jnp.stack([_tile2(sb_q_g) * qscale, _tile2(sb_k_g), _tile2(fox_q_g) * qscale, _tile2(fox_k_g)])
    gains = jnp.pad(gains, ((0, SUBLANES - 4), (0, 0)))
    rw = jnp.pad(router_w, ((0, 0), (0, LANES - N_EXPERTS)))
    rb = jnp.pad(router_b, (0, LANES - N_EXPERTS)).reshape(1, LANES)
    b1 = jnp.stack([b_mlp1[:, 0::2], b_mlp1[:, 1::2]], axis=1)
    b2 = b_mlp2.reshape(N_EXPERTS, 1, D_MODEL)

    mod3 = _modulation(c, w_ada, b_ada).reshape(nb, N_MOD, D_MODEL)

    cb, v, sbq, sbk, sbv, fq, fk, fv, cum = _inproj(
        x2, mod3, norm1_g.reshape(1, D_MODEL), w_in_p, _block_diag_ones(SB_DIM),
        _tri(TM_PROJ, strict=False, upper=False), fb, gains, seq)

    r3 = lambda a: a.reshape(nb, seq, SB_DIM)
    ysb = _sb_attention(r3(sbq), r3(sbk), r3(sbv), _tri(TQ, strict=True, upper=True))

    n_kb = seq // TQ
    cum4 = jnp.pad(cum[:, :FOX_HEADS, :].reshape(nb, FOX_DIM // LANES, 2, seq),
                   ((0, 0), (0, 0), (0, SUBLANES - 2), (0, 0)))
    fend = cum[:, :FOX_HEADS, TQ - 1::TQ].reshape(nb * FOX_HEADS * n_kb)
    zbound = (2.0 * 1.02 * HEAD_DIM * qscale * jnp.max(jnp.abs(fox_q_g)) * jnp.max(jnp.abs(fox_k_g))).reshape(1)
    yfox = _fox_attention(r3(fq), r3(fk), r3(fv), cum4, fend, zbound)

    x1, h2t, sel, gates, cnt = _outproj(
        x2, cb, v, ysb.reshape(t, SB_DIM), yfox.reshape(t, FOX_DIM), conv_w, out_norm_g.reshape(1, D_MIX),
        w_out.astype(BF16), mod3, norm2_g.reshape(1, D_MODEL), rw, rb, _block_diag_ones(2 * LANES),
        _tri(TM_PROJ, strict=True, upper=False), seq)

    counts = cnt[0, :N_EXPERTS].astype(jnp.int32)
    padded = ((counts + TM_GROUP - 1) // TM_GROUP) * TM_GROUP
    off_end = jnp.cumsum(padded)
    off = off_end - padded
    n_rows = t * TOP_K + N_EXPERTS * TM_GROUP
    n_tiles = n_rows // TM_GROUP
    start = jnp.arange(n_tiles, dtype=jnp.int32) * TM_GROUP
    te = jnp.minimum(jnp.sum(start[:, None] >= off_end[None, :], axis=1), N_EXPERTS - 1).astype(jnp.int32)
    nv = jnp.clip(off[te] + counts[te] - start, 0, TM_GROUP).astype(jnp.int32)
    eidx = sel[:, :TOP_K].reshape(t * TOP_K)
    rank = sel[:, TOP_K:2 * TOP_K].reshape(t * TOP_K)
    meta = jnp.concatenate([off, off + counts, padded - counts, off_end[-1:] // TM_GROUP]).astype(jnp.int32)

    xg = _dispatch(meta, eidx, rank, h2t, n_rows)
    yg = _experts(te, nv, xg, w_mlp1, b1, w_mlp2, b2)
    return _combine(meta, eidx, rank, gates, x1, mod3, yg, seq)


def kernel(x, c, norm1_g, w_ada, b_ada, w_in, conv_w, sb_q_g, sb_k_g, fox_q_g, fox_k_g, fox_f_b, out_norm_g, w_out,
           norm2_g, router_w, router_b, w_mlp1, b_mlp1, w_mlp2, b_mlp2):
    b, s, d = x.shape
    x2 = x.reshape(b * s, d)
    params = (norm1_g, w_ada, b_ada, w_in, conv_w, sb_q_g, sb_k_g, fox_q_g, fox_k_g, fox_f_b, out_norm_g, w_out,
              norm2_g, router_w, router_b, w_mlp1, b_mlp1, w_mlp2, b_mlp2)
    for layer in range(norm1_g.shape[0]):
        x2 = _layer(x2, c, s, *(p[layer] for p in params))
    return x2.reshape(b, s, d)
```

```python
import functools

import numpy as np
import jax
import jax.numpy as jnp
from jax import lax
from jax.experimental import pallas as pl
from jax.experimental.pallas import tpu as pltpu

F32 = jnp.float32
BF16 = jnp.bfloat16

D_MODEL = 1024
HEAD_DIM = 64
CONV_CH = 256
CONV_W = 3
SB_DIM = 384
FOX_DIM = 384
FOX_HEADS = 6
D_MIX = 1024
N_MOD = 6
N_EXPERTS = 32
TOP_K = 4
D_EXPERT = 1024
SWIGLU_ALPHA = 1.702
SWIGLU_LIMIT = 7.0
EPS = 1e-6

LANES = 128
SUBLANES = 8
ROW_TILES = D_MODEL // LANES
D_IN_PAD = 3 * CONV_CH + 3 * SB_DIM + 3 * FOX_DIM + LANES
COL_SBQ = 3 * CONV_CH
COL_SBK = COL_SBQ + SB_DIM
COL_SBV = COL_SBK + SB_DIM
COL_FQ = COL_SBV + SB_DIM
COL_FK = COL_FQ + FOX_DIM
COL_FV = COL_FK + FOX_DIM
COL_FLOG = COL_FV + FOX_DIM

TM_PROJ = 512
TQ = 256
TM_DISPATCH = 1024
TM_COMBINE = 256
TM_GROUP = 512
LOG2_E = 1.4426950408889634
LOG2_NEGLIGIBLE = -64.0
VMEM_LIMIT = 48 * 1024 * 1024
EXPERT_VMEM_LIMIT = 58 * 1024 * 1024


def _cparams(n_axes, vmem=VMEM_LIMIT):
    return pltpu.CompilerParams(dimension_semantics=("arbitrary",) * n_axes, vmem_limit_bytes=vmem)


def _dot(a, b):
    return jnp.dot(a, b, preferred_element_type=F32)


def _dot_nt(a, b):
    return lax.dot_general(a, b, (((1,), (1,)), ((), ())), preferred_element_type=F32)


def _split2(x):
    hi = x.astype(BF16)
    lo = (x - hi.astype(F32)).astype(BF16)
    return hi, lo


def _log_sigmoid(x):
    return jnp.minimum(x, 0.0) - jnp.log(1.0 + jnp.exp(-jnp.abs(x)))


def _mod_kernel(c_ref, w_ref, b_ref, o_ref):
    c = c_ref[...]
    s = c / (1.0 + jnp.exp(-c))
    o_ref[...] = _dot(s.astype(BF16), w_ref[...].astype(BF16)) + b_ref[...]


def _modulation(c, w_ada, b_ada):
    b, d = c.shape
    n = w_ada.shape[1]
    tn = 1024
    return pl.pallas_call(
        _mod_kernel,
        grid=(n // tn,),
        in_specs=[pl.BlockSpec((b, d), lambda j: (0, 0)),
                  pl.BlockSpec((d, tn), lambda j: (0, j)),
                  pl.BlockSpec((1, tn), lambda j: (0, j))],
        out_specs=pl.BlockSpec((b, tn), lambda j: (0, j)),
        out_shape=jax.ShapeDtypeStruct((b, n), F32),
        compiler_params=_cparams(1),
        name="modulation",
    )(c, w_ada, b_ada.reshape(1, n))


def _inproj_kernel(tiles_per_seq, x_ref, mod_ref, g1_ref, w_ref, bd_ref, tri_ref, fb_ref, gains_ref,
                   cb_ref, v_ref, sbq_ref, sbk_ref, sbv_ref, fq_ref, fk_ref, fv_ref, cum_ref, carry_ref):
    tm = x_ref.shape[0]
    x = x_ref[...]
    ms = jnp.mean(x * x, axis=-1, keepdims=True)
    h = (x * lax.rsqrt(ms + EPS)) * g1_ref[...]
    h = h * (1.0 + mod_ref[0, 1:2, :]) + mod_ref[0, 0:1, :]
    hb = h.astype(BF16)

    pc = _dot(hb, w_ref[:, 0:3 * CONV_CH])
    cb_ref[...] = pc[:, 0:CONV_CH]
    v_ref[...] = pc[:, CONV_CH:2 * CONV_CH] * pc[:, 2 * CONV_CH:3 * CONV_CH]

    def normed(col, row):
        p = _dot(hb, w_ref[:, col:col + SB_DIM])
        ss = _dot((p * p).astype(BF16), bd_ref[...])
        return (p * lax.rsqrt(ss * (1.0 / HEAD_DIM) + EPS) * gains_ref[row:row + 1, :]).astype(BF16)

    sbq_ref[...] = normed(COL_SBQ, 0)
    sbk_ref[...] = normed(COL_SBK, 1)
    sbv_ref[...] = _dot(hb, w_ref[:, COL_SBV:COL_SBV + SB_DIM]).astype(BF16)
    fq_ref[...] = normed(COL_FQ, 2)
    fk_ref[...] = normed(COL_FK, 3)
    fv_ref[...] = _dot(hb, w_ref[:, COL_FV:COL_FV + FOX_DIM]).astype(BF16)

    lf = _log_sigmoid(_dot(hb, w_ref[:, COL_FLOG:COL_FLOG + LANES]) + fb_ref[...]) * LOG2_E
    t0 = lf.astype(BF16)
    r1 = lf - t0.astype(F32)
    t1 = r1.astype(BF16)
    t2 = (r1 - t1.astype(F32)).astype(BF16)
    tri = tri_ref[...]
    cs = _dot(tri, t0) + _dot(tri, t1) + _dot(tri, t2)

    @pl.when(pl.program_id(0) % tiles_per_seq == 0)
    def _():
        carry_ref[...] = jnp.zeros_like(carry_ref)

    cum = cs + carry_ref[0:1, :]
    carry_ref[0:1, :] = cum[tm - 1:tm, :]
    cum_ref[0] = cum.T[0:SUBLANES, :]


def _inproj(x2, mod3, g1, w_in_p, bd, tri, fb, gains, seq):
    t = x2.shape[0]
    tm = TM_PROJ
    tiles_per_seq = seq // tm
    nb = t // seq
    row = lambda i: (i, 0)
    const = lambda i: (0, 0)
    outs = [jax.ShapeDtypeStruct((t, CONV_CH), F32), jax.ShapeDtypeStruct((t, CONV_CH), F32)]
    outs += [jax.ShapeDtypeStruct((t, SB_DIM), BF16)] * 6
    outs += [jax.ShapeDtypeStruct((nb, SUBLANES, seq), F32)]
    out_specs = [pl.BlockSpec((tm, CONV_CH), row)] * 2 + [pl.BlockSpec((tm, SB_DIM), row)] * 6
    out_specs += [pl.BlockSpec((1, SUBLANES, tm), lambda i: (i // tiles_per_seq, 0, i % tiles_per_seq))]
    return pl.pallas_call(
        functools.partial(_inproj_kernel, tiles_per_seq),
        grid=(t // tm,),
        in_specs=[pl.BlockSpec((tm, D_MODEL), row),
                  pl.BlockSpec((1, N_MOD, D_MODEL), lambda i: (i // tiles_per_seq, 0, 0)),
                  pl.BlockSpec((1, D_MODEL), const),
                  pl.BlockSpec((D_MODEL, D_IN_PAD), const),
                  pl.BlockSpec((SB_DIM, SB_DIM), const),
                  pl.BlockSpec((tm, tm), const),
                  pl.BlockSpec((1, LANES), const),
                  pl.BlockSpec((SUBLANES, SB_DIM), const)],
        out_specs=out_specs,
        out_shape=outs,
        scratch_shapes=[pltpu.VMEM((SUBLANES, LANES), F32)],
        compiler_params=_cparams(1),
        name="inproj",
    )(x2, mod3, g1, w_in_p, bd, tri, fb, gains)


def _mask_heads(k_ref, v_ref, km_ref, vm_ref):
    lane = lax.broadcasted_iota(jnp.int32, k_ref.shape[1:], 1)
    for h in range(2):
        keep = (lane // HEAD_DIM) == h
        km_ref[h] = jnp.where(keep, k_ref[0], jnp.zeros_like(k_ref[0]))
        vm_ref[h] = jnp.where(keep, v_ref[0], jnp.zeros_like(v_ref[0]))


def _sb_kernel(q_ref, k_ref, v_ref, u_ref, o_ref, km_ref, vm_ref):
    tq = q_ref.shape[1]
    qi = pl.program_id(2)

    @pl.when(qi == 0)
    def _():
        _mask_heads(k_ref, v_ref, km_ref, vm_ref)

    q = q_ref[0]
    u = u_ref[...]
    row = lax.broadcasted_iota(jnp.int32, (tq, tq), 0)
    col = lax.broadcasted_iota(jnp.int32, (tq, tq), 1)
    strict = col < row

    def block(kb, rsums, acc, diag):
        ks = pl.multiple_of(kb * tq, tq)
        new_rsums = []
        for h in range(2):
            z = _dot_nt(q, km_ref[h, pl.ds(ks, tq), :])
            log_beta = jnp.minimum(z, 0.0) - jnp.log2(1.0 + jnp.exp2(-jnp.abs(z)))
            log_1mb = log_beta - z
            if diag:
                log_1mb = jnp.where(strict, log_1mb, 0.0)
            hi, lo = _split2(log_1mb)
            rest = _dot(hi, u) + _dot(lo, u) + rsums[h]
            w = jnp.exp2(log_beta + rest)
            if diag:
                w = jnp.where(strict, w, 0.0)
            acc = acc + _dot(w.astype(BF16), vm_ref[h, pl.ds(ks, tq), :])
            new_rsums.append(rsums[h] + jnp.sum(log_1mb, axis=-1, keepdims=True))
        return tuple(new_rsums), acc

    def more(rsums):
        return (jnp.max(jnp.maximum(rsums[0], rsums[1])) > LOG2_NEGLIGIBLE).astype(jnp.int32)

    zero = jnp.zeros((tq, 1), F32)
    rsums, acc = block(qi, (zero, zero), jnp.zeros((tq, LANES), F32), True)

    def cond(c):
        return jnp.logical_and(c[0] >= 0, c[3] > 0)

    def body(c):
        kb, rsums, acc, _ = c
        rsums, acc = block(kb, rsums, acc, False)
        return kb - 1, rsums, acc, more(rsums)

    _, _, acc, _ = lax.while_loop(cond, body, (qi - 1, rsums, acc, more(rsums)))
    o_ref[0] = acc


def _sb_attention(q, k, v, u):
    b, s, _ = q.shape
    tq = TQ
    qspec = pl.BlockSpec((1, tq, LANES), lambda bi, hp, qi: (bi, qi, hp))
    kvspec = pl.BlockSpec((1, s, LANES), lambda bi, hp, qi: (bi, 0, hp))
    return pl.pallas_call(
        _sb_kernel,
        grid=(b, SB_DIM // LANES, s // tq),
        in_specs=[qspec, kvspec, kvspec, pl.BlockSpec((tq, tq), lambda bi, hp, qi: (0, 0))],
        out_specs=qspec,
        out_shape=jax.ShapeDtypeStruct((b, s, SB_DIM), F32),
        scratch_shapes=[pltpu.VMEM((2, s, LANES), BF16), pltpu.VMEM((2, s, LANES), BF16)],
        compiler_params=_cparams(3),
        name="sb_attention",
    )(q, k, v, u)


def _fox_kernel(n_kb, fend_ref, zb_ref, q_ref, k_ref, v_ref, cum_ref, o_ref, km_ref, vm_ref):
    tq = q_ref.shape[1]
    bi = pl.program_id(0)
    hp = pl.program_id(1)
    qi = pl.program_id(2)

    @pl.when(qi == 0)
    def _():
        _mask_heads(k_ref, v_ref, km_ref, vm_ref)

    q = q_ref[0]
    row = lax.broadcasted_iota(jnp.int32, (tq, tq), 0)
    col = lax.broadcasted_iota(jnp.int32, (tq, tq), 1)
    causal = col <= row

    def block(kb, state, diag):
        ks = pl.multiple_of(kb * tq, tq)
        new_state = []
        for h in range(2):
            m, l, acc = state[h]
            s = _dot_nt(q, km_ref[h, pl.ds(ks, tq), :]) - cum_ref[0, 0, h:h + 1, pl.ds(ks, tq)]
            if diag:
                s = jnp.where(causal, s, -1e30)
            m_new = jnp.maximum(m, jnp.max(s, axis=-1, keepdims=True))
            p = jnp.exp2(s - m_new)
            alpha = jnp.exp2(m - m_new)
            l = alpha * l + jnp.sum(p, axis=-1, keepdims=True)
            acc = alpha * acc + _dot(p.astype(BF16), vm_ref[h, pl.ds(ks, tq), :])
            new_state.append((m_new, l, acc))
        return tuple(new_state)

    init = (jnp.full((tq, 1), -1e30, F32), jnp.zeros((tq, 1), F32), jnp.zeros((tq, LANES), F32))
    state = block(qi, (init, init), True)

    zbound = zb_ref[0]

    def first_block(h):
        base = (bi * FOX_HEADS + hp * 2 + h) * n_kb
        f_q = fend_ref[base + jnp.maximum(qi - 1, 0)]

        def keep(kb):
            gap = zbound + f_q - fend_ref[base + jnp.maximum(kb - 1, 0)]
            return jnp.logical_and(kb > 0, gap >= LOG2_NEGLIGIBLE)

        return lax.while_loop(keep, lambda kb: kb - 1, qi)

    kb_lo = jnp.minimum(first_block(0), first_block(1))
    state = lax.fori_loop(kb_lo, qi, lambda kb, st: block(kb, st, False), state)
    o_ref[0] = state[0][2] / state[0][1] + state[1][2] / state[1][1]


def _fox_attention(q, k, v, cum4, fend, zbound):
    b, s, _ = q.shape
    tq = TQ
    n_kb = s // tq
    qspec = pl.BlockSpec((1, tq, LANES), lambda bi, hp, qi, *_: (bi, qi, hp))
    kvspec = pl.BlockSpec((1, s, LANES), lambda bi, hp, qi, *_: (bi, 0, hp))
    grid_spec = pltpu.PrefetchScalarGridSpec(
        num_scalar_prefetch=2,
        grid=(b, FOX_DIM // LANES, n_kb),
        in_specs=[qspec, kvspec, kvspec,
                  pl.BlockSpec((1, 1, SUBLANES, s), lambda bi, hp, qi, *_: (bi, hp, 0, 0))],
        out_specs=qspec,
        scratch_shapes=[pltpu.VMEM((2, s, LANES), BF16), pltpu.VMEM((2, s, LANES), BF16)],
    )
    return pl.pallas_call(
        functools.partial(_fox_kernel, n_kb),
        grid_spec=grid_spec,
        out_shape=jax.ShapeDtypeStruct((b, s, FOX_DIM), F32),
        compiler_params=_cparams(3),
        name="fox_attention",
    )(fend, zbound, q, k, v, cum4)


def _outproj_kernel(tiles_per_seq, x_ref, cb_ref, v_ref, vprev_ref, ysb_ref, yfox_ref, cw_ref, og_ref, wout_ref,
                    mod_ref, g2_ref, rw_ref, rb_ref, bd_ref, ltri_ref,
                    x1_ref, h2t_ref, sel_ref, gate_ref, cnt_ref, vext_ref, carry_ref):
    tm = x_ref.shape[0]
    i = pl.program_id(0)

    @pl.when(i == 0)
    def _():
        carry_ref[...] = jnp.zeros_like(carry_ref)

    first = (i % tiles_per_seq) == 0
    vext_ref[0:SUBLANES, :] = jnp.where(first, 0.0, vprev_ref[...])
    vext_ref[SUBLANES:SUBLANES + tm, :] = v_ref[...]
    conv = (cw_ref[0:1, :] * vext_ref[SUBLANES - 2:SUBLANES - 2 + tm, :]
            + cw_ref[1:2, :] * vext_ref[SUBLANES - 1:SUBLANES - 1 + tm, :]
            + cw_ref[2:3, :] * v_ref[...])
    y = jnp.concatenate([cb_ref[...] * conv, ysb_ref[...], yfox_ref[...]], axis=-1)

    ysq = (y * y).astype(BF16)
    bd = bd_ref[...]
    ss = jnp.concatenate([_dot(ysq[:, c:c + 2 * LANES], bd) for c in range(0, D_MIX, 2 * LANES)], axis=-1)
    yn = y * lax.rsqrt(ss * (1.0 / HEAD_DIM) + EPS) * og_ref[...]
    x1 = x_ref[...] + mod_ref[0, 2:3, :] * _dot(yn.astype(BF16), wout_ref[...])
    x1_ref[...] = x1

    ms = jnp.mean(x1 * x1, axis=-1, keepdims=True)
    h2 = (x1 * lax.rsqrt(ms + EPS)) * g2_ref[...]
    h2 = h2 * (1.0 + mod_ref[0, 4:5, :]) + mod_ref[0, 3:4, :]
    for j in range(ROW_TILES):
        h2t_ref[pl.ds(j, tm, stride=SUBLANES), :] = h2[:, j * LANES:(j + 1) * LANES]

    logits = jnp.dot(h2, rw_ref[...], precision=lax.Precision.HIGHEST, preferred_element_type=F32) + rb_ref[...]
    lane = lax.broadcasted_iota(jnp.int32, (tm, LANES), 1)
    lane_f = lane.astype(F32)
    neg = jnp.float32(-jnp.inf)
    cur = jnp.where(lane < N_EXPERTS, logits, neg)
    tops, sels = [], []
    for _ in range(TOP_K):
        mk = jnp.max(cur, axis=-1, keepdims=True)
        ik = jnp.min(jnp.where(cur == mk, lane_f, float(LANES)), axis=-1, keepdims=True)
        sel = lane_f == ik
        cur = jnp.where(sel, neg, cur)
        tops.append(mk)
        sels.append((sel, ik.astype(jnp.int32)))
    es = [jnp.exp(t - tops[0]) for t in tops]
    inv = 1.0 / (es[0] + es[1] + es[2] + es[3])

    multi = jnp.zeros((tm, LANES), F32)
    for sel, _ in sels:
        multi = jnp.where(sel, 1.0, multi)
    before = _dot(ltri_ref[...], multi.astype(BF16)) + carry_ref[0:1, :]
    carry_ref[0:1, :] = before[tm - 1:tm, :] + multi[tm - 1:tm, :]

    sel_out = jnp.zeros((tm, LANES), jnp.int32)
    gate_out = jnp.zeros((tm, LANES), F32)
    for k, (sel, ik) in enumerate(sels):
        rank = jnp.sum(jnp.where(sel, before, 0.0), axis=-1, keepdims=True).astype(jnp.int32)
        sel_out = jnp.where(lane == k, ik, sel_out)
        sel_out = jnp.where(lane == TOP_K + k, rank, sel_out)
        gate_out = jnp.where(lane == k, es[k] * inv, gate_out)
    sel_ref[...] = sel_out
    gate_ref[...] = gate_out
    cnt_ref[...] = jnp.broadcast_to(carry_ref[0:1, :], cnt_ref.shape)


def _outproj(x2, cb, v, ysb, yfox, conv_w, og, w_out_b, mod3, g2, rw, rb, bd2, ltri, seq):
    t = x2.shape[0]
    tm = TM_PROJ
    tiles_per_seq = seq // tm
    row = lambda i: (i, 0)
    const = lambda i: (0, 0)
    return pl.pallas_call(
        functools.partial(_outproj_kernel, tiles_per_seq),
        grid=(t // tm,),
        in_specs=[pl.BlockSpec((tm, D_MODEL), row),
                  pl.BlockSpec((tm, CONV_CH), row),
                  pl.BlockSpec((tm, CONV_CH), row),
                  pl.BlockSpec((SUBLANES, CONV_CH), lambda i: (jnp.maximum(i * (tm // SUBLANES) - 1, 0), 0)),
                  pl.BlockSpec((tm, SB_DIM), row),
                  pl.BlockSpec((tm, FOX_DIM), row),
                  pl.BlockSpec((CONV_W, CONV_CH), const),
                  pl.BlockSpec((1, D_MIX), const),
                  pl.BlockSpec((D_MIX, D_MODEL), const),
                  pl.BlockSpec((1, N_MOD, D_MODEL), lambda i: (i // tiles_per_seq, 0, 0)),
                  pl.BlockSpec((1, D_MODEL), const),
                  pl.BlockSpec((D_MODEL, LANES), const),
                  pl.BlockSpec((1, LANES), const),
                  pl.BlockSpec((2 * LANES, 2 * LANES), const),
                  pl.BlockSpec((tm, tm), const)],
        out_specs=[pl.BlockSpec((tm, D_MODEL), row),
                   pl.BlockSpec((tm * SUBLANES, LANES), row),
                   pl.BlockSpec((tm, LANES), row),
                   pl.BlockSpec((tm, LANES), row),
                   pl.BlockSpec((SUBLANES, LANES), const)],
        out_shape=[jax.ShapeDtypeStruct((t, D_MODEL), F32),
                   jax.ShapeDtypeStruct((t * SUBLANES, LANES), F32),
                   jax.ShapeDtypeStruct((t, LANES), jnp.int32),
                   jax.ShapeDtypeStruct((t, LANES), F32),
                   jax.ShapeDtypeStruct((SUBLANES, LANES), F32)],
        scratch_shapes=[pltpu.VMEM((tm + SUBLANES, CONV_CH), F32), pltpu.VMEM((SUBLANES, LANES), F32)],
        compiler_params=_cparams(1),
        name="outproj_router",
    )(x2, cb, v, v, ysb, yfox, conv_w, og, w_out_b, mod3, g2, rw, rb, bd2, ltri)


def _row_copy(src, src_row, dst, dst_row, sem):
    return pltpu.make_async_copy(src.at[pl.ds(pl.multiple_of(src_row * SUBLANES, SUBLANES), SUBLANES), :],
                                 dst.at[pl.ds(pl.multiple_of(dst_row * SUBLANES, SUBLANES), SUBLANES), :], sem)


def _zero_fill(meta_ref, zero_ref, xg_ref, zsem):
    n_tiles = xg_ref.shape[0] // (TM_GROUP * SUBLANES)
    zero_ref[...] = jnp.zeros_like(zero_ref)

    def sweep(wait):
        def per_expert(e, c):
            pos = meta_ref[N_EXPERTS + e]
            n = meta_ref[2 * N_EXPERTS + e]
            bit = TM_GROUP // 2
            while bit:
                @pl.when((n & bit) != 0)
                def _(pos=pos, bit=bit):
                    cp = pltpu.make_async_copy(
                        zero_ref.at[pl.ds(0, bit * SUBLANES), :],
                        xg_ref.at[pl.ds(pl.multiple_of(pos * SUBLANES, SUBLANES), bit * SUBLANES), :], zsem)
                    cp.wait() if wait else cp.start()
                pos = pos + (n & bit)
                bit //= 2
            return c

        lax.fori_loop(0, N_EXPERTS, per_expert, 0)

        def per_tile(j, c):
            rows = TM_GROUP * SUBLANES
            cp = pltpu.make_async_copy(zero_ref, xg_ref.at[pl.ds(pl.multiple_of(j * rows, rows), rows), :], zsem)
            cp.wait() if wait else cp.start()
            return c

        lax.fori_loop(meta_ref[3 * N_EXPERTS], n_tiles, per_tile, 0)

    sweep(False)
    sweep(True)


def _dispatch_kernel(meta_ref, pos_ref, h_ref, xg_ref, zero_ref, sem, zsem):
    tm = h_ref.shape[0] // SUBLANES

    @pl.when(pl.program_id(0) == 0)
    def _():
        _zero_fill(meta_ref, zero_ref, xg_ref, zsem)

    def issue(r, c):
        for k in range(TOP_K):
            _row_copy(h_ref, r, xg_ref, pos_ref[r * TOP_K + k], sem).start()
        return c

    lax.fori_loop(0, tm, issue, 0)
    n = tm * TOP_K * SUBLANES
    pltpu.make_async_copy(xg_ref.at[pl.ds(0, n), :], xg_ref.at[pl.ds(0, n), :], sem).wait()


def _dispatch(meta, pos, h2t, n_rows):
    t = h2t.shape[0] // SUBLANES
    tm = TM_DISPATCH
    grid_spec = pltpu.PrefetchScalarGridSpec(
        num_scalar_prefetch=1,
        grid=(t // tm,),
        in_specs=[pl.BlockSpec((tm * TOP_K,), lambda i, *_: (i,), memory_space=pltpu.SMEM),
                  pl.BlockSpec((tm * SUBLANES, LANES), lambda i, *_: (i, 0))],
        out_specs=pl.BlockSpec(memory_space=pl.ANY),
        scratch_shapes=[pltpu.VMEM((TM_GROUP * SUBLANES, LANES), F32),
                        pltpu.SemaphoreType.DMA(()), pltpu.SemaphoreType.DMA(())],
    )
    return pl.pallas_call(
        _dispatch_kernel,
        grid_spec=grid_spec,
        out_shape=jax.ShapeDtypeStruct((n_rows * SUBLANES, LANES), F32),
        compiler_params=_cparams(1),
        name="moe_dispatch",
    )(meta, pos, h2t)


def _expert_kernel(te_ref, nv_ref, xg_ref, w1_ref, b1_ref, w2_ref, b2_ref, yg_ref, wg_ref, wl_ref, w2b_ref, wt_ref):
    tmg = xg_ref.shape[0] // SUBLANES
    j = pl.program_id(0)
    nv = nv_ref[j]

    @pl.when(jnp.logical_or(j == 0, te_ref[j] != te_ref[jnp.maximum(j - 1, 0)]))
    def _():
        ncol = wt_ref.shape[0] // ROW_TILES
        half = ncol // 2
        for n0 in range(0, 2 * D_EXPERT, ncol):
            wt = w1_ref[0, 0, :, n0:n0 + ncol].T
            for c in range(ROW_TILES):
                wt_ref[pl.ds(c, ncol, stride=ROW_TILES), :] = wt[:, c * LANES:(c + 1) * LANES]
            rows = slice(n0 // 2, n0 // 2 + half)
            for c in range(ROW_TILES):
                cols = slice(c * LANES, (c + 1) * LANES)
                wg_ref[rows, cols] = wt_ref[pl.ds(c, half, stride=2 * ROW_TILES), :].astype(BF16)
                wl_ref[rows, cols] = wt_ref[pl.ds(ROW_TILES + c, half, stride=2 * ROW_TILES), :].astype(BF16)
            w2b_ref[rows, :] = w2_ref[0, 0, rows, :].astype(BF16)

    @pl.when(nv == 0)
    def _():
        yg_ref[...] = jnp.zeros_like(yg_ref)

    @pl.when(nv > 0)
    def _():
        x = jnp.concatenate([xg_ref[pl.ds(c, tmg, stride=SUBLANES), :] for c in range(ROW_TILES)], axis=-1)
        x = x.astype(BF16)
        glu = jnp.minimum(_dot_nt(x, wg_ref[...]) + b1_ref[0, 0:1, :], SWIGLU_LIMIT)
        lin = jnp.clip(_dot_nt(x, wl_ref[...]) + b1_ref[0, 1:2, :], -SWIGLU_LIMIT, SWIGLU_LIMIT)
        a = glu * (1.0 / (1.0 + jnp.exp(-SWIGLU_ALPHA * glu))) * (lin + 1.0)
        y = _dot(a.astype(BF16), w2b_ref[...]) + b2_ref[0]
        for c in range(ROW_TILES):
            yg_ref[pl.ds(c, tmg, stride=SUBLANES), :] = y[:, c * LANES:(c + 1) * LANES]


def _experts(te, nv, xg, layer, w1, b1, w2, b2):
    tmg = TM_GROUP
    n_tiles = xg.shape[0] // (tmg * SUBLANES)
    grid_spec = pltpu.PrefetchScalarGridSpec(
        num_scalar_prefetch=2,
        grid=(n_tiles,),
        in_specs=[pl.BlockSpec((tmg * SUBLANES, LANES), lambda j, te, nv: (j, 0)),
                  pl.BlockSpec((1, 1, D_MODEL, 2 * D_EXPERT), lambda j, te, nv: (layer, te[j], 0, 0)),
                  pl.BlockSpec((1, 2, D_EXPERT), lambda j, te, nv: (te[j], 0, 0)),
                  pl.BlockSpec((1, 1, D_EXPERT, D_MODEL), lambda j, te, nv: (layer, te[j], 0, 0)),
                  pl.BlockSpec((1, 1, D_MODEL), lambda j, te, nv: (te[j], 0, 0))],
        out_specs=pl.BlockSpec((tmg * SUBLANES, LANES), lambda j, te, nv: (j, 0)),
        scratch_shapes=[pltpu.VMEM((D_EXPERT, D_MODEL), BF16), pltpu.VMEM((D_EXPERT, D_MODEL), BF16),
                        pltpu.VMEM((D_EXPERT, D_MODEL), BF16), pltpu.VMEM((2 * LANES * ROW_TILES, LANES), F32)],
    )
    return pl.pallas_call(
        _expert_kernel,
        grid_spec=grid_spec,
        out_shape=jax.ShapeDtypeStruct(xg.shape, F32),
        compiler_params=_cparams(1, EXPERT_VMEM_LIMIT),
        name="moe_experts",
    )(te, nv, xg, w1, b1, w2, b2)


def _combine_kernel(pos_ref, pos_next_ref, gate_ref, x1_ref, mod_ref, yg_ref, o_ref, buf_ref, sem):
    tm = x1_ref.shape[0]
    i = pl.program_id(0)
    slot = i % 2

    def gather(idx_ref, to_slot):
        def issue(r, c):
            for k in range(TOP_K):
                _row_copy(yg_ref, idx_ref[r * TOP_K + k], buf_ref.at[to_slot, k], r, sem.at[to_slot]).start()
            return c

        lax.fori_loop(0, tm, issue, 0)

    @pl.when(i == 0)
    def _():
        gather(pos_ref, 0)

    @pl.when(i + 1 < pl.num_programs(0))
    def _():
        gather(pos_next_ref, 1 - slot)

    for k in range(TOP_K):
        pltpu.make_async_copy(buf_ref.at[slot, k], buf_ref.at[slot, k], sem.at[slot]).wait()

    gates = gate_ref[...]
    gate_cols = [gates[:, k:k + 1] for k in range(TOP_K)]
    g2 = mod_ref[0, 5:6, :]
    for c in range(ROW_TILES):
        acc = jnp.zeros((tm, LANES), F32)
        for k in range(TOP_K):
            acc = acc + gate_cols[k] * buf_ref[slot, k, pl.ds(c, tm, stride=SUBLANES), :]
        cols = slice(c * LANES, (c + 1) * LANES)
        o_ref[:, cols] = x1_ref[:, cols] + g2[:, cols] * acc


def _combine(pos, gates, x1, mod3, yg, seq):
    t = x1.shape[0]
    tm = TM_COMBINE
    tiles_per_seq = seq // tm
    n = t // tm
    grid_spec = pltpu.PrefetchScalarGridSpec(
        num_scalar_prefetch=0,
        grid=(n,),
        in_specs=[pl.BlockSpec((tm * TOP_K,), lambda i: (i,), memory_space=pltpu.SMEM),
                  pl.BlockSpec((tm * TOP_K,), lambda i: (jnp.minimum(i + 1, n - 1),), memory_space=pltpu.SMEM),
                  pl.BlockSpec((tm, LANES), lambda i: (i, 0)),
                  pl.BlockSpec((tm, D_MODEL), lambda i: (i, 0)),
                  pl.BlockSpec((1, N_MOD, D_MODEL), lambda i: (i // tiles_per_seq, 0, 0)),
                  pl.BlockSpec(memory_space=pl.ANY)],
        out_specs=pl.BlockSpec((tm, D_MODEL), lambda i: (i, 0)),
        scratch_shapes=[pltpu.VMEM((2, TOP_K, tm * SUBLANES, LANES), F32), pltpu.SemaphoreType.DMA((2,))],
    )
    return pl.pallas_call(
        _combine_kernel,
        grid_spec=grid_spec,
        out_shape=jax.ShapeDtypeStruct((t, D_MODEL), F32),
        compiler_params=_cparams(1),
        name="moe_combine",
    )(pos, pos, gates, x1, mod3, yg)


def _block_diag_ones(n):
    i = np.arange(n)
    return jnp.asarray((i[:, None] // HEAD_DIM) == (i[None, :] // HEAD_DIM), dtype=BF16)


def _tri(n, strict, upper):
    i = np.arange(n)
    if upper:
        m = (i[:, None] > i[None, :]) if strict else (i[:, None] >= i[None, :])
        return jnp.asarray(m, dtype=BF16)
    m = (i[None, :] < i[:, None]) if strict else (i[None, :] <= i[:, None])
    return jnp.asarray(m, dtype=BF16)


def _tile2(g):
    return jnp.tile(g, SB_DIM // HEAD_DIM)


def _layer(x2, c, seq, layer, w_mlp1, w_mlp2, norm1_g, w_ada, b_ada, w_in, conv_w, sb_q_g, sb_k_g, fox_q_g, fox_k_g, fox_f_b,
           out_norm_g, w_out, norm2_g, router_w, router_b, b_mlp1, b_mlp2):
    t = x2.shape[0]
    nb = t // seq
    scale = HEAD_DIM ** -0.5

    w_in_p = jnp.pad(w_in, ((0, 0), (0, D_IN_PAD - w_in.shape[1]))).astype(BF16)
    fb = jnp.pad(fox_f_b, (0, LANES - FOX_HEADS)).reshape(1, LANES)
    qscale = scale * LOG2_E
    gains = jnp.stack([_tile2(sb_q_g) * qscale, _tile2(sb_k_g), _tile2(fox_q_g) * qscale, _tile2(fox_k_g)])
    gains = jnp.pad(gains, ((0, SUBLANES - 4), (0, 0)))
    rw = jnp.pad(router_w, ((0, 0), (0, LANES - N_EXPERTS)))
    rb = jnp.pad(router_b, (0, LANES - N_EXPERTS)).reshape(1, LANES)
    b1 = jnp.stack([b_mlp1[:, 0::2], b_mlp1[:, 1::2]], axis=1)
    b2 = b_mlp2.reshape(N_EXPERTS, 1, D_MODEL)

    mod3 = _modulation(c, w_ada, b_ada).reshape(nb, N_MOD, D_MODEL)

    cb, v, sbq, sbk, sbv, fq, fk, fv, cum = _inproj(
        x2, mod3, norm1_g.reshape(1, D_MODEL), w_in_p, _block_diag_ones(SB_DIM),
        _tri(TM_PROJ, strict=False, upper=False), fb, gains, seq)

    r3 = lambda a: a.reshape(nb, seq, SB_DIM)
    ysb = _sb_attention(r3(sbq), r3(sbk), r3(sbv), _tri(TQ, strict=True, upper=True))

    n_kb = seq // TQ
    cum4 = jnp.pad(cum[:, :FOX_HEADS, :].reshape(nb, FOX_DIM // LANES, 2, seq),
                   ((0, 0), (0, 0), (0, SUBLANES - 2), (0, 0)))
    fend = cum[:, :FOX_HEADS, TQ - 1::TQ].reshape(nb * FOX_HEADS * n_kb)
    zbound = (2.0 * 1.02 * HEAD_DIM * qscale * jnp.max(jnp.abs(fox_q_g)) * jnp.max(jnp.abs(fox_k_g))).reshape(1)
    yfox = _fox_attention(r3(fq), r3(fk), r3(fv), cum4, fend, zbound)

    x1, h2t, sel, gates, cnt = _outproj(
        x2, cb, v, ysb.reshape(t, SB_DIM), yfox.reshape(t, FOX_DIM), conv_w, out_norm_g.reshape(1, D_MIX),
        w_out.astype(BF16), mod3, norm2_g.reshape(1, D_MODEL), rw, rb, _block_diag_ones(2 * LANES),
        _tri(TM_PROJ, strict=True, upper=False), seq)

    counts = cnt[0, :N_EXPERTS].astype(jnp.int32)
    padded = ((counts + TM_GROUP - 1) // TM_GROUP) * TM_GROUP
    off_end = jnp.cumsum(padded)
    off = off_end - padded
    n_rows = t * TOP_K + N_EXPERTS * TM_GROUP
    n_tiles = n_rows // TM_GROUP
    start = jnp.arange(n_tiles, dtype=jnp.int32) * TM_GROUP
    te = jnp.minimum(jnp.sum(start[:, None] >= off_end[None, :], axis=1), N_EXPERTS - 1).astype(jnp.int32)
    nv = jnp.clip(off[te] + counts[te] - start, 0, TM_GROUP).astype(jnp.int32)
    pos = (off[sel[:, :TOP_K]] + sel[:, TOP_K:2 * TOP_K]).astype(jnp.int32).reshape(t * TOP_K)
    meta = jnp.concatenate([off, off + counts, padded - counts, off_end[-1:] // TM_GROUP]).astype(jnp.int32)

    xg = _dispatch(meta, pos, h2t, n_rows)
    yg = _experts(te, nv, xg, layer, w_mlp1, b1, w_mlp2, b2)
    return _combine(pos, gates, x1, mod3, yg, seq)


def kernel(x, c, norm1_g, w_ada, b_ada, w_in, conv_w, sb_q_g, sb_k_g, fox_q_g, fox_k_g, fox_f_b, out_norm_g, w_out,
           norm2_g, router_w, router_b, w_mlp1, b_mlp1, w_mlp2, b_mlp2):
    b, s, d = x.shape
    x2 = x.reshape(b * s, d)
    params = (norm1_g, w_ada, b_ada, w_in, conv_w, sb_q_g, sb_k_g, fox_q_g, fox_k_g, fox_f_b, out_norm_g, w_out,
              norm2_g, router_w, router_b, b_mlp1, b_mlp2)
    for layer in range(norm1_g.shape[0]):
        x2 = _layer(x2, c, s, layer, w_mlp1, w_mlp2, *(p[layer] for p in params))
    return x2.reshape(b, s, d)
```

```python
import functools

import numpy as np
import jax
import jax.numpy as jnp
from jax import lax
from jax.experimental import pallas as pl
from jax.experimental.pallas import tpu as pltpu

F32 = jnp.float32
BF16 = jnp.bfloat16

D_MODEL = 1024
HEAD_DIM = 64
CONV_CH = 256
CONV_W = 3
SB_DIM = 384
FOX_DIM = 384
FOX_HEADS = 6
N_PAIRS = 3
D_MIX = 1024
N_MOD = 6
N_EXPERTS = 32
TOP_K = 4
D_EXPERT = 1024
SWIGLU_ALPHA = 1.702
SWIGLU_LIMIT = 7.0
EPS = 1e-6

LANES = 128
SUBLANES = 8
ROW_TILES = D_MODEL // LANES
D_IN_PAD = 3 * CONV_CH + 3 * SB_DIM + 3 * FOX_DIM + LANES
COL_SBQ = 3 * CONV_CH
COL_SBK = COL_SBQ + SB_DIM
COL_SBV = COL_SBK + SB_DIM
COL_FQ = COL_SBV + SB_DIM
COL_FK = COL_FQ + FOX_DIM
COL_FV = COL_FK + FOX_DIM
COL_FLOG = COL_FV + FOX_DIM

TM_PROJ = 512
TQ = 256
TB = 128
TM_DISPATCH = 1024
TM_COMBINE = 256
TM_GROUP = 512
LOG2_E = 1.4426950408889634
LOG2_NEGLIGIBLE = -64.0
VMEM_LIMIT = 48 * 1024 * 1024
EXPERT_VMEM_LIMIT = 58 * 1024 * 1024


def _cparams(n_axes, vmem=VMEM_LIMIT):
    return pltpu.CompilerParams(dimension_semantics=("arbitrary",) * n_axes, vmem_limit_bytes=vmem)


def _dot(a, b):
    return jnp.dot(a, b, preferred_element_type=F32)


def _dot_nt(a, b):
    return lax.dot_general(a, b, (((1,), (1,)), ((), ())), preferred_element_type=F32)


def _split2(x):
    hi = x.astype(BF16)
    lo = (x - hi.astype(F32)).astype(BF16)
    return hi, lo


def _log_sigmoid(x):
    return jnp.minimum(x, 0.0) - jnp.log(1.0 + jnp.exp(-jnp.abs(x)))


def _mod_kernel(c_ref, w_ref, b_ref, o_ref):
    c = c_ref[...]
    s = c / (1.0 + jnp.exp(-c))
    o_ref[...] = _dot(s.astype(BF16), w_ref[...].astype(BF16)) + b_ref[...]


def _modulation(c, w_ada, b_ada):
    b, d = c.shape
    n = w_ada.shape[1]
    tn = 1024
    return pl.pallas_call(
        _mod_kernel,
        grid=(n // tn,),
        in_specs=[pl.BlockSpec((b, d), lambda j: (0, 0)),
                  pl.BlockSpec((d, tn), lambda j: (0, j)),
                  pl.BlockSpec((1, tn), lambda j: (0, j))],
        out_specs=pl.BlockSpec((b, tn), lambda j: (0, j)),
        out_shape=jax.ShapeDtypeStruct((b, n), F32),
        compiler_params=_cparams(1),
        name="modulation",
    )(c, w_ada, b_ada.reshape(1, n))


def _inproj_kernel(tiles_per_seq, x_ref, mod_ref, g1_ref, w_ref, bd_ref, tri_ref, fb_ref, gains_ref,
                   cb_ref, v_ref, sbq_ref, sbk_ref, sbv_ref, fq_ref, fk_ref, fv_ref, cum_ref, carry_ref):
    tm = x_ref.shape[0]
    x = x_ref[...]
    ms = jnp.mean(x * x, axis=-1, keepdims=True)
    h = (x * lax.rsqrt(ms + EPS)) * g1_ref[...]
    h = h * (1.0 + mod_ref[0, 1:2, :]) + mod_ref[0, 0:1, :]
    hb = h.astype(BF16)

    pc = _dot(hb, w_ref[:, 0:3 * CONV_CH])
    cb_ref[...] = pc[:, 0:CONV_CH]
    v_ref[...] = pc[:, CONV_CH:2 * CONV_CH] * pc[:, 2 * CONV_CH:3 * CONV_CH]

    def normed(col, row):
        p = _dot(hb, w_ref[:, col:col + SB_DIM])
        ss = _dot((p * p).astype(BF16), bd_ref[...])
        return (p * lax.rsqrt(ss * (1.0 / HEAD_DIM) + EPS) * gains_ref[row:row + 1, :]).astype(BF16)

    sbq_ref[...] = normed(COL_SBQ, 0)
    sbk_ref[...] = normed(COL_SBK, 1)
    sbv_ref[...] = _dot(hb, w_ref[:, COL_SBV:COL_SBV + SB_DIM]).astype(BF16)
    fq_ref[...] = normed(COL_FQ, 2)
    fk_ref[...] = normed(COL_FK, 3)
    fv_ref[...] = _dot(hb, w_ref[:, COL_FV:COL_FV + FOX_DIM]).astype(BF16)

    lf = _log_sigmoid(_dot(hb, w_ref[:, COL_FLOG:COL_FLOG + LANES]) + fb_ref[...]) * LOG2_E
    t0 = lf.astype(BF16)
    r1 = lf - t0.astype(F32)
    t1 = r1.astype(BF16)
    t2 = (r1 - t1.astype(F32)).astype(BF16)
    tri = tri_ref[...]
    cs = _dot(tri, t0) + _dot(tri, t1) + _dot(tri, t2)

    @pl.when(pl.program_id(0) % tiles_per_seq == 0)
    def _():
        carry_ref[...] = jnp.zeros_like(carry_ref)

    cum = cs + carry_ref[0:1, :]
    carry_ref[0:1, :] = cum[tm - 1:tm, :]
    cum_ref[0] = cum.T[0:SUBLANES, :]


def _inproj(x2, mod3, g1, w_in_p, bd, tri, fb, gains, seq):
    t = x2.shape[0]
    tm = TM_PROJ
    tiles_per_seq = seq // tm
    nb = t // seq
    row = lambda i: (i, 0)
    const = lambda i: (0, 0)
    outs = [jax.ShapeDtypeStruct((t, CONV_CH), F32), jax.ShapeDtypeStruct((t, CONV_CH), F32)]
    outs += [jax.ShapeDtypeStruct((t, SB_DIM), BF16)] * 6
    outs += [jax.ShapeDtypeStruct((nb, SUBLANES, seq), F32)]
    out_specs = [pl.BlockSpec((tm, CONV_CH), row)] * 2 + [pl.BlockSpec((tm, SB_DIM), row)] * 6
    out_specs += [pl.BlockSpec((1, SUBLANES, tm), lambda i: (i // tiles_per_seq, 0, i % tiles_per_seq))]
    return pl.pallas_call(
        functools.partial(_inproj_kernel, tiles_per_seq),
        grid=(t // tm,),
        in_specs=[pl.BlockSpec((tm, D_MODEL), row),
                  pl.BlockSpec((1, N_MOD, D_MODEL), lambda i: (i // tiles_per_seq, 0, 0)),
                  pl.BlockSpec((1, D_MODEL), const),
                  pl.BlockSpec((D_MODEL, D_IN_PAD), const),
                  pl.BlockSpec((SB_DIM, SB_DIM), const),
                  pl.BlockSpec((tm, tm), const),
                  pl.BlockSpec((1, LANES), const),
                  pl.BlockSpec((SUBLANES, SB_DIM), const)],
        out_specs=out_specs,
        out_shape=outs,
        scratch_shapes=[pltpu.VMEM((SUBLANES, LANES), F32)],
        compiler_params=_cparams(1),
        name="inproj",
    )(x2, mod3, g1, w_in_p, bd, tri, fb, gains)


def _stack_heads(k_ref, v_ref, kk_ref, vv_ref):
    n_blocks = kk_ref.shape[1]
    lane = lax.broadcasted_iota(jnp.int32, (TB, LANES), 1)
    first = lane < HEAD_DIM

    def body(j, c):
        rows = pl.ds(pl.multiple_of(j * TB, TB), TB)
        for src, dst in ((k_ref, kk_ref), (v_ref, vv_ref)):
            for p in range(N_PAIRS):
                blk = src[0, rows, p * LANES:(p + 1) * LANES]
                zero = jnp.zeros_like(blk)
                dst[p, j, 0:TB, :] = jnp.where(first, blk, zero)
                dst[p, j, TB:2 * TB, :] = jnp.where(first, zero, blk)
        return c

    lax.fori_loop(0, n_blocks, body, 0)


def _per_head(a0, a1):
    return jnp.concatenate([jnp.broadcast_to(a0, (TB, TB)), jnp.broadcast_to(a1, (TB, TB))], axis=1)


def _pair_iota():
    row = lax.broadcasted_iota(jnp.int32, (TB, 2 * TB), 0)
    col = lax.broadcasted_iota(jnp.int32, (TB, 2 * TB), 1) % TB
    return row, col


_CHAINS = [(p, s) for p in range(N_PAIRS) for s in range(2)]


def _q_block(q_ref, p, s):
    return q_ref[0, s * TB:(s + 1) * TB, p * LANES:(p + 1) * LANES]


def _sb_kernel(q_ref, k_ref, v_ref, uu_ref, o_ref, kk_ref, vv_ref):
    qi = pl.program_id(1)

    @pl.when(qi == 0)
    def _():
        _stack_heads(k_ref, v_ref, kk_ref, vv_ref)

    uu = uu_ref[...]
    row, col = _pair_iota()
    strict = col < row

    def block(p, s, kb, valid, state, diag):
        rs0, rs1, acc = state
        kb = jnp.maximum(kb, 0)
        z = _dot_nt(_q_block(q_ref, p, s), kk_ref[p, kb])
        log_beta = jnp.minimum(z, 0.0) - jnp.log2(1.0 + jnp.exp2(-jnp.abs(z)))
        log_1mb = log_beta - z
        keep = strict if diag else None
        if valid is not None:
            keep = valid if keep is None else jnp.logical_and(keep, valid)
        if keep is not None:
            log_1mb = jnp.where(keep, log_1mb, 0.0)
        hi, lo = _split2(log_1mb)
        rest = _dot(hi, uu) + _dot(lo, uu) + _per_head(rs0, rs1)
        w = jnp.exp2(log_beta + rest)
        if keep is not None:
            w = jnp.where(keep, w, 0.0)
        acc = acc + _dot(w.astype(BF16), vv_ref[p, kb])
        rs0 = rs0 + jnp.sum(log_1mb[:, :TB], axis=-1, keepdims=True)
        rs1 = rs1 + jnp.sum(log_1mb[:, TB:], axis=-1, keepdims=True)
        return rs0, rs1, acc

    def more(states):
        top = states[0][0]
        for st in states:
            top = jnp.maximum(top, jnp.maximum(st[0], st[1]))
        return (jnp.max(top) > LOG2_NEGLIGIBLE).astype(jnp.int32)

    zero = jnp.zeros((TB, 1), F32)
    init = (zero, zero, jnp.zeros((TB, LANES), F32))
    states = [block(p, s, 2 * qi + s, None, init, True) for p, s in _CHAINS]
    states = [block(p, s, 2 * qi + s - 1, (qi > 0) if s == 0 else None, st, False)
              for (p, s), st in zip(_CHAINS, states)]

    def cond(c):
        return jnp.logical_and(2 * qi + 1 - c[0] >= 0, c[2] > 0)

    def body(c):
        back, states, _ = c
        states = [block(p, s, 2 * qi + s - back, 2 * qi + s - back >= 0, st, False)
                  for (p, s), st in zip(_CHAINS, states)]
        return back + 1, states, more(states)

    _, states, _ = lax.while_loop(cond, body, (jnp.int32(2), states, more(states)))
    for (p, s), st in zip(_CHAINS, states):
        o_ref[0, s * TB:(s + 1) * TB, p * LANES:(p + 1) * LANES] = st[2]


def _sb_attention(q, k, v, uu):
    b, s, d = q.shape
    tq = TQ
    qspec = pl.BlockSpec((1, tq, d), lambda bi, qi: (bi, qi, 0))
    kvspec = pl.BlockSpec((1, s, d), lambda bi, qi: (bi, 0, 0))
    scratch = pltpu.VMEM((N_PAIRS, s // TB, 2 * TB, LANES), BF16)
    return pl.pallas_call(
        _sb_kernel,
        grid=(b, s // tq),
        in_specs=[qspec, kvspec, kvspec, pl.BlockSpec((2 * TB, 2 * TB), lambda bi, qi: (0, 0))],
        out_specs=qspec,
        out_shape=jax.ShapeDtypeStruct((b, s, d), F32),
        scratch_shapes=[scratch, scratch],
        compiler_params=_cparams(2),
        name="sb_attention",
    )(q, k, v, uu)


def _fox_kernel(n_kb, fend_ref, zb_ref, q_ref, k_ref, v_ref, f2_ref, o_ref, kk_ref, vv_ref):
    bi = pl.program_id(0)
    qi = pl.program_id(1)

    @pl.when(qi == 0)
    def _():
        _stack_heads(k_ref, v_ref, kk_ref, vv_ref)

    row, col = _pair_iota()
    causal = col <= row
    lane = lax.broadcasted_iota(jnp.int32, (TB, LANES), 1)
    first = lane < HEAD_DIM

    def block(p, s, kb, valid, state, diag):
        m0, m1, l0, l1, acc = state
        kb = jnp.maximum(kb, 0)
        sc = _dot_nt(_q_block(q_ref, p, s), kk_ref[p, kb]) - f2_ref[0, p, pl.ds(kb, 1), :]
        keep = causal if diag else None
        if valid is not None:
            keep = valid if keep is None else jnp.logical_and(keep, valid)
        if keep is not None:
            sc = jnp.where(keep, sc, -1e30)
        n0 = jnp.maximum(m0, jnp.max(sc[:, :TB], axis=-1, keepdims=True))
        n1 = jnp.maximum(m1, jnp.max(sc[:, TB:], axis=-1, keepdims=True))
        pr = jnp.exp2(sc - _per_head(n0, n1))
        a0 = jnp.exp2(m0 - n0)
        a1 = jnp.exp2(m1 - n1)
        l0 = a0 * l0 + jnp.sum(pr[:, :TB], axis=-1, keepdims=True)
        l1 = a1 * l1 + jnp.sum(pr[:, TB:], axis=-1, keepdims=True)
        acc = acc * jnp.where(first, a0, a1) + _dot(pr.astype(BF16), vv_ref[p, kb])
        return n0, n1, l0, l1, acc

    low = jnp.full((TB, 1), -1e30, F32)
    zero = jnp.zeros((TB, 1), F32)
    init = (low, low, zero, zero, jnp.zeros((TB, LANES), F32))
    states = [block(p, s, 2 * qi + s, None, init, True) for p, s in _CHAINS]

    zbound = zb_ref[0]

    def first_block(head, blk):
        base = (bi * FOX_HEADS + head) * n_kb
        f_q = fend_ref[base + jnp.maximum(blk - 1, 0)]

        def keep(kb):
            gap = zbound + f_q - fend_ref[base + jnp.maximum(kb - 1, 0)]
            return jnp.logical_and(kb > 0, gap >= LOG2_NEGLIGIBLE)

        return lax.while_loop(keep, lambda kb: kb - 1, blk)

    steps = jnp.int32(0)
    for s in range(2):
        for head in range(FOX_HEADS):
            steps = jnp.maximum(steps, 2 * qi + s - first_block(head, 2 * qi + s))

    def body(back, states):
        return [block(p, s, 2 * qi + s - back, 2 * qi + s - back >= 0, st, False)
                for (p, s), st in zip(_CHAINS, states)]

    states = lax.fori_loop(1, steps + 1, body, states)
    for (p, s), (m0, m1, l0, l1, acc) in zip(_CHAINS, states):
        o_ref[0, s * TB:(s + 1) * TB, p * LANES:(p + 1) * LANES] = acc / jnp.where(first, l0, l1)


def _fox_attention(q, k, v, f2, fend, zbound):
    b, s, d = q.shape
    tq = TQ
    n_kb = s // TB
    qspec = pl.BlockSpec((1, tq, d), lambda bi, qi, *_: (bi, qi, 0))
    kvspec = pl.BlockSpec((1, s, d), lambda bi, qi, *_: (bi, 0, 0))
    scratch = pltpu.VMEM((N_PAIRS, n_kb, 2 * TB, LANES), BF16)
    grid_spec = pltpu.PrefetchScalarGridSpec(
        num_scalar_prefetch=2,
        grid=(b, s // tq),
        in_specs=[qspec, kvspec, kvspec,
                  pl.BlockSpec((1, N_PAIRS, n_kb, 2 * TB), lambda bi, qi, *_: (bi, 0, 0, 0))],
        out_specs=qspec,
        scratch_shapes=[scratch, scratch],
    )
    return pl.pallas_call(
        functools.partial(_fox_kernel, n_kb),
        grid_spec=grid_spec,
        out_shape=jax.ShapeDtypeStruct((b, s, d), F32),
        compiler_params=_cparams(2),
        name="fox_attention",
    )(fend, zbound, q, k, v, f2)


def _outproj_kernel(tiles_per_seq, x_ref, cb_ref, v_ref, vprev_ref, ysb_ref, yfox_ref, cw_ref, og_ref, wout_ref,
                    mod_ref, g2_ref, rw_ref, rb_ref, bd_ref, ltri_ref,
                    x1_ref, h2t_ref, sel_ref, gate_ref, cnt_ref, vext_ref, carry_ref):
    tm = x_ref.shape[0]
    i = pl.program_id(0)

    @pl.when(i == 0)
    def _():
        carry_ref[...] = jnp.zeros_like(carry_ref)

    first = (i % tiles_per_seq) == 0
    vext_ref[0:SUBLANES, :] = jnp.where(first, 0.0, vprev_ref[...])
    vext_ref[SUBLANES:SUBLANES + tm, :] = v_ref[...]
    conv = (cw_ref[0:1, :] * vext_ref[SUBLANES - 2:SUBLANES - 2 + tm, :]
            + cw_ref[1:2, :] * vext_ref[SUBLANES - 1:SUBLANES - 1 + tm, :]
            + cw_ref[2:3, :] * v_ref[...])
    y = jnp.concatenate([cb_ref[...] * conv, ysb_ref[...], yfox_ref[...]], axis=-1)

    ysq = (y * y).astype(BF16)
    bd = bd_ref[...]
    ss = jnp.concatenate([_dot(ysq[:, c:c + 2 * LANES], bd) for c in range(0, D_MIX, 2 * LANES)], axis=-1)
    yn = y * lax.rsqrt(ss * (1.0 / HEAD_DIM) + EPS) * og_ref[...]
    x1 = x_ref[...] + mod_ref[0, 2:3, :] * _dot(yn.astype(BF16), wout_ref[...])
    x1_ref[...] = x1

    ms = jnp.mean(x1 * x1, axis=-1, keepdims=True)
    h2 = (x1 * lax.rsqrt(ms + EPS)) * g2_ref[...]
    h2 = h2 * (1.0 + mod_ref[0, 4:5, :]) + mod_ref[0, 3:4, :]
    for j in range(ROW_TILES):
        h2t_ref[pl.ds(j, tm, stride=SUBLANES), :] = h2[:, j * LANES:(j + 1) * LANES]

    h_hi, h_lo = _split2(h2)
    both = _dot(h_hi, rw_ref[...])
    logits = both[:, :LANES] + both[:, LANES:] + _dot(h_lo, rw_ref[:, :LANES]) + rb_ref[...]
    lane = lax.broadcasted_iota(jnp.int32, (tm, LANES), 1)
    lane_f = lane.astype(F32)
    neg = jnp.float32(-jnp.inf)
    cur = jnp.where(lane < N_EXPERTS, logits, neg)
    tops, sels = [], []
    for _ in range(TOP_K):
        mk = jnp.max(cur, axis=-1, keepdims=True)
        ik = jnp.min(jnp.where(cur == mk, lane_f, float(LANES)), axis=-1, keepdims=True)
        sel = lane_f == ik
        cur = jnp.where(sel, neg, cur)
        tops.append(mk)
        sels.append((sel, ik.astype(jnp.int32)))
    es = [jnp.exp(t - tops[0]) for t in tops]
    inv = 1.0 / (es[0] + es[1] + es[2] + es[3])

    multi = jnp.zeros((tm, LANES), F32)
    for sel, _ in sels:
        multi = jnp.where(sel, 1.0, multi)
    before = _dot(ltri_ref[...], multi.astype(BF16)) + carry_ref[0:1, :]
    carry_ref[0:1, :] = before[tm - 1:tm, :] + multi[tm - 1:tm, :]

    sel_out = jnp.zeros((tm, LANES), jnp.int32)
    gate_out = jnp.zeros((tm, LANES), F32)
    for k, (sel, ik) in enumerate(sels):
        rank = jnp.sum(jnp.where(sel, before, 0.0), axis=-1, keepdims=True).astype(jnp.int32)
        sel_out = jnp.where(lane == k, ik, sel_out)
        sel_out = jnp.where(lane == TOP_K + k, rank, sel_out)
        gate_out = jnp.where(lane == k, es[k] * inv, gate_out)
    sel_ref[...] = sel_out
    gate_ref[...] = gate_out
    cnt_ref[...] = jnp.broadcast_to(carry_ref[0:1, :], cnt_ref.shape)


def _outproj(x2, cb, v, ysb, yfox, conv_w, og, w_out_b, mod3, g2, rw, rb, bd2, ltri, seq):
    t = x2.shape[0]
    tm = TM_PROJ
    tiles_per_seq = seq // tm
    row = lambda i: (i, 0)
    const = lambda i: (0, 0)
    return pl.pallas_call(
        functools.partial(_outproj_kernel, tiles_per_seq),
        grid=(t // tm,),
        in_specs=[pl.BlockSpec((tm, D_MODEL), row),
                  pl.BlockSpec((tm, CONV_CH), row),
                  pl.BlockSpec((tm, CONV_CH), row),
                  pl.BlockSpec((SUBLANES, CONV_CH), lambda i: (jnp.maximum(i * (tm // SUBLANES) - 1, 0), 0)),
                  pl.BlockSpec((tm, SB_DIM), row),
                  pl.BlockSpec((tm, FOX_DIM), row),
                  pl.BlockSpec((CONV_W, CONV_CH), const),
                  pl.BlockSpec((1, D_MIX), const),
                  pl.BlockSpec((D_MIX, D_MODEL), const),
                  pl.BlockSpec((1, N_MOD, D_MODEL), lambda i: (i // tiles_per_seq, 0, 0)),
                  pl.BlockSpec((1, D_MODEL), const),
                  pl.BlockSpec((D_MODEL, 2 * LANES), const),
                  pl.BlockSpec((1, LANES), const),
                  pl.BlockSpec((2 * LANES, 2 * LANES), const),
                  pl.BlockSpec((tm, tm), const)],
        out_specs=[pl.BlockSpec((tm, D_MODEL), row),
                   pl.BlockSpec((tm * SUBLANES, LANES), row),
                   pl.BlockSpec((tm, LANES), row),
                   pl.BlockSpec((tm, LANES), row),
                   pl.BlockSpec((SUBLANES, LANES), const)],
        out_shape=[jax.ShapeDtypeStruct((t, D_MODEL), F32),
                   jax.ShapeDtypeStruct((t * SUBLANES, LANES), F32),
                   jax.ShapeDtypeStruct((t, LANES), jnp.int32),
                   jax.ShapeDtypeStruct((t, LANES), F32),
                   jax.ShapeDtypeStruct((SUBLANES, LANES), F32)],
        scratch_shapes=[pltpu.VMEM((tm + SUBLANES, CONV_CH), F32), pltpu.VMEM((SUBLANES, LANES), F32)],
        compiler_params=_cparams(1),
        name="outproj_router",
    )(x2, cb, v, v, ysb, yfox, conv_w, og, w_out_b, mod3, g2, rw, rb, bd2, ltri)


def _row_copy(src, src_row, dst, dst_row, sem):
    return pltpu.make_async_copy(src.at[pl.ds(pl.multiple_of(src_row * SUBLANES, SUBLANES), SUBLANES), :],
                                 dst.at[pl.ds(pl.multiple_of(dst_row * SUBLANES, SUBLANES), SUBLANES), :], sem)


def _zero_fill(meta_ref, zero_ref, xg_ref, zsem):
    n_tiles = xg_ref.shape[0] // (TM_GROUP * SUBLANES)
    zero_ref[...] = jnp.zeros_like(zero_ref)

    def sweep(wait):
        def per_expert(e, c):
            pos = meta_ref[N_EXPERTS + e]
            n = meta_ref[2 * N_EXPERTS + e]
            bit = TM_GROUP // 2
            while bit:
                @pl.when((n & bit) != 0)
                def _(pos=pos, bit=bit):
                    cp = pltpu.make_async_copy(
                        zero_ref.at[pl.ds(0, bit * SUBLANES), :],
                        xg_ref.at[pl.ds(pl.multiple_of(pos * SUBLANES, SUBLANES), bit * SUBLANES), :], zsem)
                    cp.wait() if wait else cp.start()
                pos = pos + (n & bit)
                bit //= 2
            return c

        lax.fori_loop(0, N_EXPERTS, per_expert, 0)

        def per_tile(j, c):
            rows = TM_GROUP * SUBLANES
            cp = pltpu.make_async_copy(zero_ref, xg_ref.at[pl.ds(pl.multiple_of(j * rows, rows), rows), :], zsem)
            cp.wait() if wait else cp.start()
            return c

        lax.fori_loop(meta_ref[3 * N_EXPERTS], n_tiles, per_tile, 0)

    sweep(False)
    sweep(True)


def _dispatch_kernel(meta_ref, pos_ref, h_ref, xg_ref, zero_ref, sem, zsem):
    tm = h_ref.shape[0] // SUBLANES

    @pl.when(pl.program_id(0) == 0)
    def _():
        _zero_fill(meta_ref, zero_ref, xg_ref, zsem)

    def issue(r, c):
        for k in range(TOP_K):
            _row_copy(h_ref, r, xg_ref, pos_ref[r * TOP_K + k], sem).start()
        return c

    lax.fori_loop(0, tm, issue, 0)
    n = tm * TOP_K * SUBLANES
    pltpu.make_async_copy(xg_ref.at[pl.ds(0, n), :], xg_ref.at[pl.ds(0, n), :], sem).wait()


def _dispatch(meta, pos, h2t, n_rows):
    t = h2t.shape[0] // SUBLANES
    tm = TM_DISPATCH
    grid_spec = pltpu.PrefetchScalarGridSpec(
        num_scalar_prefetch=1,
        grid=(t // tm,),
        in_specs=[pl.BlockSpec((tm * TOP_K,), lambda i, *_: (i,), memory_space=pltpu.SMEM),
                  pl.BlockSpec((tm * SUBLANES, LANES), lambda i, *_: (i, 0))],
        out_specs=pl.BlockSpec(memory_space=pl.ANY),
        scratch_shapes=[pltpu.VMEM((TM_GROUP * SUBLANES, LANES), F32),
                        pltpu.SemaphoreType.DMA(()), pltpu.SemaphoreType.DMA(())],
    )
    return pl.pallas_call(
        _dispatch_kernel,
        grid_spec=grid_spec,
        out_shape=jax.ShapeDtypeStruct((n_rows * SUBLANES, LANES), F32),
        compiler_params=_cparams(1),
        name="moe_dispatch",
    )(meta, pos, h2t)


def _expert_kernel(te_ref, nv_ref, xg_ref, w1_ref, b1_ref, w2_ref, b2_ref, yg_ref, wg_ref, wl_ref, w2b_ref, wt_ref):
    tmg = xg_ref.shape[0] // SUBLANES
    j = pl.program_id(0)
    nv = nv_ref[j]

    @pl.when(jnp.logical_or(j == 0, te_ref[j] != te_ref[jnp.maximum(j - 1, 0)]))
    def _():
        ncol = wt_ref.shape[0] // ROW_TILES
        half = ncol // 2
        for n0 in range(0, 2 * D_EXPERT, ncol):
            wt = w1_ref[0, 0, :, n0:n0 + ncol].T
            for c in range(ROW_TILES):
                wt_ref[pl.ds(c, ncol, stride=ROW_TILES), :] = wt[:, c * LANES:(c + 1) * LANES]
            rows = slice(n0 // 2, n0 // 2 + half)
            for c in range(ROW_TILES):
                cols = slice(c * LANES, (c + 1) * LANES)
                wg_ref[rows, cols] = wt_ref[pl.ds(c, half, stride=2 * ROW_TILES), :].astype(BF16)
                wl_ref[rows, cols] = wt_ref[pl.ds(ROW_TILES + c, half, stride=2 * ROW_TILES), :].astype(BF16)
            w2b_ref[rows, :] = w2_ref[0, 0, rows, :].astype(BF16)

    @pl.when(nv == 0)
    def _():
        yg_ref[...] = jnp.zeros_like(yg_ref)

    @pl.when(nv > 0)
    def _():
        x = jnp.concatenate([xg_ref[pl.ds(c, tmg, stride=SUBLANES), :] for c in range(ROW_TILES)], axis=-1)
        x = x.astype(BF16)
        glu = jnp.minimum(_dot_nt(x, wg_ref[...]) + b1_ref[0, 0:1, :], SWIGLU_LIMIT)
        lin = jnp.clip(_dot_nt(x, wl_ref[...]) + b1_ref[0, 1:2, :], -SWIGLU_LIMIT, SWIGLU_LIMIT)
        a = glu * (1.0 / (1.0 + jnp.exp(-SWIGLU_ALPHA * glu))) * (lin + 1.0)
        y = _dot(a.astype(BF16), w2b_ref[...]) + b2_ref[0]
        for c in range(ROW_TILES):
            yg_ref[pl.ds(c, tmg, stride=SUBLANES), :] = y[:, c * LANES:(c + 1) * LANES]


def _experts(te, nv, xg, layer, w1, b1, w2, b2):
    tmg = TM_GROUP
    n_tiles = xg.shape[0] // (tmg * SUBLANES)
    grid_spec = pltpu.PrefetchScalarGridSpec(
        num_scalar_prefetch=2,
        grid=(n_tiles,),
        in_specs=[pl.BlockSpec((tmg * SUBLANES, LANES), lambda j, te, nv: (j, 0)),
                  pl.BlockSpec((1, 1, D_MODEL, 2 * D_EXPERT), lambda j, te, nv: (layer, te[j], 0, 0)),
                  pl.BlockSpec((1, 2, D_EXPERT), lambda j, te, nv: (te[j], 0, 0)),
                  pl.BlockSpec((1, 1, D_EXPERT, D_MODEL), lambda j, te, nv: (layer, te[j], 0, 0)),
                  pl.BlockSpec((1, 1, D_MODEL), lambda j, te, nv: (te[j], 0, 0))],
        out_specs=pl.BlockSpec((tmg * SUBLANES, LANES), lambda j, te, nv: (j, 0)),
        scratch_shapes=[pltpu.VMEM((D_EXPERT, D_MODEL), BF16), pltpu.VMEM((D_EXPERT, D_MODEL), BF16),
                        pltpu.VMEM((D_EXPERT, D_MODEL), BF16), pltpu.VMEM((2 * LANES * ROW_TILES, LANES), F32)],
    )
    return pl.pallas_call(
        _expert_kernel,
        grid_spec=grid_spec,
        out_shape=jax.ShapeDtypeStruct(xg.shape, F32),
        compiler_params=_cparams(1, EXPERT_VMEM_LIMIT),
        name="moe_experts",
    )(te, nv, xg, w1, b1, w2, b2)


def _combine_kernel(pos_ref, pos_next_ref, gate_ref, x1_ref, mod_ref, yg_ref, o_ref, buf_ref, sem):
    tm = x1_ref.shape[0]
    i = pl.program_id(0)
    slot = i % 2

    def gather(idx_ref, to_slot):
        def issue(r, c):
            for k in range(TOP_K):
                _row_copy(yg_ref, idx_ref[r * TOP_K + k], buf_ref.at[to_slot, k], r, sem.at[to_slot]).start()
            return c

        lax.fori_loop(0, tm, issue, 0)

    @pl.when(i == 0)
    def _():
        gather(pos_ref, 0)

    @pl.when(i + 1 < pl.num_programs(0))
    def _():
        gather(pos_next_ref, 1 - slot)

    for k in range(TOP_K):
        pltpu.make_async_copy(buf_ref.at[slot, k], buf_ref.at[slot, k], sem.at[slot]).wait()

    gates = gate_ref[...]
    gate_cols = [gates[:, k:k + 1] for k in range(TOP_K)]
    g2 = mod_ref[0, 5:6, :]
    for c in range(ROW_TILES):
        acc = jnp.zeros((tm, LANES), F32)
        for k in range(TOP_K):
            acc = acc + gate_cols[k] * buf_ref[slot, k, pl.ds(c, tm, stride=SUBLANES), :]
        cols = slice(c * LANES, (c + 1) * LANES)
        o_ref[:, cols] = x1_ref[:, cols] + g2[:, cols] * acc


def _combine(pos, gates, x1, mod3, yg, seq):
    t = x1.shape[0]
    tm = TM_COMBINE
    tiles_per_seq = seq // tm
    n = t // tm
    grid_spec = pltpu.PrefetchScalarGridSpec(
        num_scalar_prefetch=0,
        grid=(n,),
        in_specs=[pl.BlockSpec((tm * TOP_K,), lambda i: (i,), memory_space=pltpu.SMEM),
                  pl.BlockSpec((tm * TOP_K,), lambda i: (jnp.minimum(i + 1, n - 1),), memory_space=pltpu.SMEM),
                  pl.BlockSpec((tm, LANES), lambda i: (i, 0)),
                  pl.BlockSpec((tm, D_MODEL), lambda i: (i, 0)),
                  pl.BlockSpec((1, N_MOD, D_MODEL), lambda i: (i // tiles_per_seq, 0, 0)),
                  pl.BlockSpec(memory_space=pl.ANY)],
        out_specs=pl.BlockSpec((tm, D_MODEL), lambda i: (i, 0)),
        scratch_shapes=[pltpu.VMEM((2, TOP_K, tm * SUBLANES, LANES), F32), pltpu.SemaphoreType.DMA((2,))],
    )
    return pl.pallas_call(
        _combine_kernel,
        grid_spec=grid_spec,
        out_shape=jax.ShapeDtypeStruct((t, D_MODEL), F32),
        compiler_params=_cparams(1),
        name="moe_combine",
    )(pos, pos, gates, x1, mod3, yg)


def _block_diag_ones(n):
    i = np.arange(n)
    return jnp.asarray((i[:, None] // HEAD_DIM) == (i[None, :] // HEAD_DIM), dtype=BF16)


def _tri(n, strict, upper):
    i = np.arange(n)
    if upper:
        m = (i[:, None] > i[None, :]) if strict else (i[:, None] >= i[None, :])
        return jnp.asarray(m, dtype=BF16)
    m = (i[None, :] < i[:, None]) if strict else (i[None, :] <= i[:, None])
    return jnp.asarray(m, dtype=BF16)


def _tile2(g):
    return jnp.tile(g, SB_DIM // HEAD_DIM)


def _layer(x2, c, seq, layer, w_mlp1, w_mlp2, norm1_g, w_ada, b_ada, w_in, conv_w, sb_q_g, sb_k_g, fox_q_g, fox_k_g, fox_f_b,
           out_norm_g, w_out, norm2_g, router_w, router_b, b_mlp1, b_mlp2):
    t = x2.shape[0]
    nb = t // seq
    scale = HEAD_DIM ** -0.5

    w_in_p = jnp.pad(w_in, ((0, 0), (0, D_IN_PAD - w_in.shape[1]))).astype(BF16)
    fb = jnp.pad(fox_f_b, (0, LANES - FOX_HEADS)).reshape(1, LANES)
    qscale = scale * LOG2_E
    gains = jnp.stack([_tile2(sb_q_g) * qscale, _tile2(sb_k_g), _tile2(fox_q_g) * qscale, _tile2(fox_k_g)])
    gains = jnp.pad(gains, ((0, SUBLANES - 4), (0, 0)))
    rw = jnp.pad(router_w, ((0, 0), (0, LANES - N_EXPERTS)))
    rw_hi = rw.astype(BF16)
    rw = jnp.concatenate([rw_hi, (rw - rw_hi.astype(F32)).astype(BF16)], axis=1)
    rb = jnp.pad(router_b, (0, LANES - N_EXPERTS)).reshape(1, LANES)
    b1 = jnp.stack([b_mlp1[:, 0::2], b_mlp1[:, 1::2]], axis=1)
    b2 = b_mlp2.reshape(N_EXPERTS, 1, D_MODEL)

    mod3 = _modulation(c, w_ada, b_ada).reshape(nb, N_MOD, D_MODEL)

    cb, v, sbq, sbk, sbv, fq, fk, fv, cum = _inproj(
        x2, mod3, norm1_g.reshape(1, D_MODEL), w_in_p, _block_diag_ones(SB_DIM),
        _tri(TM_PROJ, strict=False, upper=False), fb, gains, seq)

    r3 = lambda a: a.reshape(nb, seq, SB_DIM)
    u = _tri(TB, strict=True, upper=True)
    zu = jnp.zeros_like(u)
    uu = jnp.concatenate([jnp.concatenate([u, zu], axis=1), jnp.concatenate([zu, u], axis=1)], axis=0)
    ysb = _sb_attention(r3(sbq), r3(sbk), r3(sbv), uu)

    n_kb = seq // TB
    f2 = cum[:, :FOX_HEADS, :].reshape(nb, FOX_DIM // LANES, 2, n_kb, TB).transpose(0, 1, 3, 2, 4)
    f2 = f2.reshape(nb, FOX_DIM // LANES, n_kb, 2 * TB)
    fend = cum[:, :FOX_HEADS, TB - 1::TB].reshape(nb * FOX_HEADS * n_kb)
    zbound = (2.0 * 1.02 * HEAD_DIM * qscale * jnp.max(jnp.abs(fox_q_g)) * jnp.max(jnp.abs(fox_k_g))).reshape(1)
    yfox = _fox_attention(r3(fq), r3(fk), r3(fv), f2, fend, zbound)

    x1, h2t, sel, gates, cnt = _outproj(
        x2, cb, v, ysb.reshape(t, SB_DIM), yfox.reshape(t, FOX_DIM), conv_w, out_norm_g.reshape(1, D_MIX),
        w_out.astype(BF16), mod3, norm2_g.reshape(1, D_MODEL), rw, rb, _block_diag_ones(2 * LANES),
        _tri(TM_PROJ, strict=True, upper=False), seq)

    counts = cnt[0, :N_EXPERTS].astype(jnp.int32)
    padded = ((counts + TM_GROUP - 1) // TM_GROUP) * TM_GROUP
    off_end = jnp.cumsum(padded)
    off = off_end - padded
    n_rows = t * TOP_K + N_EXPERTS * TM_GROUP
    n_tiles = n_rows // TM_GROUP
    start = jnp.arange(n_tiles, dtype=jnp.int32) * TM_GROUP
    te = jnp.minimum(jnp.sum(start[:, None] >= off_end[None, :], axis=1), N_EXPERTS - 1).astype(jnp.int32)
    nv = jnp.clip(off[te] + counts[te] - start, 0, TM_GROUP).astype(jnp.int32)
    pos = (off[sel[:, :TOP_K]] + sel[:, TOP_K:2 * TOP_K]).astype(jnp.int32).reshape(t * TOP_K)
    meta = jnp.concatenate([off, off + counts, padded - counts, off_end[-1:] // TM_GROUP]).astype(jnp.int32)

    xg = _dispatch(meta, pos, h2t, n_rows)
    yg = _experts(te, nv, xg, layer, w_mlp1, b1, w_mlp2, b2)
    return _combine(pos, gates, x1, mod3, yg, seq)


def kernel(x, c, norm1_g, w_ada, b_ada, w_in, conv_w, sb_q_g, sb_k_g, fox_q_g, fox_k_g, fox_f_b, out_norm_g, w_out,
           norm2_g, router_w, router_b, w_mlp1, b_mlp1, w_mlp2, b_mlp2):
    b, s, d = x.shape
    x2 = x.reshape(b * s, d)
    params = (norm1_g, w_ada, b_ada, w_in, conv_w, sb_q_g, sb_k_g, fox_q_g, fox_k_g, fox_f_b, out_norm_g, w_out,
              norm2_g, router_w, router_b, b_mlp1, b_mlp2)
    for layer in range(norm1_g.shape[0]):
        x2 = _layer(x2, c, s, layer, w_mlp1, w_mlp2, *(p[layer] for p in params))
    return x2.reshape(b, s, d)
```

```python
import functools

import numpy as np
import jax
import jax.numpy as jnp
from jax import lax
from jax.experimental import pallas as pl
from jax.experimental.pallas import tpu as pltpu

F32 = jnp.float32
BF16 = jnp.bfloat16

D_MODEL = 1024
HEAD_DIM = 64
CONV_CH = 256
CONV_W = 3
SB_DIM = 384
FOX_DIM = 384
FOX_HEADS = 6
N_PAIRS = 3
D_MIX = 1024
N_MOD = 6
N_EXPERTS = 32
TOP_K = 4
D_EXPERT = 1024
SWIGLU_ALPHA = 1.702
SWIGLU_LIMIT = 7.0
EPS = 1e-6

LANES = 128
SUBLANES = 8
ROW_TILES = D_MODEL // LANES
D_IN_PAD = 3 * CONV_CH + 3 * SB_DIM + 3 * FOX_DIM + LANES
COL_SBQ = 3 * CONV_CH
COL_SBK = COL_SBQ + SB_DIM
COL_SBV = COL_SBK + SB_DIM
COL_FQ = COL_SBV + SB_DIM
COL_FK = COL_FQ + FOX_DIM
COL_FV = COL_FK + FOX_DIM
COL_FLOG = COL_FV + FOX_DIM

TM_PROJ = 512
TQ = 256
TB = 128
TM_DISPATCH = 1024
TM_COMBINE = 256
TM_GROUP = 512
LOG2_E = 1.4426950408889634
LOG2_NEGLIGIBLE = -64.0
VMEM_LIMIT = 48 * 1024 * 1024
EXPERT_VMEM_LIMIT = 58 * 1024 * 1024


def _cparams(n_axes, vmem=VMEM_LIMIT):
    return pltpu.CompilerParams(dimension_semantics=("arbitrary",) * n_axes, vmem_limit_bytes=vmem)


def _dot(a, b):
    return jnp.dot(a, b, preferred_element_type=F32)


def _dot_nt(a, b):
    return lax.dot_general(a, b, (((1,), (1,)), ((), ())), preferred_element_type=F32)


def _split2(x):
    hi = x.astype(BF16)
    lo = (x - hi.astype(F32)).astype(BF16)
    return hi, lo


def _log_sigmoid(x):
    return jnp.minimum(x, 0.0) - jnp.log(1.0 + jnp.exp(-jnp.abs(x)))


def _mod_kernel(c_ref, w_ref, b_ref, o_ref):
    c = c_ref[...]
    s = c / (1.0 + jnp.exp(-c))
    o_ref[...] = _dot(s.astype(BF16), w_ref[...].astype(BF16)) + b_ref[...]


def _modulation(c, w_ada, b_ada):
    b, d = c.shape
    n = w_ada.shape[1]
    tn = 1024
    return pl.pallas_call(
        _mod_kernel,
        grid=(n // tn,),
        in_specs=[pl.BlockSpec((b, d), lambda j: (0, 0)),
                  pl.BlockSpec((d, tn), lambda j: (0, j)),
                  pl.BlockSpec((1, tn), lambda j: (0, j))],
        out_specs=pl.BlockSpec((b, tn), lambda j: (0, j)),
        out_shape=jax.ShapeDtypeStruct((b, n), F32),
        compiler_params=_cparams(1),
        name="modulation",
    )(c, w_ada, b_ada.reshape(1, n))


def _inproj_kernel(tiles_per_seq, x_ref, mod_ref, g1_ref, w_ref, bd_ref, tri_ref, fb_ref, gains_ref,
                   cb_ref, v_ref, sbq_ref, sbk_ref, sbv_ref, fq_ref, fk_ref, fv_ref, cum_ref, carry_ref):
    tm = x_ref.shape[0]
    x = x_ref[...]
    ms = jnp.mean(x * x, axis=-1, keepdims=True)
    h = (x * lax.rsqrt(ms + EPS)) * g1_ref[...]
    h = h * (1.0 + mod_ref[0, 1:2, :]) + mod_ref[0, 0:1, :]
    hb = h.astype(BF16)

    pc = _dot(hb, w_ref[:, 0:3 * CONV_CH])
    cb_ref[...] = pc[:, 0:CONV_CH]
    v_ref[...] = pc[:, CONV_CH:2 * CONV_CH] * pc[:, 2 * CONV_CH:3 * CONV_CH]

    def normed(col, row):
        p = _dot(hb, w_ref[:, col:col + SB_DIM])
        ss = _dot((p * p).astype(BF16), bd_ref[...])
        return (p * lax.rsqrt(ss * (1.0 / HEAD_DIM) + EPS) * gains_ref[row:row + 1, :]).astype(BF16)

    sbq_ref[...] = normed(COL_SBQ, 0)
    sbk_ref[...] = normed(COL_SBK, 1)
    sbv_ref[...] = _dot(hb, w_ref[:, COL_SBV:COL_SBV + SB_DIM]).astype(BF16)
    fq_ref[...] = normed(COL_FQ, 2)
    fk_ref[...] = normed(COL_FK, 3)
    fv_ref[...] = _dot(hb, w_ref[:, COL_FV:COL_FV + FOX_DIM]).astype(BF16)

    lf = _log_sigmoid(_dot(hb, w_ref[:, COL_FLOG:COL_FLOG + LANES]) + fb_ref[...]) * LOG2_E
    t0 = lf.astype(BF16)
    r1 = lf - t0.astype(F32)
    t1 = r1.astype(BF16)
    t2 = (r1 - t1.astype(F32)).astype(BF16)
    tri = tri_ref[...]
    cs = _dot(tri, t0) + _dot(tri, t1) + _dot(tri, t2)

    @pl.when(pl.program_id(0) % tiles_per_seq == 0)
    def _():
        carry_ref[...] = jnp.zeros_like(carry_ref)

    cum = cs + carry_ref[0:1, :]
    carry_ref[0:1, :] = cum[tm - 1:tm, :]
    cum_ref[0] = cum.T[0:SUBLANES, :]


def _inproj(x2, mod3, g1, w_in_p, bd, tri, fb, gains, seq):
    t = x2.shape[0]
    tm = TM_PROJ
    tiles_per_seq = seq // tm
    nb = t // seq
    row = lambda i: (i, 0)
    const = lambda i: (0, 0)
    outs = [jax.ShapeDtypeStruct((t, CONV_CH), F32), jax.ShapeDtypeStruct((t, CONV_CH), F32)]
    outs += [jax.ShapeDtypeStruct((t, SB_DIM), BF16)] * 6
    outs += [jax.ShapeDtypeStruct((nb, SUBLANES, seq), F32)]
    out_specs = [pl.BlockSpec((tm, CONV_CH), row)] * 2 + [pl.BlockSpec((tm, SB_DIM), row)] * 6
    out_specs += [pl.BlockSpec((1, SUBLANES, tm), lambda i: (i // tiles_per_seq, 0, i % tiles_per_seq))]
    return pl.pallas_call(
        functools.partial(_inproj_kernel, tiles_per_seq),
        grid=(t // tm,),
        in_specs=[pl.BlockSpec((tm, D_MODEL), row),
                  pl.BlockSpec((1, N_MOD, D_MODEL), lambda i: (i // tiles_per_seq, 0, 0)),
                  pl.BlockSpec((1, D_MODEL), const),
                  pl.BlockSpec((D_MODEL, D_IN_PAD), const),
                  pl.BlockSpec((SB_DIM, SB_DIM), const),
                  pl.BlockSpec((tm, tm), const),
                  pl.BlockSpec((1, LANES), const),
                  pl.BlockSpec((SUBLANES, SB_DIM), const)],
        out_specs=out_specs,
        out_shape=outs,
        scratch_shapes=[pltpu.VMEM((SUBLANES, LANES), F32)],
        compiler_params=_cparams(1),
        name="inproj",
    )(x2, mod3, g1, w_in_p, bd, tri, fb, gains)


def _stack_heads(k_ref, v_ref, kk_ref, vv_ref):
    n_blocks = kk_ref.shape[1]
    lane = lax.broadcasted_iota(jnp.int32, (TB, LANES), 1)
    first = lane < HEAD_DIM

    def body(j, c):
        rows = pl.ds(pl.multiple_of(j * TB, TB), TB)
        for src, dst in ((k_ref, kk_ref), (v_ref, vv_ref)):
            for p in range(N_PAIRS):
                blk = src[0, rows, p * LANES:(p + 1) * LANES]
                zero = jnp.zeros_like(blk)
                dst[p, j, 0:TB, :] = jnp.where(first, blk, zero)
                dst[p, j, TB:2 * TB, :] = jnp.where(first, zero, blk)
        return c

    lax.fori_loop(0, n_blocks, body, 0)


def _per_head(a0, a1):
    return jnp.concatenate([jnp.broadcast_to(a0, (TB, TB)), jnp.broadcast_to(a1, (TB, TB))], axis=1)


def _pair_iota():
    row = lax.broadcasted_iota(jnp.int32, (TB, 2 * TB), 0)
    col = lax.broadcasted_iota(jnp.int32, (TB, 2 * TB), 1) % TB
    return row, col


_CHAINS = [(p, s) for p in range(N_PAIRS) for s in range(2)]


def _q_block(q_ref, p, s):
    return q_ref[0, s * TB:(s + 1) * TB, p * LANES:(p + 1) * LANES]


def _sb_kernel(q_ref, k_ref, v_ref, uu_ref, o_ref, kk_ref, vv_ref):
    qi = pl.program_id(1)

    @pl.when(qi == 0)
    def _():
        _stack_heads(k_ref, v_ref, kk_ref, vv_ref)

    uu = uu_ref[...]
    row, col = _pair_iota()
    strict = col < row

    def block(p, s, kb, valid, state, diag):
        rs0, rs1, acc = state
        kb = jnp.maximum(kb, 0)
        z = _dot_nt(_q_block(q_ref, p, s), kk_ref[p, kb])
        log_beta = jnp.minimum(z, 0.0) - jnp.log2(1.0 + jnp.exp2(-jnp.abs(z)))
        log_1mb = log_beta - z
        keep = strict if diag else None
        if valid is not None:
            keep = valid if keep is None else jnp.logical_and(keep, valid)
        if keep is not None:
            log_1mb = jnp.where(keep, log_1mb, 0.0)
        hi, lo = _split2(log_1mb)
        rest = _dot(hi, uu) + _dot(lo, uu) + _per_head(rs0, rs1)
        w = jnp.exp2(log_beta + rest)
        if keep is not None:
            w = jnp.where(keep, w, 0.0)
        acc = acc + _dot(w.astype(BF16), vv_ref[p, kb])
        rs0 = rs0 + jnp.sum(log_1mb[:, :TB], axis=-1, keepdims=True)
        rs1 = rs1 + jnp.sum(log_1mb[:, TB:], axis=-1, keepdims=True)
        return rs0, rs1, acc

    def more(states):
        top = states[0][0]
        for st in states:
            top = jnp.maximum(top, jnp.maximum(st[0], st[1]))
        return (jnp.max(top) > LOG2_NEGLIGIBLE).astype(jnp.int32)

    zero = jnp.zeros((TB, 1), F32)
    init = (zero, zero, jnp.zeros((TB, LANES), F32))
    states = [block(p, s, 2 * qi + s, None, init, True) for p, s in _CHAINS]
    states = [block(p, s, 2 * qi + s - 1, (qi > 0) if s == 0 else None, st, False)
              for (p, s), st in zip(_CHAINS, states)]

    def cond(c):
        return jnp.logical_and(2 * qi + 1 - c[0] >= 0, c[2] > 0)

    def body(c):
        back, states, _ = c
        states = [block(p, s, 2 * qi + s - back, 2 * qi + s - back >= 0, st, False)
                  for (p, s), st in zip(_CHAINS, states)]
        return back + 1, states, more(states)

    _, states, _ = lax.while_loop(cond, body, (jnp.int32(2), states, more(states)))
    for (p, s), st in zip(_CHAINS, states):
        o_ref[0, s * TB:(s + 1) * TB, p * LANES:(p + 1) * LANES] = st[2]


def _sb_attention(q, k, v, uu):
    b, s, d = q.shape
    tq = TQ
    qspec = pl.BlockSpec((1, tq, d), lambda bi, qi: (bi, qi, 0))
    kvspec = pl.BlockSpec((1, s, d), lambda bi, qi: (bi, 0, 0))
    scratch = pltpu.VMEM((N_PAIRS, s // TB, 2 * TB, LANES), BF16)
    return pl.pallas_call(
        _sb_kernel,
        grid=(b, s // tq),
        in_specs=[qspec, kvspec, kvspec, pl.BlockSpec((2 * TB, 2 * TB), lambda bi, qi: (0, 0))],
        out_specs=qspec,
        out_shape=jax.ShapeDtypeStruct((b, s, d), F32),
        scratch_shapes=[scratch, scratch],
        compiler_params=_cparams(2),
        name="sb_attention",
    )(q, k, v, uu)


def _fox_kernel(n_kb, fend_ref, zb_ref, q_ref, k_ref, v_ref, f2_ref, o_ref, kk_ref, vv_ref, sc_ref):
    bi = pl.program_id(0)
    qi = pl.program_id(1)

    @pl.when(qi == 0)
    def _():
        _stack_heads(k_ref, v_ref, kk_ref, vv_ref)

    row, col = _pair_iota()
    causal = col <= row
    lane = lax.broadcasted_iota(jnp.int32, (TB, LANES), 1)
    first = lane < HEAD_DIM
    zbound = zb_ref[0]

    def first_block(head, blk):
        base = (bi * FOX_HEADS + head) * n_kb
        f_q = fend_ref[base + jnp.maximum(blk - 1, 0)]

        def keep(kb):
            gap = zbound + f_q - fend_ref[base + jnp.maximum(kb - 1, 0)]
            return jnp.logical_and(kb > 0, gap >= LOG2_NEGLIGIBLE)

        return lax.while_loop(keep, lambda kb: kb - 1, blk)

    def scores(p, s, back, steps):
        kb = 2 * qi + s - back
        sc = _dot_nt(_q_block(q_ref, p, s), kk_ref[p, jnp.maximum(kb, 0)]) - f2_ref[0, p, pl.ds(jnp.maximum(kb, 0), 1), :]
        return jnp.where(jnp.logical_and(kb >= 0, back <= steps), sc, -1e30)

    def head_max(a):
        return (jnp.max(a[:, :TB], axis=-1, keepdims=True), jnp.max(a[:, TB:], axis=-1, keepdims=True))

    def head_sum(a):
        return (jnp.sum(a[:, :TB], axis=-1, keepdims=True), jnp.sum(a[:, TB:], axis=-1, keepdims=True))

    for p in range(N_PAIRS):
        steps = jnp.int32(0)
        for s in range(2):
            for h in range(2):
                steps = jnp.maximum(steps, 2 * qi + s - first_block(2 * p + h, 2 * qi + s))
        n_it = (steps + 1) // 2

        runmax = []
        for s in range(2):
            sc = jnp.where(causal, scores(p, s, 0, steps), -1e30)
            sc_ref[s, 0] = sc
            runmax.append(sc)

        def pass1(it, runmax):
            out = list(runmax)
            for u in (1, 2):
                back = 2 * it + u
                for s in range(2):
                    sc = scores(p, s, back, steps)
                    sc_ref[s, back] = sc
                    out[s] = jnp.maximum(out[s], sc)
            return out

        runmax = lax.fori_loop(0, n_it, pass1, runmax)
        tops = [_per_head(*head_max(runmax[s])) for s in range(2)]

        def weigh(s, back, runsum, acc):
            pr = jnp.exp2(sc_ref[s, back] - tops[s])
            kb = jnp.maximum(2 * qi + s - back, 0)
            return runsum + pr, acc + _dot(pr.astype(BF16), vv_ref[p, kb])

        state = [weigh(s, 0, jnp.zeros((TB, 2 * TB), F32), jnp.zeros((TB, LANES), F32)) for s in range(2)]

        def pass2(it, state):
            out = list(state)
            for u in (1, 2):
                for s in range(2):
                    out[s] = weigh(s, 2 * it + u, *out[s])
            return out

        state = lax.fori_loop(0, n_it, pass2, state)
        for s in range(2):
            l0, l1 = head_sum(state[s][0])
            o_ref[0, s * TB:(s + 1) * TB, p * LANES:(p + 1) * LANES] = state[s][1] / jnp.where(first, l0, l1)


def _fox_attention(q, k, v, f2, fend, zbound):
    b, s, d = q.shape
    tq = TQ
    n_kb = s // TB
    qspec = pl.BlockSpec((1, tq, d), lambda bi, qi, *_: (bi, qi, 0))
    kvspec = pl.BlockSpec((1, s, d), lambda bi, qi, *_: (bi, 0, 0))
    scratch = pltpu.VMEM((N_PAIRS, n_kb, 2 * TB, LANES), BF16)
    grid_spec = pltpu.PrefetchScalarGridSpec(
        num_scalar_prefetch=2,
        grid=(b, s // tq),
        in_specs=[qspec, kvspec, kvspec,
                  pl.BlockSpec((1, N_PAIRS, n_kb, 2 * TB), lambda bi, qi, *_: (bi, 0, 0, 0))],
        out_specs=qspec,
        scratch_shapes=[scratch, scratch, pltpu.VMEM((2, n_kb + 2, TB, 2 * TB), F32)],
    )
    return pl.pallas_call(
        functools.partial(_fox_kernel, n_kb),
        grid_spec=grid_spec,
        out_shape=jax.ShapeDtypeStruct((b, s, d), F32),
        compiler_params=_cparams(2),
        name="fox_attention",
    )(fend, zbound, q, k, v, f2)


def _outproj_kernel(tiles_per_seq, x_ref, cb_ref, v_ref, vprev_ref, ysb_ref, yfox_ref, cw_ref, og_ref, wout_ref,
                    mod_ref, g2_ref, rw_ref, rb_ref, bd_ref, ltri_ref,
                    x1_ref, h2t_ref, sel_ref, gate_ref, cnt_ref, vext_ref, carry_ref):
    tm = x_ref.shape[0]
    i = pl.program_id(0)

    @pl.when(i == 0)
    def _():
        carry_ref[...] = jnp.zeros_like(carry_ref)

    first = (i % tiles_per_seq) == 0
    vext_ref[0:SUBLANES, :] = jnp.where(first, 0.0, vprev_ref[...])
    vext_ref[SUBLANES:SUBLANES + tm, :] = v_ref[...]
    conv = (cw_ref[0:1, :] * vext_ref[SUBLANES - 2:SUBLANES - 2 + tm, :]
            + cw_ref[1:2, :] * vext_ref[SUBLANES - 1:SUBLANES - 1 + tm, :]
            + cw_ref[2:3, :] * v_ref[...])
    y = jnp.concatenate([cb_ref[...] * conv, ysb_ref[...], yfox_ref[...]], axis=-1)

    ysq = (y * y).astype(BF16)
    bd = bd_ref[...]
    ss = jnp.concatenate([_dot(ysq[:, c:c + 2 * LANES], bd) for c in range(0, D_MIX, 2 * LANES)], axis=-1)
    yn = y * lax.rsqrt(ss * (1.0 / HEAD_DIM) + EPS) * og_ref[...]
    x1 = x_ref[...] + mod_ref[0, 2:3, :] * _dot(yn.astype(BF16), wout_ref[...])
    x1_ref[...] = x1

    ms = jnp.mean(x1 * x1, axis=-1, keepdims=True)
    h2 = (x1 * lax.rsqrt(ms + EPS)) * g2_ref[...]
    h2 = h2 * (1.0 + mod_ref[0, 4:5, :]) + mod_ref[0, 3:4, :]
    for j in range(ROW_TILES):
        h2t_ref[pl.ds(j, tm, stride=SUBLANES), :] = h2[:, j * LANES:(j + 1) * LANES]

    h_hi, h_lo = _split2(h2)
    both = _dot(h_hi, rw_ref[...])
    logits = both[:, :LANES] + both[:, LANES:] + _dot(h_lo, rw_ref[:, :LANES]) + rb_ref[...]
    lane = lax.broadcasted_iota(jnp.int32, (tm, LANES), 1)
    lane_f = lane.astype(F32)
    neg = jnp.float32(-jnp.inf)
    cur = jnp.where(lane < N_EXPERTS, logits, neg)
    tops, sels = [], []
    for _ in range(TOP_K):
        mk = jnp.max(cur, axis=-1, keepdims=True)
        ik = jnp.min(jnp.where(cur == mk, lane_f, float(LANES)), axis=-1, keepdims=True)
        sel = lane_f == ik
        cur = jnp.where(sel, neg, cur)
        tops.append(mk)
        sels.append((sel, ik.astype(jnp.int32)))
    es = [jnp.exp(t - tops[0]) for t in tops]
    inv = 1.0 / (es[0] + es[1] + es[2] + es[3])

    multi = jnp.zeros((tm, LANES), F32)
    for sel, _ in sels:
        multi = jnp.where(sel, 1.0, multi)
    before = _dot(ltri_ref[...], multi.astype(BF16)) + carry_ref[0:1, :]
    carry_ref[0:1, :] = before[tm - 1:tm, :] + multi[tm - 1:tm, :]

    sel_out = jnp.zeros((tm, LANES), jnp.int32)
    gate_out = jnp.zeros((tm, LANES), F32)
    for k, (sel, ik) in enumerate(sels):
        rank = jnp.sum(jnp.where(sel, before, 0.0), axis=-1, keepdims=True).astype(jnp.int32)
        sel_out = jnp.where(lane == k, ik, sel_out)
        sel_out = jnp.where(lane == TOP_K + k, rank, sel_out)
        gate_out = jnp.where(lane == k, es[k] * inv, gate_out)
    sel_ref[...] = sel_out
    gate_ref[...] = gate_out
    cnt_ref[...] = jnp.broadcast_to(carry_ref[0:1, :], cnt_ref.shape)


def _outproj(x2, cb, v, ysb, yfox, conv_w, og, w_out_b, mod3, g2, rw, rb, bd2, ltri, seq):
    t = x2.shape[0]
    tm = TM_PROJ
    tiles_per_seq = seq // tm
    row = lambda i: (i, 0)
    const = lambda i: (0, 0)
    return pl.pallas_call(
        functools.partial(_outproj_kernel, tiles_per_seq),
        grid=(t // tm,),
        in_specs=[pl.BlockSpec((tm, D_MODEL), row),
                  pl.BlockSpec((tm, CONV_CH), row),
                  pl.BlockSpec((tm, CONV_CH), row),
                  pl.BlockSpec((SUBLANES, CONV_CH), lambda i: (jnp.maximum(i * (tm // SUBLANES) - 1, 0), 0)),
                  pl.BlockSpec((tm, SB_DIM), row),
                  pl.BlockSpec((tm, FOX_DIM), row),
                  pl.BlockSpec((CONV_W, CONV_CH), const),
                  pl.BlockSpec((1, D_MIX), const),
                  pl.BlockSpec((D_MIX, D_MODEL), const),
                  pl.BlockSpec((1, N_MOD, D_MODEL), lambda i: (i // tiles_per_seq, 0, 0)),
                  pl.BlockSpec((1, D_MODEL), const),
                  pl.BlockSpec((D_MODEL, 2 * LANES), const),
                  pl.BlockSpec((1, LANES), const),
                  pl.BlockSpec((2 * LANES, 2 * LANES), const),
                  pl.BlockSpec((tm, tm), const)],
        out_specs=[pl.BlockSpec((tm, D_MODEL), row),
                   pl.BlockSpec((tm * SUBLANES, LANES), row),
                   pl.BlockSpec((tm, LANES), row),
                   pl.BlockSpec((tm, LANES), row),
                   pl.BlockSpec((SUBLANES, LANES), const)],
        out_shape=[jax.ShapeDtypeStruct((t, D_MODEL), F32),
                   jax.ShapeDtypeStruct((t * SUBLANES, LANES), F32),
                   jax.ShapeDtypeStruct((t, LANES), jnp.int32),
                   jax.ShapeDtypeStruct((t, LANES), F32),
                   jax.ShapeDtypeStruct((SUBLANES, LANES), F32)],
        scratch_shapes=[pltpu.VMEM((tm + SUBLANES, CONV_CH), F32), pltpu.VMEM((SUBLANES, LANES), F32)],
        compiler_params=_cparams(1),
        name="outproj_router",
    )(x2, cb, v, v, ysb, yfox, conv_w, og, w_out_b, mod3, g2, rw, rb, bd2, ltri)


def _row_copy(src, src_row, dst, dst_row, sem):
    return pltpu.make_async_copy(src.at[pl.ds(pl.multiple_of(src_row * SUBLANES, SUBLANES), SUBLANES), :],
                                 dst.at[pl.ds(pl.multiple_of(dst_row * SUBLANES, SUBLANES), SUBLANES), :], sem)


def _zero_fill(meta_ref, zero_ref, xg_ref, zsem):
    n_tiles = xg_ref.shape[0] // (TM_GROUP * SUBLANES)
    zero_ref[...] = jnp.zeros_like(zero_ref)

    def sweep(wait):
        def per_expert(e, c):
            pos = meta_ref[N_EXPERTS + e]
            n = meta_ref[2 * N_EXPERTS + e]
            bit = TM_GROUP // 2
            while bit:
                @pl.when((n & bit) != 0)
                def _(pos=pos, bit=bit):
                    cp = pltpu.make_async_copy(
                        zero_ref.at[pl.ds(0, bit * SUBLANES), :],
                        xg_ref.at[pl.ds(pl.multiple_of(pos * SUBLANES, SUBLANES), bit * SUBLANES), :], zsem)
                    cp.wait() if wait else cp.start()
                pos = pos + (n & bit)
                bit //= 2
            return c

        lax.fori_loop(0, N_EXPERTS, per_expert, 0)

        def per_tile(j, c):
            rows = TM_GROUP * SUBLANES
            cp = pltpu.make_async_copy(zero_ref, xg_ref.at[pl.ds(pl.multiple_of(j * rows, rows), rows), :], zsem)
            cp.wait() if wait else cp.start()
            return c

        lax.fori_loop(meta_ref[3 * N_EXPERTS], n_tiles, per_tile, 0)

    sweep(False)
    sweep(True)


def _dispatch_kernel(meta_ref, pos_ref, h_ref, xg_ref, zero_ref, sem, zsem):
    tm = h_ref.shape[0] // SUBLANES

    @pl.when(pl.program_id(0) == 0)
    def _():
        _zero_fill(meta_ref, zero_ref, xg_ref, zsem)

    def issue(r, c):
        for k in range(TOP_K):
            _row_copy(h_ref, r, xg_ref, pos_ref[r * TOP_K + k], sem).start()
        return c

    lax.fori_loop(0, tm, issue, 0)
    n = tm * TOP_K * SUBLANES
    pltpu.make_async_copy(xg_ref.at[pl.ds(0, n), :], xg_ref.at[pl.ds(0, n), :], sem).wait()


def _dispatch(meta, pos, h2t, n_rows):
    t = h2t.shape[0] // SUBLANES
    tm = TM_DISPATCH
    grid_spec = pltpu.PrefetchScalarGridSpec(
        num_scalar_prefetch=1,
        grid=(t // tm,),
        in_specs=[pl.BlockSpec((tm * TOP_K,), lambda i, *_: (i,), memory_space=pltpu.SMEM),
                  pl.BlockSpec((tm * SUBLANES, LANES), lambda i, *_: (i, 0))],
        out_specs=pl.BlockSpec(memory_space=pl.ANY),
        scratch_shapes=[pltpu.VMEM((TM_GROUP * SUBLANES, LANES), F32),
                        pltpu.SemaphoreType.DMA(()), pltpu.SemaphoreType.DMA(())],
    )
    return pl.pallas_call(
        _dispatch_kernel,
        grid_spec=grid_spec,
        out_shape=jax.ShapeDtypeStruct((n_rows * SUBLANES, LANES), F32),
        compiler_params=_cparams(1),
        name="moe_dispatch",
    )(meta, pos, h2t)


def _expert_kernel(te_ref, nv_ref, xg_ref, w1_ref, b1_ref, w2_ref, b2_ref, yg_ref, wg_ref, wl_ref, w2b_ref, wt_ref):
    tmg = xg_ref.shape[0] // SUBLANES
    j = pl.program_id(0)
    nv = nv_ref[j]

    @pl.when(jnp.logical_or(j == 0, te_ref[j] != te_ref[jnp.maximum(j - 1, 0)]))
    def _():
        ncol = wt_ref.shape[0] // ROW_TILES
        half = ncol // 2
        for n0 in range(0, 2 * D_EXPERT, ncol):
            wt = w1_ref[0, 0, :, n0:n0 + ncol].T
            for c in range(ROW_TILES):
                wt_ref[pl.ds(c, ncol, stride=ROW_TILES), :] = wt[:, c * LANES:(c + 1) * LANES]
            rows = slice(n0 // 2, n0 // 2 + half)
            for c in range(ROW_TILES):
                cols = slice(c * LANES, (c + 1) * LANES)
                wg_ref[rows, cols] = wt_ref[pl.ds(c, half, stride=2 * ROW_TILES), :].astype(BF16)
                wl_ref[rows, cols] = wt_ref[pl.ds(ROW_TILES + c, half, stride=2 * ROW_TILES), :].astype(BF16)
            w2b_ref[rows, :] = w2_ref[0, 0, rows, :].astype(BF16)

    @pl.when(nv == 0)
    def _():
        yg_ref[...] = jnp.zeros_like(yg_ref)

    @pl.when(nv > 0)
    def _():
        x = jnp.concatenate([xg_ref[pl.ds(c, tmg, stride=SUBLANES), :] for c in range(ROW_TILES)], axis=-1)
        x = x.astype(BF16)
        glu = jnp.minimum(_dot_nt(x, wg_ref[...]) + b1_ref[0, 0:1, :], SWIGLU_LIMIT)
        lin = jnp.clip(_dot_nt(x, wl_ref[...]) + b1_ref[0, 1:2, :], -SWIGLU_LIMIT, SWIGLU_LIMIT)
        a = glu * (1.0 / (1.0 + jnp.exp(-SWIGLU_ALPHA * glu))) * (lin + 1.0)
        y = _dot(a.astype(BF16), w2b_ref[...]) + b2_ref[0]
        for c in range(ROW_TILES):
            yg_ref[pl.ds(c, tmg, stride=SUBLANES), :] = y[:, c * LANES:(c + 1) * LANES]


def _experts(te, nv, xg, layer, w1, b1, w2, b2):
    tmg = TM_GROUP
    n_tiles = xg.shape[0] // (tmg * SUBLANES)
    grid_spec = pltpu.PrefetchScalarGridSpec(
        num_scalar_prefetch=2,
        grid=(n_tiles,),
        in_specs=[pl.BlockSpec((tmg * SUBLANES, LANES), lambda j, te, nv: (j, 0)),
                  pl.BlockSpec((1, 1, D_MODEL, 2 * D_EXPERT), lambda j, te, nv: (layer, te[j], 0, 0)),
                  pl.BlockSpec((1, 2, D_EXPERT), lambda j, te, nv: (te[j], 0, 0)),
                  pl.BlockSpec((1, 1, D_EXPERT, D_MODEL), lambda j, te, nv: (layer, te[j], 0, 0)),
                  pl.BlockSpec((1, 1, D_MODEL), lambda j, te, nv: (te[j], 0, 0))],
        out_specs=pl.BlockSpec((tmg * SUBLANES, LANES), lambda j, te, nv: (j, 0)),
        scratch_shapes=[pltpu.VMEM((D_EXPERT, D_MODEL), BF16), pltpu.VMEM((D_EXPERT, D_MODEL), BF16),
                        pltpu.VMEM((D_EXPERT, D_MODEL), BF16), pltpu.VMEM((2 * LANES * ROW_TILES, LANES), F32)],
    )
    return pl.pallas_call(
        _expert_kernel,
        grid_spec=grid_spec,
        out_shape=jax.ShapeDtypeStruct(xg.shape, F32),
        compiler_params=_cparams(1, EXPERT_VMEM_LIMIT),
        name="moe_experts",
    )(te, nv, xg, w1, b1, w2, b2)


def _combine_kernel(pos_ref, pos_next_ref, gate_ref, x1_ref, mod_ref, yg_ref, o_ref, buf_ref, sem):
    tm = x1_ref.shape[0]
    i = pl.program_id(0)
    slot = i % 2

    def gather(idx_ref, to_slot):
        def issue(r, c):
            for k in range(TOP_K):
                _row_copy(yg_ref, idx_ref[r * TOP_K + k], buf_ref.at[to_slot, k], r, sem.at[to_slot]).start()
            return c

        lax.fori_loop(0, tm, issue, 0)

    @pl.when(i == 0)
    def _():
        gather(pos_ref, 0)

    @pl.when(i + 1 < pl.num_programs(0))
    def _():
        gather(pos_next_ref, 1 - slot)

    for k in range(TOP_K):
        pltpu.make_async_copy(buf_ref.at[slot, k], buf_ref.at[slot, k], sem.at[slot]).wait()

    gates = gate_ref[...]
    gate_cols = [gates[:, k:k + 1] for k in range(TOP_K)]
    g2 = mod_ref[0, 5:6, :]
    for c in range(ROW_TILES):
        acc = jnp.zeros((tm, LANES), F32)
        for k in range(TOP_K):
            acc = acc + gate_cols[k] * buf_ref[slot, k, pl.ds(c, tm, stride=SUBLANES), :]
        cols = slice(c * LANES, (c + 1) * LANES)
        o_ref[:, cols] = x1_ref[:, cols] + g2[:, cols] * acc


def _combine(pos, gates, x1, mod3, yg, seq):
    t = x1.shape[0]
    tm = TM_COMBINE
    tiles_per_seq = seq // tm
    n = t // tm
    grid_spec = pltpu.PrefetchScalarGridSpec(
        num_scalar_prefetch=0,
        grid=(n,),
        in_specs=[pl.BlockSpec((tm * TOP_K,), lambda i: (i,), memory_space=pltpu.SMEM),
                  pl.BlockSpec((tm * TOP_K,), lambda i: (jnp.minimum(i + 1, n - 1),), memory_space=pltpu.SMEM),
                  pl.BlockSpec((tm, LANES), lambda i: (i, 0)),
                  pl.BlockSpec((tm, D_MODEL), lambda i: (i, 0)),
                  pl.BlockSpec((1, N_MOD, D_MODEL), lambda i: (i // tiles_per_seq, 0, 0)),
                  pl.BlockSpec(memory_space=pl.ANY)],
        out_specs=pl.BlockSpec((tm, D_MODEL), lambda i: (i, 0)),
        scratch_shapes=[pltpu.VMEM((2, TOP_K, tm * SUBLANES, LANES), F32), pltpu.SemaphoreType.DMA((2,))],
    )
    return pl.pallas_call(
        _combine_kernel,
        grid_spec=grid_spec,
        out_shape=jax.ShapeDtypeStruct((t, D_MODEL), F32),
        compiler_params=_cparams(1),
        name="moe_combine",
    )(pos, pos, gates, x1, mod3, yg)


def _block_diag_ones(n):
    i = np.arange(n)
    return jnp.asarray((i[:, None] // HEAD_DIM) == (i[None, :] // HEAD_DIM), dtype=BF16)


def _tri(n, strict, upper):
    i = np.arange(n)
    if upper:
        m = (i[:, None] > i[None, :]) if strict else (i[:, None] >= i[None, :])
        return jnp.asarray(m, dtype=BF16)
    m = (i[None, :] < i[:, None]) if strict else (i[None, :] <= i[:, None])
    return jnp.asarray(m, dtype=BF16)


def _tile2(g):
    return jnp.tile(g, SB_DIM // HEAD_DIM)


def _layer(x2, c, seq, layer, w_mlp1, w_mlp2, norm1_g, w_ada, b_ada, w_in, conv_w, sb_q_g, sb_k_g, fox_q_g, fox_k_g, fox_f_b,
           out_norm_g, w_out, norm2_g, router_w, router_b, b_mlp1, b_mlp2):
    t = x2.shape[0]
    nb = t // seq
    scale = HEAD_DIM ** -0.5

    w_in_p = jnp.pad(w_in, ((0, 0), (0, D_IN_PAD - w_in.shape[1]))).astype(BF16)
    fb = jnp.pad(fox_f_b, (0, LANES - FOX_HEADS)).reshape(1, LANES)
    qscale = scale * LOG2_E
    gains = jnp.stack([_tile2(sb_q_g) * qscale, _tile2(sb_k_g), _tile2(fox_q_g) * qscale, _tile2(fox_k_g)])
    gains = jnp.pad(gains, ((0, SUBLANES - 4), (0, 0)))
    rw = jnp.pad(router_w, ((0, 0), (0, LANES - N_EXPERTS)))
    rw_hi = rw.astype(BF16)
    rw = jnp.concatenate([rw_hi, (rw - rw_hi.astype(F32)).astype(BF16)], axis=1)
    rb = jnp.pad(router_b, (0, LANES - N_EXPERTS)).reshape(1, LANES)
    b1 = jnp.stack([b_mlp1[:, 0::2], b_mlp1[:, 1::2]], axis=1)
    b2 = b_mlp2.reshape(N_EXPERTS, 1, D_MODEL)

    mod3 = _modulation(c, w_ada, b_ada).reshape(nb, N_MOD, D_MODEL)

    cb, v, sbq, sbk, sbv, fq, fk, fv, cum = _inproj(
        x2, mod3, norm1_g.reshape(1, D_MODEL), w_in_p, _block_diag_ones(SB_DIM),
        _tri(TM_PROJ, strict=False, upper=False), fb, gains, seq)

    r3 = lambda a: a.reshape(nb, seq, SB_DIM)
    u = _tri(TB, strict=True, upper=True)
    zu = jnp.zeros_like(u)
    uu = jnp.concatenate([jnp.concatenate([u, zu], axis=1), jnp.concatenate([zu, u], axis=1)], axis=0)
    ysb = _sb_attention(r3(sbq), r3(sbk), r3(sbv), uu)

    n_kb = seq // TB
    f2 = cum[:, :FOX_HEADS, :].reshape(nb, FOX_DIM // LANES, 2, n_kb, TB).transpose(0, 1, 3, 2, 4)
    f2 = f2.reshape(nb, FOX_DIM // LANES, n_kb, 2 * TB)
    fend = cum[:, :FOX_HEADS, TB - 1::TB].reshape(nb * FOX_HEADS * n_kb)
    zbound = (2.0 * 1.02 * HEAD_DIM * qscale * jnp.max(jnp.abs(fox_q_g)) * jnp.max(jnp.abs(fox_k_g))).reshape(1)
    yfox = _fox_attention(r3(fq), r3(fk), r3(fv), f2, fend, zbound)

    x1, h2t, sel, gates, cnt = _outproj(
        x2, cb, v, ysb.reshape(t, SB_DIM), yfox.reshape(t, FOX_DIM), conv_w, out_norm_g.reshape(1, D_MIX),
        w_out.astype(BF16), mod3, norm2_g.reshape(1, D_MODEL), rw, rb, _block_diag_ones(2 * LANES),
        _tri(TM_PROJ, strict=True, upper=False), seq)

    counts = cnt[0, :N_EXPERTS].astype(jnp.int32)
    padded = ((counts + TM_GROUP - 1) // TM_GROUP) * TM_GROUP
    off_end = jnp.cumsum(padded)
    off = off_end - padded
    n_rows = t * TOP_K + N_EXPERTS * TM_GROUP
    n_tiles = n_rows // TM_GROUP
    start = jnp.arange(n_tiles, dtype=jnp.int32) * TM_GROUP
    te = jnp.minimum(jnp.sum(start[:, None] >= off_end[None, :], axis=1), N_EXPERTS - 1).astype(jnp.int32)
    nv = jnp.clip(off[te] + counts[te] - start, 0, TM_GROUP).astype(jnp.int32)
    pos = (off[sel[:, :TOP_K]] + sel[:, TOP_K:2 * TOP_K]).astype(jnp.int32).reshape(t * TOP_K)
    meta = jnp.concatenate([off, off + counts, padded - counts, off_end[-1:] // TM_GROUP]).astype(jnp.int32)

    xg = _dispatch(meta, pos, h2t, n_rows)
    yg = _experts(te, nv, xg, layer, w_mlp1, b1, w_mlp2, b2)
    return _combine(pos, gates, x1, mod3, yg, seq)


def kernel(x, c, norm1_g, w_ada, b_ada, w_in, conv_w, sb_q_g, sb_k_g, fox_q_g, fox_k_g, fox_f_b, out_norm_g, w_out,
           norm2_g, router_w, router_b, w_mlp1, b_mlp1, w_mlp2, b_mlp2):
    b, s, d = x.shape
    x2 = x.reshape(b * s, d)
    params = (norm1_g, w_ada, b_ada, w_in, conv_w, sb_q_g, sb_k_g, fox_q_g, fox_k_g, fox_f_b, out_norm_g, w_out,
              norm2_g, router_w, router_b, b_mlp1, b_mlp2)
    for layer in range(norm1_g.shape[0]):
        x2 = _layer(x2, c, s, layer, w_mlp1, w_mlp2, *(p[layer] for p in params))
    return x2.reshape(b, s, d)
```

```python
import functools

import numpy as np
import jax
import jax.numpy as jnp
from jax import lax
from jax.experimental import pallas as pl
from jax.experimental.pallas import tpu as pltpu

F32 = jnp.float32
BF16 = jnp.bfloat16

D_MODEL = 1024
HEAD_DIM = 64
CONV_CH = 256
CONV_W = 3
SB_DIM = 384
FOX_DIM = 384
FOX_HEADS = 6
N_PAIRS = 3
D_MIX = 1024
N_MOD = 6
N_EXPERTS = 32
TOP_K = 4
D_EXPERT = 1024
SWIGLU_ALPHA = 1.702
SWIGLU_LIMIT = 7.0
EPS = 1e-6

LANES = 128
SUBLANES = 8
ROW_TILES = D_MODEL // LANES
D_IN_PAD = 3 * CONV_CH + 3 * SB_DIM + 3 * FOX_DIM + LANES
COL_SBQ = 3 * CONV_CH
COL_SBK = COL_SBQ + SB_DIM
COL_SBV = COL_SBK + SB_DIM
COL_FQ = COL_SBV + SB_DIM
COL_FK = COL_FQ + FOX_DIM
COL_FV = COL_FK + FOX_DIM
COL_FLOG = COL_FV + FOX_DIM

TM_PROJ = 512
TQ = 256
TB = 128
TM_DISPATCH = 1024
TM_COMBINE = 256
TM_GROUP = 512
LOG2_E = 1.4426950408889634
LOG2_NEGLIGIBLE = -40.0
VMEM_LIMIT = 48 * 1024 * 1024
INPROJ_VMEM_LIMIT = 56 * 1024 * 1024
EXPERT_VMEM_LIMIT = 58 * 1024 * 1024


def _cparams(n_axes, vmem=VMEM_LIMIT):
    return pltpu.CompilerParams(dimension_semantics=("arbitrary",) * n_axes, vmem_limit_bytes=vmem)


def _dot(a, b):
    return jnp.dot(a, b, preferred_element_type=F32)


def _dot_nt(a, b):
    return lax.dot_general(a, b, (((1,), (1,)), ((), ())), preferred_element_type=F32)


def _split2(x):
    hi = x.astype(BF16)
    lo = (x - hi.astype(F32)).astype(BF16)
    return hi, lo


def _row_copy(src, src_row, dst, dst_row, sem):
    return pltpu.make_async_copy(src.at[pl.ds(pl.multiple_of(src_row * SUBLANES, SUBLANES), SUBLANES), :],
                                 dst.at[pl.ds(pl.multiple_of(dst_row * SUBLANES, SUBLANES), SUBLANES), :], sem)


def _log_sigmoid(x):
    return jnp.minimum(x, 0.0) - jnp.log(1.0 + jnp.exp(-jnp.abs(x)))


def _mod_kernel(c_ref, w_ref, b_ref, o_ref):
    c = c_ref[...]
    s = c / (1.0 + jnp.exp(-c))
    o_ref[...] = _dot(s.astype(BF16), w_ref[...].astype(BF16)) + b_ref[...]


def _modulation(c, w_ada, b_ada):
    b, d = c.shape
    n = w_ada.shape[1]
    tn = 1024
    return pl.pallas_call(
        _mod_kernel,
        grid=(n // tn,),
        in_specs=[pl.BlockSpec((b, d), lambda j: (0, 0)),
                  pl.BlockSpec((d, tn), lambda j: (0, j)),
                  pl.BlockSpec((1, tn), lambda j: (0, j))],
        out_specs=pl.BlockSpec((b, tn), lambda j: (0, j)),
        out_shape=jax.ShapeDtypeStruct((b, n), F32),
        compiler_params=_cparams(1),
        name="modulation",
    )(c, w_ada, b_ada.reshape(1, n))


def _gather_rows(yg_ref, idx_ref, buf_ref, sem, slot, n_rows):
    def issue(r, c):
        for k in range(TOP_K):
            _row_copy(yg_ref, idx_ref[r * TOP_K + k], buf_ref.at[slot, k], r, sem.at[slot]).start()
        return c

    lax.fori_loop(0, n_rows, issue, 0)


def _combine_rows(gate_ref, buf_ref, slot, n_rows, x1_ref, g2, out_ref):
    gates = gate_ref[...]
    gate_cols = [gates[:, k:k + 1] for k in range(TOP_K)]
    for c in range(ROW_TILES):
        acc = jnp.zeros((n_rows, LANES), F32)
        for k in range(TOP_K):
            acc = acc + gate_cols[k] * buf_ref[slot, k, pl.ds(c, n_rows, stride=SUBLANES), :]
        cols = slice(c * LANES, (c + 1) * LANES)
        out_ref[:, cols] = x1_ref[:, cols] + g2[:, cols] * acc


def _wait_rows(buf_ref, sem, slot):
    for k in range(TOP_K):
        pltpu.make_async_copy(buf_ref.at[slot, k], buf_ref.at[slot, k], sem.at[slot]).wait()


def _inproj_kernel(tiles_per_seq, fused, *refs):
    if fused:
        (pos_ref, pos_next_ref, gate_ref, x1_ref, modp_ref, yg_ref, mod_ref, g1_ref, w_ref, bd_ref, tri_ref, fb_ref,
         gains_ref, x_out_ref, cb_ref, v_ref, sbq_ref, sbk_ref, sbv_ref, fq_ref, fk_ref, fv_ref, cum_ref,
         carry_ref, x_ref, buf_ref, sem) = refs
        tm = x1_ref.shape[0]
        i = pl.program_id(0)
        slot = i % 2

        @pl.when(i == 0)
        def _():
            _gather_rows(yg_ref, pos_ref, buf_ref, sem, 0, tm)

        @pl.when(i + 1 < pl.num_programs(0))
        def _():
            _gather_rows(yg_ref, pos_next_ref, buf_ref, sem, 1 - slot, tm)

        _wait_rows(buf_ref, sem, slot)
        _combine_rows(gate_ref, buf_ref, slot, tm, x1_ref, modp_ref[0, 5:6, :], x_ref)
        x_out_ref[...] = x_ref[...]
    else:
        (x_ref, mod_ref, g1_ref, w_ref, bd_ref, tri_ref, fb_ref, gains_ref,
         cb_ref, v_ref, sbq_ref, sbk_ref, sbv_ref, fq_ref, fk_ref, fv_ref, cum_ref, carry_ref) = refs
        tm = x_ref.shape[0]
    x = x_ref[...]
    ms = jnp.mean(x * x, axis=-1, keepdims=True)
    h = (x * lax.rsqrt(ms + EPS)) * g1_ref[...]
    h = h * (1.0 + mod_ref[0, 1:2, :]) + mod_ref[0, 0:1, :]
    hb = h.astype(BF16)

    pc = _dot(hb, w_ref[:, 0:3 * CONV_CH])
    cb_ref[...] = pc[:, 0:CONV_CH]
    v_ref[...] = pc[:, CONV_CH:2 * CONV_CH] * pc[:, 2 * CONV_CH:3 * CONV_CH]

    def normed(col, row):
        p = _dot(hb, w_ref[:, col:col + SB_DIM])
        ss = _dot((p * p).astype(BF16), bd_ref[...])
        return (p * lax.rsqrt(ss * (1.0 / HEAD_DIM) + EPS) * gains_ref[row:row + 1, :]).astype(BF16)

    sbq_ref[...] = normed(COL_SBQ, 0)
    sbk_ref[...] = normed(COL_SBK, 1)
    sbv_ref[...] = _dot(hb, w_ref[:, COL_SBV:COL_SBV + SB_DIM]).astype(BF16)
    fq_ref[...] = normed(COL_FQ, 2)
    fk_ref[...] = normed(COL_FK, 3)
    fv_ref[...] = _dot(hb, w_ref[:, COL_FV:COL_FV + FOX_DIM]).astype(BF16)

    lf = _log_sigmoid(_dot(hb, w_ref[:, COL_FLOG:COL_FLOG + LANES]) + fb_ref[...]) * LOG2_E
    t0 = lf.astype(BF16)
    r1 = lf - t0.astype(F32)
    t1 = r1.astype(BF16)
    t2 = (r1 - t1.astype(F32)).astype(BF16)
    tri = tri_ref[...]
    cs = _dot(tri, t0) + _dot(tri, t1) + _dot(tri, t2)

    @pl.when(pl.program_id(0) % tiles_per_seq == 0)
    def _():
        carry_ref[...] = jnp.zeros_like(carry_ref)

    cum = cs + carry_ref[0:1, :]
    carry_ref[0:1, :] = cum[tm - 1:tm, :]
    cum_ref[0] = cum.T[0:SUBLANES, :]


def _inproj(x_parts, mod3, g1, w_in_p, bd, tri, fb, gains, seq):
    fused = len(x_parts) > 1
    t = x_parts[2].shape[0] if fused else x_parts[0].shape[0]
    tm = TM_PROJ
    n = t // tm
    tiles_per_seq = seq // tm
    nb = t // seq
    row = lambda i: (i, 0)
    const = lambda i: (0, 0)
    per_seq = lambda i: (i // tiles_per_seq, 0, 0)
    outs = [jax.ShapeDtypeStruct((t, CONV_CH), F32), jax.ShapeDtypeStruct((t, CONV_CH), F32)]
    outs += [jax.ShapeDtypeStruct((t, SB_DIM), BF16)] * 6
    outs += [jax.ShapeDtypeStruct((nb, SUBLANES, seq), F32)]
    out_specs = [pl.BlockSpec((tm, CONV_CH), row)] * 2 + [pl.BlockSpec((tm, SB_DIM), row)] * 6
    out_specs += [pl.BlockSpec((1, SUBLANES, tm), lambda i: (i // tiles_per_seq, 0, i % tiles_per_seq))]
    in_specs = [pl.BlockSpec((1, N_MOD, D_MODEL), per_seq),
                pl.BlockSpec((1, D_MODEL), const),
                pl.BlockSpec((D_MODEL, D_IN_PAD), const),
                pl.BlockSpec((SB_DIM, SB_DIM), const),
                pl.BlockSpec((tm, tm), const),
                pl.BlockSpec((1, LANES), const),
                pl.BlockSpec((SUBLANES, SB_DIM), const)]
    scratch = [pltpu.VMEM((SUBLANES, LANES), F32)]
    if fused:
        pos, gates, x1, mod3_prev, yg = x_parts
        operands = (pos, pos, gates, x1, mod3_prev, yg)
        in_specs = [pl.BlockSpec((tm * TOP_K,), lambda i: (i,), memory_space=pltpu.SMEM),
                    pl.BlockSpec((tm * TOP_K,), lambda i: (jnp.minimum(i + 1, n - 1),), memory_space=pltpu.SMEM),
                    pl.BlockSpec((tm, LANES), row),
                    pl.BlockSpec((tm, D_MODEL), row),
                    pl.BlockSpec((1, N_MOD, D_MODEL), per_seq),
                    pl.BlockSpec(memory_space=pl.ANY)] + in_specs
        outs = [jax.ShapeDtypeStruct((t, D_MODEL), F32)] + outs
        out_specs = [pl.BlockSpec((tm, D_MODEL), row)] + out_specs
        scratch += [pltpu.VMEM((tm, D_MODEL), F32), pltpu.VMEM((2, TOP_K, tm * SUBLANES, LANES), F32),
                    pltpu.SemaphoreType.DMA((2,))]
    else:
        operands = x_parts
        in_specs = [pl.BlockSpec((tm, D_MODEL), row)] + in_specs
    return pl.pallas_call(
        functools.partial(_inproj_kernel, tiles_per_seq, fused),
        grid=(n,),
        in_specs=in_specs,
        out_specs=out_specs,
        out_shape=outs,
        scratch_shapes=scratch,
        compiler_params=_cparams(1, INPROJ_VMEM_LIMIT),
        name="inproj",
    )(*operands, mod3, g1, w_in_p, bd, tri, fb, gains)


def _stack_heads(k_ref, v_ref, kk_ref, vv_ref):
    n_blocks = kk_ref.shape[1]
    lane = lax.broadcasted_iota(jnp.int32, (TB, LANES), 1)
    first = lane < HEAD_DIM

    def body(j, c):
        rows = pl.ds(pl.multiple_of(j * TB, TB), TB)
        for src, dst in ((k_ref, kk_ref), (v_ref, vv_ref)):
            for p in range(N_PAIRS):
                blk = src[0, rows, p * LANES:(p + 1) * LANES]
                zero = jnp.zeros_like(blk)
                dst[p, j, 0:TB, :] = jnp.where(first, blk, zero)
                dst[p, j, TB:2 * TB, :] = jnp.where(first, zero, blk)
        return c

    lax.fori_loop(0, n_blocks, body, 0)


def _per_head(a0, a1):
    return jnp.concatenate([jnp.broadcast_to(a0, (TB, TB)), jnp.broadcast_to(a1, (TB, TB))], axis=1)


def _pair_iota():
    row = lax.broadcasted_iota(jnp.int32, (TB, 2 * TB), 0)
    col = lax.broadcasted_iota(jnp.int32, (TB, 2 * TB), 1) % TB
    return row, col


_CHAINS = [(p, s) for p in range(N_PAIRS) for s in range(2)]


def _q_block(q_ref, p, s):
    return q_ref[0, s * TB:(s + 1) * TB, p * LANES:(p + 1) * LANES]


def _sb_kernel(q_ref, k_ref, v_ref, uu_ref, o_ref, kk_ref, vv_ref):
    qi = pl.program_id(1)

    @pl.when(qi == 0)
    def _():
        _stack_heads(k_ref, v_ref, kk_ref, vv_ref)

    uu = uu_ref[...]
    row, col = _pair_iota()
    strict = col < row

    def block(p, s, kb, valid, state, diag):
        rs0, rs1, acc = state
        kb = jnp.maximum(kb, 0)
        z = _dot_nt(_q_block(q_ref, p, s), kk_ref[p, kb])
        log_beta = jnp.minimum(z, 0.0) - jnp.log2(1.0 + jnp.exp2(-jnp.abs(z)))
        log_1mb = log_beta - z
        keep = strict if diag else None
        if valid is not None:
            keep = valid if keep is None else jnp.logical_and(keep, valid)
        if keep is not None:
            log_1mb = jnp.where(keep, log_1mb, 0.0)
        hi, lo = _split2(log_1mb)
        rest = _dot(hi, uu) + _dot(lo, uu) + _per_head(rs0, rs1)
        w = jnp.exp2(log_beta + rest)
        if keep is not None:
            w = jnp.where(keep, w, 0.0)
        acc = acc + _dot(w.astype(BF16), vv_ref[p, kb])
        rs0 = rs0 + jnp.sum(log_1mb[:, :TB], axis=-1, keepdims=True)
        rs1 = rs1 + jnp.sum(log_1mb[:, TB:], axis=-1, keepdims=True)
        return rs0, rs1, acc

    def more(states):
        top = states[0][0]
        for st in states:
            top = jnp.maximum(top, jnp.maximum(st[0], st[1]))
        return (jnp.max(top) > LOG2_NEGLIGIBLE).astype(jnp.int32)

    zero = jnp.zeros((TB, 1), F32)
    init = (zero, zero, jnp.zeros((TB, LANES), F32))
    states = [block(p, s, 2 * qi + s, None, init, True) for p, s in _CHAINS]
    states = [block(p, s, 2 * qi + s - 1, (qi > 0) if s == 0 else None, st, False)
              for (p, s), st in zip(_CHAINS, states)]

    def cond(c):
        return jnp.logical_and(2 * qi + 1 - c[0] >= 0, c[2] > 0)

    def body(c):
        back, states, _ = c
        states = [block(p, s, 2 * qi + s - back, 2 * qi + s - back >= 0, st, False)
                  for (p, s), st in zip(_CHAINS, states)]
        return back + 1, states, more(states)

    _, states, _ = lax.while_loop(cond, body, (jnp.int32(2), states, more(states)))
    for (p, s), st in zip(_CHAINS, states):
        o_ref[0, s * TB:(s + 1) * TB, p * LANES:(p + 1) * LANES] = st[2]


def _sb_attention(q, k, v, uu):
    b, s, d = q.shape
    tq = TQ
    qspec = pl.BlockSpec((1, tq, d), lambda bi, qi: (bi, qi, 0))
    kvspec = pl.BlockSpec((1, s, d), lambda bi, qi: (bi, 0, 0))
    scratch = pltpu.VMEM((N_PAIRS, s // TB, 2 * TB, LANES), BF16)
    return pl.pallas_call(
        _sb_kernel,
        grid=(b, s // tq),
        in_specs=[qspec, kvspec, kvspec, pl.BlockSpec((2 * TB, 2 * TB), lambda bi, qi: (0, 0))],
        out_specs=qspec,
        out_shape=jax.ShapeDtypeStruct((b, s, d), F32),
        scratch_shapes=[scratch, scratch],
        compiler_params=_cparams(2),
        name="sb_attention",
    )(q, k, v, uu)


def _fox_kernel(n_kb, fend_ref, zb_ref, q_ref, k_ref, v_ref, f2_ref, o_ref, kk_ref, vv_ref, sc_ref):
    bi = pl.program_id(0)
    qi = pl.program_id(1)

    @pl.when(qi == 0)
    def _():
        _stack_heads(k_ref, v_ref, kk_ref, vv_ref)

    row, col = _pair_iota()
    causal = col <= row
    lane = lax.broadcasted_iota(jnp.int32, (TB, LANES), 1)
    first = lane < HEAD_DIM
    zbound = zb_ref[0]

    def first_block(head, blk):
        base = (bi * FOX_HEADS + head) * n_kb
        f_q = fend_ref[base + jnp.maximum(blk - 1, 0)]

        def keep(kb):
            gap = zbound + f_q - fend_ref[base + jnp.maximum(kb - 1, 0)]
            return jnp.logical_and(kb > 0, gap >= LOG2_NEGLIGIBLE)

        return lax.while_loop(keep, lambda kb: kb - 1, blk)

    def scores(p, s, back, steps):
        kb = 2 * qi + s - back
        sc = _dot_nt(_q_block(q_ref, p, s), kk_ref[p, jnp.maximum(kb, 0)]) - f2_ref[0, p, pl.ds(jnp.maximum(kb, 0), 1), :]
        return jnp.where(jnp.logical_and(kb >= 0, back <= steps), sc, -1e30)

    def head_max(a):
        return (jnp.max(a[:, :TB], axis=-1, keepdims=True), jnp.max(a[:, TB:], axis=-1, keepdims=True))

    def head_sum(a):
        return (jnp.sum(a[:, :TB], axis=-1, keepdims=True), jnp.sum(a[:, TB:], axis=-1, keepdims=True))

    for p in range(N_PAIRS):
        steps = jnp.int32(0)
        for s in range(2):
            for h in range(2):
                steps = jnp.maximum(steps, 2 * qi + s - first_block(2 * p + h, 2 * qi + s))
        n_it = (steps + 1) // 2

        runmax = []
        for s in range(2):
            sc = jnp.where(causal, scores(p, s, 0, steps), -1e30)
            sc_ref[s, 0] = sc
            runmax.append(sc)

        def pass1(it, runmax):
            out = list(runmax)
            for u in (1, 2):
                back = 2 * it + u
                for s in range(2):
                    sc = scores(p, s, back, steps)
                    sc_ref[s, back] = sc
                    out[s] = jnp.maximum(out[s], sc)
            return out

        runmax = lax.fori_loop(0, n_it, pass1, runmax)
        tops = [_per_head(*head_max(runmax[s])) for s in range(2)]

        def weigh(s, back, runsum, acc):
            pr = jnp.exp2(sc_ref[s, back] - tops[s])
            kb = jnp.maximum(2 * qi + s - back, 0)
            return runsum + pr, acc + _dot(pr.astype(BF16), vv_ref[p, kb])

        state = [weigh(s, 0, jnp.zeros((TB, 2 * TB), F32), jnp.zeros((TB, LANES), F32)) for s in range(2)]

        def pass2(it, state):
            out = list(state)
            for u in (1, 2):
                for s in range(2):
                    out[s] = weigh(s, 2 * it + u, *out[s])
            return out

        state = lax.fori_loop(0, n_it, pass2, state)
        for s in range(2):
            l0, l1 = head_sum(state[s][0])
            o_ref[0, s * TB:(s + 1) * TB, p * LANES:(p + 1) * LANES] = state[s][1] / jnp.where(first, l0, l1)


def _fox_attention(q, k, v, f2, fend, zbound):
    b, s, d = q.shape
    tq = TQ
    n_kb = s // TB
    qspec = pl.BlockSpec((1, tq, d), lambda bi, qi, *_: (bi, qi, 0))
    kvspec = pl.BlockSpec((1, s, d), lambda bi, qi, *_: (bi, 0, 0))
    scratch = pltpu.VMEM((N_PAIRS, n_kb, 2 * TB, LANES), BF16)
    grid_spec = pltpu.PrefetchScalarGridSpec(
        num_scalar_prefetch=2,
        grid=(b, s // tq),
        in_specs=[qspec, kvspec, kvspec,
                  pl.BlockSpec((1, N_PAIRS, n_kb, 2 * TB), lambda bi, qi, *_: (bi, 0, 0, 0))],
        out_specs=qspec,
        scratch_shapes=[scratch, scratch, pltpu.VMEM((2, n_kb + 2, TB, 2 * TB), F32)],
    )
    return pl.pallas_call(
        functools.partial(_fox_kernel, n_kb),
        grid_spec=grid_spec,
        out_shape=jax.ShapeDtypeStruct((b, s, d), F32),
        compiler_params=_cparams(2),
        name="fox_attention",
    )(fend, zbound, q, k, v, f2)


def _outproj_kernel(tiles_per_seq, x_ref, cb_ref, v_ref, vprev_ref, ysb_ref, yfox_ref, cw_ref, og_ref, wout_ref,
                    mod_ref, g2_ref, rw_ref, rb_ref, bd_ref, ltri_ref,
                    x1_ref, h2t_ref, sel_ref, gate_ref, cnt_ref, vext_ref, carry_ref):
    tm = x_ref.shape[0]
    i = pl.program_id(0)

    @pl.when(i == 0)
    def _():
        carry_ref[...] = jnp.zeros_like(carry_ref)

    first = (i % tiles_per_seq) == 0
    vext_ref[0:SUBLANES, :] = jnp.where(first, 0.0, vprev_ref[...])
    vext_ref[SUBLANES:SUBLANES + tm, :] = v_ref[...]
    conv = (cw_ref[0:1, :] * vext_ref[SUBLANES - 2:SUBLANES - 2 + tm, :]
            + cw_ref[1:2, :] * vext_ref[SUBLANES - 1:SUBLANES - 1 + tm, :]
            + cw_ref[2:3, :] * v_ref[...])
    y = jnp.concatenate([cb_ref[...] * conv, ysb_ref[...], yfox_ref[...]], axis=-1)

    ysq = (y * y).astype(BF16)
    bd = bd_ref[...]
    ss = jnp.concatenate([_dot(ysq[:, c:c + 2 * LANES], bd) for c in range(0, D_MIX, 2 * LANES)], axis=-1)
    yn = y * lax.rsqrt(ss * (1.0 / HEAD_DIM) + EPS) * og_ref[...]
    x1 = x_ref[...] + mod_ref[0, 2:3, :] * _dot(yn.astype(BF16), wout_ref[...])
    x1_ref[...] = x1

    ms = jnp.mean(x1 * x1, axis=-1, keepdims=True)
    h2 = (x1 * lax.rsqrt(ms + EPS)) * g2_ref[...]
    h2 = h2 * (1.0 + mod_ref[0, 4:5, :]) + mod_ref[0, 3:4, :]
    for j in range(ROW_TILES):
        h2t_ref[pl.ds(j, tm, stride=SUBLANES), :] = h2[:, j * LANES:(j + 1) * LANES]

    h_hi, h_lo = _split2(h2)
    both = _dot(h_hi, rw_ref[...])
    logits = both[:, :LANES] + both[:, LANES:] + _dot(h_lo, rw_ref[:, :LANES]) + rb_ref[...]
    lane = lax.broadcasted_iota(jnp.int32, (tm, LANES), 1)
    lane_f = lane.astype(F32)
    neg = jnp.float32(-jnp.inf)
    cur = jnp.where(lane < N_EXPERTS, logits, neg)
    tops, sels = [], []
    for _ in range(TOP_K):
        mk = jnp.max(cur, axis=-1, keepdims=True)
        ik = jnp.min(jnp.where(cur == mk, lane_f, float(LANES)), axis=-1, keepdims=True)
        sel = lane_f == ik
        cur = jnp.where(sel, neg, cur)
        tops.append(mk)
        sels.append((sel, ik.astype(jnp.int32)))
    es = [jnp.exp(t - tops[0]) for t in tops]
    inv = 1.0 / (es[0] + es[1] + es[2] + es[3])

    multi = jnp.zeros((tm, LANES), F32)
    for sel, _ in sels:
        multi = jnp.where(sel, 1.0, multi)
    before = _dot(ltri_ref[...], multi.astype(BF16)) + carry_ref[0:1, :]
    carry_ref[0:1, :] = before[tm - 1:tm, :] + multi[tm - 1:tm, :]

    sel_out = jnp.zeros((tm, LANES), jnp.int32)
    gate_out = jnp.zeros((tm, LANES), F32)
    for k, (sel, ik) in enumerate(sels):
        rank = jnp.sum(jnp.where(sel, before, 0.0), axis=-1, keepdims=True).astype(jnp.int32)
        sel_out = jnp.where(lane == k, ik, sel_out)
        sel_out = jnp.where(lane == TOP_K + k, rank, sel_out)
        gate_out = jnp.where(lane == k, es[k] * inv, gate_out)
    sel_ref[...] = sel_out
    gate_ref[...] = gate_out
    cnt_ref[...] = jnp.broadcast_to(carry_ref[0:1, :], cnt_ref.shape)


def _outproj(x2, cb, v, ysb, yfox, conv_w, og, w_out_b, mod3, g2, rw, rb, bd2, ltri, seq):
    t = x2.shape[0]
    tm = TM_PROJ
    tiles_per_seq = seq // tm
    row = lambda i: (i, 0)
    const = lambda i: (0, 0)
    return pl.pallas_call(
        functools.partial(_outproj_kernel, tiles_per_seq),
        grid=(t // tm,),
        in_specs=[pl.BlockSpec((tm, D_MODEL), row),
                  pl.BlockSpec((tm, CONV_CH), row),
                  pl.BlockSpec((tm, CONV_CH), row),
                  pl.BlockSpec((SUBLANES, CONV_CH), lambda i: (jnp.maximum(i * (tm // SUBLANES) - 1, 0), 0)),
                  pl.BlockSpec((tm, SB_DIM), row),
                  pl.BlockSpec((tm, FOX_DIM), row),
                  pl.BlockSpec((CONV_W, CONV_CH), const),
                  pl.BlockSpec((1, D_MIX), const),
                  pl.BlockSpec((D_MIX, D_MODEL), const),
                  pl.BlockSpec((1, N_MOD, D_MODEL), lambda i: (i // tiles_per_seq, 0, 0)),
                  pl.BlockSpec((1, D_MODEL), const),
                  pl.BlockSpec((D_MODEL, 2 * LANES), const),
                  pl.BlockSpec((1, LANES), const),
                  pl.BlockSpec((2 * LANES, 2 * LANES), const),
                  pl.BlockSpec((tm, tm), const)],
        out_specs=[pl.BlockSpec((tm, D_MODEL), row),
                   pl.BlockSpec((tm * SUBLANES, LANES), row),
                   pl.BlockSpec((tm, LANES), row),
                   pl.BlockSpec((tm, LANES), row),
                   pl.BlockSpec((SUBLANES, LANES), const)],
        out_shape=[jax.ShapeDtypeStruct((t, D_MODEL), F32),
                   jax.ShapeDtypeStruct((t * SUBLANES, LANES), F32),
                   jax.ShapeDtypeStruct((t, LANES), jnp.int32),
                   jax.ShapeDtypeStruct((t, LANES), F32),
                   jax.ShapeDtypeStruct((SUBLANES, LANES), F32)],
        scratch_shapes=[pltpu.VMEM((tm + SUBLANES, CONV_CH), F32), pltpu.VMEM((SUBLANES, LANES), F32)],
        compiler_params=_cparams(1),
        name="outproj_router",
    )(x2, cb, v, v, ysb, yfox, conv_w, og, w_out_b, mod3, g2, rw, rb, bd2, ltri)


def _zero_fill(meta_ref, zero_ref, xg_ref, zsem):
    n_tiles = xg_ref.shape[0] // (TM_GROUP * SUBLANES)
    zero_ref[...] = jnp.zeros_like(zero_ref)

    def sweep(wait):
        def per_expert(e, c):
            pos = meta_ref[N_EXPERTS + e]
            n = meta_ref[2 * N_EXPERTS + e]
            bit = TM_GROUP // 2
            while bit:
                @pl.when((n & bit) != 0)
                def _(pos=pos, bit=bit):
                    cp = pltpu.make_async_copy(
                        zero_ref.at[pl.ds(0, bit * SUBLANES), :],
                        xg_ref.at[pl.ds(pl.multiple_of(pos * SUBLANES, SUBLANES), bit * SUBLANES), :], zsem)
                    cp.wait() if wait else cp.start()
                pos = pos + (n & bit)
                bit //= 2
            return c

        lax.fori_loop(0, N_EXPERTS, per_expert, 0)

        def per_tile(j, c):
            rows = TM_GROUP * SUBLANES
            cp = pltpu.make_async_copy(zero_ref, xg_ref.at[pl.ds(pl.multiple_of(j * rows, rows), rows), :], zsem)
            cp.wait() if wait else cp.start()
            return c

        lax.fori_loop(meta_ref[3 * N_EXPERTS], n_tiles, per_tile, 0)

    sweep(False)
    sweep(True)


def _dispatch_kernel(meta_ref, pos_ref, h_ref, xg_ref, zero_ref, sem, zsem):
    tm = h_ref.shape[0] // SUBLANES

    @pl.when(pl.program_id(0) == 0)
    def _():
        _zero_fill(meta_ref, zero_ref, xg_ref, zsem)

    def issue(r, c):
        for k in range(TOP_K):
            _row_copy(h_ref, r, xg_ref, pos_ref[r * TOP_K + k], sem).start()
        return c

    lax.fori_loop(0, tm, issue, 0)
    n = tm * TOP_K * SUBLANES
    pltpu.make_async_copy(xg_ref.at[pl.ds(0, n), :], xg_ref.at[pl.ds(0, n), :], sem).wait()


def _dispatch(meta, pos, h2t, n_rows):
    t = h2t.shape[0] // SUBLANES
    tm = TM_DISPATCH
    grid_spec = pltpu.PrefetchScalarGridSpec(
        num_scalar_prefetch=1,
        grid=(t // tm,),
        in_specs=[pl.BlockSpec((tm * TOP_K,), lambda i, *_: (i,), memory_space=pltpu.SMEM),
                  pl.BlockSpec((tm * SUBLANES, LANES), lambda i, *_: (i, 0))],
        out_specs=pl.BlockSpec(memory_space=pl.ANY),
        scratch_shapes=[pltpu.VMEM((TM_GROUP * SUBLANES, LANES), F32),
                        pltpu.SemaphoreType.DMA(()), pltpu.SemaphoreType.DMA(())],
    )
    return pl.pallas_call(
        _dispatch_kernel,
        grid_spec=grid_spec,
        out_shape=jax.ShapeDtypeStruct((n_rows * SUBLANES, LANES), F32),
        compiler_params=_cparams(1),
        name="moe_dispatch",
    )(meta, pos, h2t)


def _expert_kernel(te_ref, nv_ref, xg_ref, w1_ref, b1_ref, w2_ref, b2_ref, yg_ref, wg_ref, wl_ref, w2b_ref, wt_ref):
    tmg = xg_ref.shape[0] // SUBLANES
    j = pl.program_id(0)
    nv = nv_ref[j]

    @pl.when(jnp.logical_or(j == 0, te_ref[j] != te_ref[jnp.maximum(j - 1, 0)]))
    def _():
        ncol = wt_ref.shape[0] // ROW_TILES
        half = ncol // 2
        for n0 in range(0, 2 * D_EXPERT, ncol):
            wt = w1_ref[0, 0, :, n0:n0 + ncol].T
            for c in range(ROW_TILES):
                wt_ref[pl.ds(c, ncol, stride=ROW_TILES), :] = wt[:, c * LANES:(c + 1) * LANES]
            rows = slice(n0 // 2, n0 // 2 + half)
            for c in range(ROW_TILES):
                cols = slice(c * LANES, (c + 1) * LANES)
                wg_ref[rows, cols] = wt_ref[pl.ds(c, half, stride=2 * ROW_TILES), :].astype(BF16)
                wl_ref[rows, cols] = wt_ref[pl.ds(ROW_TILES + c, half, stride=2 * ROW_TILES), :].astype(BF16)
            w2b_ref[rows, :] = w2_ref[0, 0, rows, :].astype(BF16)

    @pl.when(nv == 0)
    def _():
        yg_ref[...] = jnp.zeros_like(yg_ref)

    @pl.when(nv > 0)
    def _():
        x = jnp.concatenate([xg_ref[pl.ds(c, tmg, stride=SUBLANES), :] for c in range(ROW_TILES)], axis=-1)
        x = x.astype(BF16)
        glu = jnp.minimum(_dot_nt(x, wg_ref[...]) + b1_ref[0, 0:1, :], SWIGLU_LIMIT)
        lin = jnp.clip(_dot_nt(x, wl_ref[...]) + b1_ref[0, 1:2, :], -SWIGLU_LIMIT, SWIGLU_LIMIT)
        a = glu * (1.0 / (1.0 + jnp.exp(-SWIGLU_ALPHA * glu))) * (lin + 1.0)
        y = _dot(a.astype(BF16), w2b_ref[...]) + b2_ref[0]
        for c in range(ROW_TILES):
            yg_ref[pl.ds(c, tmg, stride=SUBLANES), :] = y[:, c * LANES:(c + 1) * LANES]


def _experts(te, nv, xg, layer, w1, b1, w2, b2):
    tmg = TM_GROUP
    n_tiles = xg.shape[0] // (tmg * SUBLANES)
    grid_spec = pltpu.PrefetchScalarGridSpec(
        num_scalar_prefetch=2,
        grid=(n_tiles,),
        in_specs=[pl.BlockSpec((tmg * SUBLANES, LANES), lambda j, te, nv: (j, 0)),
                  pl.BlockSpec((1, 1, D_MODEL, 2 * D_EXPERT), lambda j, te, nv: (layer, te[j], 0, 0)),
                  pl.BlockSpec((1, 2, D_EXPERT), lambda j, te, nv: (te[j], 0, 0)),
                  pl.BlockSpec((1, 1, D_EXPERT, D_MODEL), lambda j, te, nv: (layer, te[j], 0, 0)),
                  pl.BlockSpec((1, 1, D_MODEL), lambda j, te, nv: (te[j], 0, 0))],
        out_specs=pl.BlockSpec((tmg * SUBLANES, LANES), lambda j, te, nv: (j, 0)),
        scratch_shapes=[pltpu.VMEM((D_EXPERT, D_MODEL), BF16), pltpu.VMEM((D_EXPERT, D_MODEL), BF16),
                        pltpu.VMEM((D_EXPERT, D_MODEL), BF16), pltpu.VMEM((2 * LANES * ROW_TILES, LANES), F32)],
    )
    return pl.pallas_call(
        _expert_kernel,
        grid_spec=grid_spec,
        out_shape=jax.ShapeDtypeStruct(xg.shape, F32),
        compiler_params=_cparams(1, EXPERT_VMEM_LIMIT),
        name="moe_experts",
    )(te, nv, xg, w1, b1, w2, b2)


def _combine_kernel(pos_ref, pos_next_ref, gate_ref, x1_ref, mod_ref, yg_ref, o_ref, buf_ref, sem):
    tm = x1_ref.shape[0]
    i = pl.program_id(0)
    slot = i % 2

    @pl.when(i == 0)
    def _():
        _gather_rows(yg_ref, pos_ref, buf_ref, sem, 0, tm)

    @pl.when(i + 1 < pl.num_programs(0))
    def _():
        _gather_rows(yg_ref, pos_next_ref, buf_ref, sem, 1 - slot, tm)

    _wait_rows(buf_ref, sem, slot)
    _combine_rows(gate_ref, buf_ref, slot, tm, x1_ref, mod_ref[0, 5:6, :], o_ref)


def _combine(pos, gates, x1, mod3, yg, seq):
    t = x1.shape[0]
    tm = TM_COMBINE
    tiles_per_seq = seq // tm
    n = t // tm
    grid_spec = pltpu.PrefetchScalarGridSpec(
        num_scalar_prefetch=0,
        grid=(n,),
        in_specs=[pl.BlockSpec((tm * TOP_K,), lambda i: (i,), memory_space=pltpu.SMEM),
                  pl.BlockSpec((tm * TOP_K,), lambda i: (jnp.minimum(i + 1, n - 1),), memory_space=pltpu.SMEM),
                  pl.BlockSpec((tm, LANES), lambda i: (i, 0)),
                  pl.BlockSpec((tm, D_MODEL), lambda i: (i, 0)),
                  pl.BlockSpec((1, N_MOD, D_MODEL), lambda i: (i // tiles_per_seq, 0, 0)),
                  pl.BlockSpec(memory_space=pl.ANY)],
        out_specs=pl.BlockSpec((tm, D_MODEL), lambda i: (i, 0)),
        scratch_shapes=[pltpu.VMEM((2, TOP_K, tm * SUBLANES, LANES), F32), pltpu.SemaphoreType.DMA((2,))],
    )
    return pl.pallas_call(
        _combine_kernel,
        grid_spec=grid_spec,
        out_shape=jax.ShapeDtypeStruct((t, D_MODEL), F32),
        compiler_params=_cparams(1),
        name="moe_combine",
    )(pos, pos, gates, x1, mod3, yg)


def _block_diag_ones(n):
    i = np.arange(n)
    return jnp.asarray((i[:, None] // HEAD_DIM) == (i[None, :] // HEAD_DIM), dtype=BF16)


def _tri(n, strict, upper):
    i = np.arange(n)
    if upper:
        m = (i[:, None] > i[None, :]) if strict else (i[:, None] >= i[None, :])
        return jnp.asarray(m, dtype=BF16)
    m = (i[None, :] < i[:, None]) if strict else (i[None, :] <= i[:, None])
    return jnp.asarray(m, dtype=BF16)


def _tile2(g):
    return jnp.tile(g, SB_DIM // HEAD_DIM)


def _layer(x_parts, c, seq, layer, last, w_mlp1, w_mlp2, norm1_g, w_ada, b_ada, w_in, conv_w, sb_q_g, sb_k_g, fox_q_g, fox_k_g, fox_f_b,
           out_norm_g, w_out, norm2_g, router_w, router_b, b_mlp1, b_mlp2):
    t = x_parts[0].shape[0] if len(x_parts) == 1 else x_parts[2].shape[0]
    nb = t // seq
    scale = HEAD_DIM ** -0.5

    w_in_p = jnp.pad(w_in, ((0, 0), (0, D_IN_PAD - w_in.shape[1]))).astype(BF16)
    fb = jnp.pad(fox_f_b, (0, LANES - FOX_HEADS)).reshape(1, LANES)
    qscale = scale * LOG2_E
    gains = jnp.stack([_tile2(sb_q_g) * qscale, _tile2(sb_k_g), _tile2(fox_q_g) * qscale, _tile2(fox_k_g)])
    gains = jnp.pad(gains, ((0, SUBLANES - 4), (0, 0)))
    rw = jnp.pad(router_w, ((0, 0), (0, LANES - N_EXPERTS)))
    rw_hi = rw.astype(BF16)
    rw = jnp.concatenate([rw_hi, (rw - rw_hi.astype(F32)).astype(BF16)], axis=1)
    rb = jnp.pad(router_b, (0, LANES - N_EXPERTS)).reshape(1, LANES)
    b1 = jnp.stack([b_mlp1[:, 0::2], b_mlp1[:, 1::2]], axis=1)
    b2 = b_mlp2.reshape(N_EXPERTS, 1, D_MODEL)

    mod3 = _modulation(c, w_ada, b_ada).reshape(nb, N_MOD, D_MODEL)

    outs = _inproj(x_parts, mod3, norm1_g.reshape(1, D_MODEL), w_in_p, _block_diag_ones(SB_DIM),
                   _tri(TM_PROJ, strict=False, upper=False), fb, gains, seq)
    x2 = x_parts[0] if len(x_parts) == 1 else outs[0]
    cb, v, sbq, sbk, sbv, fq, fk, fv, cum = outs[-9:]

    r3 = lambda a: a.reshape(nb, seq, SB_DIM)
    u = _tri(TB, strict=True, upper=True)
    zu = jnp.zeros_like(u)
    uu = jnp.concatenate([jnp.concatenate([u, zu], axis=1), jnp.concatenate([zu, u], axis=1)], axis=0)
    ysb = _sb_attention(r3(sbq), r3(sbk), r3(sbv), uu)

    n_kb = seq // TB
    f2 = cum[:, :FOX_HEADS, :].reshape(nb, FOX_DIM // LANES, 2, n_kb, TB).transpose(0, 1, 3, 2, 4)
    f2 = f2.reshape(nb, FOX_DIM // LANES, n_kb, 2 * TB)
    fend = cum[:, :FOX_HEADS, TB - 1::TB].reshape(nb * FOX_HEADS * n_kb)
    zbound = (2.0 * 1.02 * HEAD_DIM * qscale * jnp.max(jnp.abs(fox_q_g)) * jnp.max(jnp.abs(fox_k_g))).reshape(1)
    yfox = _fox_attention(r3(fq), r3(fk), r3(fv), f2, fend, zbound)

    x1, h2t, sel, gates, cnt = _outproj(
        x2, cb, v, ysb.reshape(t, SB_DIM), yfox.reshape(t, FOX_DIM), conv_w, out_norm_g.reshape(1, D_MIX),
        w_out.astype(BF16), mod3, norm2_g.reshape(1, D_MODEL), rw, rb, _block_diag_ones(2 * LANES),
        _tri(TM_PROJ, strict=True, upper=False), seq)

    counts = cnt[0, :N_EXPERTS].astype(jnp.int32)
    padded = ((counts + TM_GROUP - 1) // TM_GROUP) * TM_GROUP
    off_end = jnp.cumsum(padded)
    off = off_end - padded
    n_rows = t * TOP_K + N_EXPERTS * TM_GROUP
    n_tiles = n_rows // TM_GROUP
    start = jnp.arange(n_tiles, dtype=jnp.int32) * TM_GROUP
    te = jnp.minimum(jnp.sum(start[:, None] >= off_end[None, :], axis=1), N_EXPERTS - 1).astype(jnp.int32)
    nv = jnp.clip(off[te] + counts[te] - start, 0, TM_GROUP).astype(jnp.int32)
    pos = (off[sel[:, :TOP_K]] + sel[:, TOP_K:2 * TOP_K]).astype(jnp.int32).reshape(t * TOP_K)
    meta = jnp.concatenate([off, off + counts, padded - counts, off_end[-1:] // TM_GROUP]).astype(jnp.int32)

    xg = _dispatch(meta, pos, h2t, n_rows)
    yg = _experts(te, nv, xg, layer, w_mlp1, b1, w_mlp2, b2)
    return _combine(pos, gates, x1, mod3, yg, seq) if last else (pos, gates, x1, mod3, yg)


def kernel(x, c, norm1_g, w_ada, b_ada, w_in, conv_w, sb_q_g, sb_k_g, fox_q_g, fox_k_g, fox_f_b, out_norm_g, w_out,
           norm2_g, router_w, router_b, w_mlp1, b_mlp1, w_mlp2, b_mlp2):
    b, s, d = x.shape
    x_parts = (x.reshape(b * s, d),)
    params = (norm1_g, w_ada, b_ada, w_in, conv_w, sb_q_g, sb_k_g, fox_q_g, fox_k_g, fox_f_b, out_norm_g, w_out,
              norm2_g, router_w, router_b, b_mlp1, b_mlp2)
    depth = norm1_g.shape[0]
    for layer in range(depth):
        x_parts = _layer(x_parts, c, s, layer, layer == depth - 1, w_mlp1, w_mlp2, *(p[layer] for p in params))
    return x_parts.reshape(b, s, d)
```

```python
import functools

import numpy as np
import jax
import jax.numpy as jnp
from jax import lax
from jax.experimental import pallas as pl
from jax.experimental.pallas import tpu as pltpu

F32 = jnp.float32
BF16 = jnp.bfloat16

D_MODEL = 1024
HEAD_DIM = 64
CONV_CH = 256
CONV_W = 3
SB_DIM = 384
FOX_DIM = 384
FOX_HEADS = 6
N_PAIRS = 3
D_MIX = 1024
N_MOD = 6
N_EXPERTS = 32
TOP_K = 4
D_EXPERT = 1024
SWIGLU_ALPHA = 1.702
SWIGLU_LIMIT = 7.0
EPS = 1e-6

LANES = 128
SUBLANES = 8
ROW_TILES = D_MODEL // LANES
D_IN_PAD = 3 * CONV_CH + 3 * SB_DIM + 3 * FOX_DIM + LANES
COL_SBQ = 3 * CONV_CH
COL_SBK = COL_SBQ + SB_DIM
COL_SBV = COL_SBK + SB_DIM
COL_FQ = COL_SBV + SB_DIM
COL_FK = COL_FQ + FOX_DIM
COL_FV = COL_FK + FOX_DIM
COL_FLOG = COL_FV + FOX_DIM

TM_PROJ = 512
TQ = 256
TB = 128
TM_DISPATCH = 1024
TM_COMBINE = 256
TM_GROUP = 512
RANK_BITS = 20
LOG2_E = 1.4426950408889634
LOG2_NEGLIGIBLE = -40.0
VMEM_LIMIT = 48 * 1024 * 1024
INPROJ_VMEM_LIMIT = 56 * 1024 * 1024
EXPERT_VMEM_LIMIT = 58 * 1024 * 1024


def _cparams(n_axes, vmem=VMEM_LIMIT):
    return pltpu.CompilerParams(dimension_semantics=("arbitrary",) * n_axes, vmem_limit_bytes=vmem)


def _dot(a, b):
    return jnp.dot(a, b, preferred_element_type=F32)


def _dot_nt(a, b):
    return lax.dot_general(a, b, (((1,), (1,)), ((), ())), preferred_element_type=F32)


def _split2(x):
    hi = x.astype(BF16)
    lo = (x - hi.astype(F32)).astype(BF16)
    return hi, lo


def _row_copy(src, src_row, dst, dst_row, sem):
    return pltpu.make_async_copy(src.at[pl.ds(pl.multiple_of(src_row * SUBLANES, SUBLANES), SUBLANES), :],
                                 dst.at[pl.ds(pl.multiple_of(dst_row * SUBLANES, SUBLANES), SUBLANES), :], sem)


def _log_sigmoid(x):
    return jnp.minimum(x, 0.0) - jnp.log(1.0 + jnp.exp(-jnp.abs(x)))


def _mod_kernel(c_ref, w_ref, b_ref, o_ref):
    c = c_ref[...]
    s = c / (1.0 + jnp.exp(-c))
    o_ref[...] = _dot(s.astype(BF16), w_ref[...].astype(BF16)) + b_ref[...]


def _modulation(c, w_ada, b_ada):
    b, d = c.shape
    n = w_ada.shape[1]
    tn = 1024
    return pl.pallas_call(
        _mod_kernel,
        grid=(n // tn,),
        in_specs=[pl.BlockSpec((b, d), lambda j: (0, 0)),
                  pl.BlockSpec((d, tn), lambda j: (0, j)),
                  pl.BlockSpec((1, tn), lambda j: (0, j))],
        out_specs=pl.BlockSpec((b, tn), lambda j: (0, j)),
        out_shape=jax.ShapeDtypeStruct((b, n), F32),
        compiler_params=_cparams(1),
        name="modulation",
    )(c, w_ada, b_ada.reshape(1, n))


def _gather_rows(yg_ref, idx_ref, buf_ref, sem, slot, n_rows):
    def issue(r, c):
        for k in range(TOP_K):
            _row_copy(yg_ref, idx_ref[r * TOP_K + k], buf_ref.at[slot, k], r, sem.at[slot]).start()
        return c

    lax.fori_loop(0, n_rows, issue, 0)


def _combine_rows(gate_ref, buf_ref, slot, n_rows, x1_ref, g2, out_ref):
    gates = gate_ref[...]
    gate_cols = [gates[:, k:k + 1] for k in range(TOP_K)]
    for c in range(ROW_TILES):
        acc = jnp.zeros((n_rows, LANES), F32)
        for k in range(TOP_K):
            acc = acc + gate_cols[k] * buf_ref[slot, k, pl.ds(c, n_rows, stride=SUBLANES), :]
        cols = slice(c * LANES, (c + 1) * LANES)
        out_ref[:, cols] = x1_ref[:, cols] + g2[:, cols] * acc


def _wait_rows(buf_ref, sem, slot):
    for k in range(TOP_K):
        pltpu.make_async_copy(buf_ref.at[slot, k], buf_ref.at[slot, k], sem.at[slot]).wait()


def _inproj_kernel(tiles_per_seq, fused, *refs):
    if fused:
        (pos_ref, pos_next_ref, gate_ref, x1_ref, modp_ref, yg_ref, mod_ref, g1_ref, w_ref, bd_ref, tri_ref, fb_ref,
         gains_ref, x_out_ref, cb_ref, v_ref, sbq_ref, sbk_ref, sbv_ref, fq_ref, fk_ref, fv_ref, cum_ref,
         carry_ref, x_ref, buf_ref, sem) = refs
        tm = x1_ref.shape[0]
        i = pl.program_id(0)
        slot = i % 2

        @pl.when(i == 0)
        def _():
            _gather_rows(yg_ref, pos_ref, buf_ref, sem, 0, tm)

        @pl.when(i + 1 < pl.num_programs(0))
        def _():
            _gather_rows(yg_ref, pos_next_ref, buf_ref, sem, 1 - slot, tm)

        _wait_rows(buf_ref, sem, slot)
        _combine_rows(gate_ref, buf_ref, slot, tm, x1_ref, modp_ref[0, 5:6, :], x_ref)
        x_out_ref[...] = x_ref[...]
    else:
        (x_ref, mod_ref, g1_ref, w_ref, bd_ref, tri_ref, fb_ref, gains_ref,
         cb_ref, v_ref, sbq_ref, sbk_ref, sbv_ref, fq_ref, fk_ref, fv_ref, cum_ref, carry_ref) = refs
        tm = x_ref.shape[0]
    x = x_ref[...]
    ms = jnp.mean(x * x, axis=-1, keepdims=True)
    h = (x * lax.rsqrt(ms + EPS)) * g1_ref[...]
    h = h * (1.0 + mod_ref[0, 1:2, :]) + mod_ref[0, 0:1, :]
    hb = h.astype(BF16)

    pc = _dot(hb, w_ref[:, 0:3 * CONV_CH])
    cb_ref[...] = pc[:, 0:CONV_CH]
    v_ref[...] = pc[:, CONV_CH:2 * CONV_CH] * pc[:, 2 * CONV_CH:3 * CONV_CH]

    def normed(col, row):
        p = _dot(hb, w_ref[:, col:col + SB_DIM])
        ss = _dot((p * p).astype(BF16), bd_ref[...])
        return (p * lax.rsqrt(ss * (1.0 / HEAD_DIM) + EPS) * gains_ref[row:row + 1, :]).astype(BF16)

    sbq_ref[...] = normed(COL_SBQ, 0)
    sbk_ref[...] = normed(COL_SBK, 1)
    sbv_ref[...] = _dot(hb, w_ref[:, COL_SBV:COL_SBV + SB_DIM]).astype(BF16)
    fq_ref[...] = normed(COL_FQ, 2)
    fk_ref[...] = normed(COL_FK, 3)
    fv_ref[...] = _dot(hb, w_ref[:, COL_FV:COL_FV + FOX_DIM]).astype(BF16)

    lf = _log_sigmoid(_dot(hb, w_ref[:, COL_FLOG:COL_FLOG + LANES]) + fb_ref[...]) * LOG2_E
    t0 = lf.astype(BF16)
    r1 = lf - t0.astype(F32)
    t1 = r1.astype(BF16)
    t2 = (r1 - t1.astype(F32)).astype(BF16)
    tri = tri_ref[...]
    cs = _dot(tri, t0) + _dot(tri, t1) + _dot(tri, t2)

    @pl.when(pl.program_id(0) % tiles_per_seq == 0)
    def _():
        carry_ref[...] = jnp.zeros_like(carry_ref)

    cum = cs + carry_ref[0:1, :]
    carry_ref[0:1, :] = cum[tm - 1:tm, :]
    cum_ref[0] = cum.T[0:SUBLANES, :]


def _inproj(x_parts, mod3, g1, w_in_p, bd, tri, fb, gains, seq):
    fused = len(x_parts) > 1
    t = x_parts[2].shape[0] if fused else x_parts[0].shape[0]
    tm = TM_PROJ
    n = t // tm
    tiles_per_seq = seq // tm
    nb = t // seq
    row = lambda i: (i, 0)
    const = lambda i: (0, 0)
    per_seq = lambda i: (i // tiles_per_seq, 0, 0)
    outs = [jax.ShapeDtypeStruct((t, CONV_CH), F32), jax.ShapeDtypeStruct((t, CONV_CH), F32)]
    outs += [jax.ShapeDtypeStruct((t, SB_DIM), BF16)] * 6
    outs += [jax.ShapeDtypeStruct((nb, SUBLANES, seq), F32)]
    out_specs = [pl.BlockSpec((tm, CONV_CH), row)] * 2 + [pl.BlockSpec((tm, SB_DIM), row)] * 6
    out_specs += [pl.BlockSpec((1, SUBLANES, tm), lambda i: (i // tiles_per_seq, 0, i % tiles_per_seq))]
    in_specs = [pl.BlockSpec((1, N_MOD, D_MODEL), per_seq),
                pl.BlockSpec((1, D_MODEL), const),
                pl.BlockSpec((D_MODEL, D_IN_PAD), const),
                pl.BlockSpec((SB_DIM, SB_DIM), const),
                pl.BlockSpec((tm, tm), const),
                pl.BlockSpec((1, LANES), const),
                pl.BlockSpec((SUBLANES, SB_DIM), const)]
    scratch = [pltpu.VMEM((SUBLANES, LANES), F32)]
    if fused:
        pos, gates, x1, mod3_prev, yg = x_parts
        operands = (pos, pos, gates, x1, mod3_prev, yg)
        in_specs = [pl.BlockSpec((tm * TOP_K,), lambda i: (i,), memory_space=pltpu.SMEM),
                    pl.BlockSpec((tm * TOP_K,), lambda i: (jnp.minimum(i + 1, n - 1),), memory_space=pltpu.SMEM),
                    pl.BlockSpec((tm, LANES), row),
                    pl.BlockSpec((tm, D_MODEL), row),
                    pl.BlockSpec((1, N_MOD, D_MODEL), per_seq),
                    pl.BlockSpec(memory_space=pl.ANY)] + in_specs
        outs = [jax.ShapeDtypeStruct((t, D_MODEL), F32)] + outs
        out_specs = [pl.BlockSpec((tm, D_MODEL), row)] + out_specs
        scratch += [pltpu.VMEM((tm, D_MODEL), F32), pltpu.VMEM((2, TOP_K, tm * SUBLANES, LANES), F32),
                    pltpu.SemaphoreType.DMA((2,))]
    else:
        operands = x_parts
        in_specs = [pl.BlockSpec((tm, D_MODEL), row)] + in_specs
    return pl.pallas_call(
        functools.partial(_inproj_kernel, tiles_per_seq, fused),
        grid=(n,),
        in_specs=in_specs,
        out_specs=out_specs,
        out_shape=outs,
        scratch_shapes=scratch,
        compiler_params=_cparams(1, INPROJ_VMEM_LIMIT),
        name="inproj",
    )(*operands, mod3, g1, w_in_p, bd, tri, fb, gains)


def _stack_heads(k_ref, v_ref, kk_ref, vv_ref):
    n_blocks = kk_ref.shape[1]
    lane = lax.broadcasted_iota(jnp.int32, (TB, LANES), 1)
    first = lane < HEAD_DIM

    def body(j, c):
        rows = pl.ds(pl.multiple_of(j * TB, TB), TB)
        for src, dst in ((k_ref, kk_ref), (v_ref, vv_ref)):
            for p in range(N_PAIRS):
                blk = src[0, rows, p * LANES:(p + 1) * LANES]
                zero = jnp.zeros_like(blk)
                dst[p, j, 0:TB, :] = jnp.where(first, blk, zero)
                dst[p, j, TB:2 * TB, :] = jnp.where(first, zero, blk)
        return c

    lax.fori_loop(0, n_blocks, body, 0)


def _per_head(a0, a1):
    return jnp.concatenate([jnp.broadcast_to(a0, (TB, TB)), jnp.broadcast_to(a1, (TB, TB))], axis=1)


def _pair_iota():
    row = lax.broadcasted_iota(jnp.int32, (TB, 2 * TB), 0)
    col = lax.broadcasted_iota(jnp.int32, (TB, 2 * TB), 1) % TB
    return row, col


_CHAINS = [(p, s) for p in range(N_PAIRS) for s in range(2)]


def _q_block(q_ref, p, s):
    return q_ref[0, s * TB:(s + 1) * TB, p * LANES:(p + 1) * LANES]


def _sb_kernel(q_ref, k_ref, v_ref, uu_ref, o_ref, kk_ref, vv_ref):
    qi = pl.program_id(1)

    @pl.when(qi == 0)
    def _():
        _stack_heads(k_ref, v_ref, kk_ref, vv_ref)

    uu = uu_ref[...]
    row, col = _pair_iota()
    strict = col < row

    def sweep(states, backs):
        items = [(c, b) for b in range(len(backs)) for c in range(len(_CHAINS))]
        kbs, keeps, z = {}, {}, {}
        for c, b in items:
            p, s = _CHAINS[c]
            back = backs[b]
            kb = 2 * qi + s - back
            if isinstance(back, int) and back == 0:
                keeps[c, b] = strict
            elif isinstance(back, int) and back == 1 and s == 1:
                keeps[c, b] = None
            else:
                keeps[c, b] = kb >= 0
            kbs[c, b] = jnp.maximum(kb, 0)
            z[c, b] = _dot_nt(_q_block(q_ref, p, s), kk_ref[p, kbs[c, b]])
        log_beta, log_1mb, sums = {}, {}, {}
        for it in items:
            log_beta[it] = jnp.minimum(z[it], 0.0) - jnp.log2(1.0 + jnp.exp2(-jnp.abs(z[it])))
            l1 = log_beta[it] - z[it]
            log_1mb[it] = l1 if keeps[it] is None else jnp.where(keeps[it], l1, 0.0)
            sums[it] = (jnp.sum(log_1mb[it][:, :TB], axis=-1, keepdims=True),
                        jnp.sum(log_1mb[it][:, TB:], axis=-1, keepdims=True))
        later = {}
        for it in items:
            hi, lo = _split2(log_1mb[it])
            later[it] = _dot(hi, uu) + _dot(lo, uu)
        states = list(states)
        for it in items:
            c = it[0]
            rs0, rs1, acc = states[c]
            w = jnp.exp2(log_beta[it] + later[it] + _per_head(rs0, rs1))
            if keeps[it] is not None:
                w = jnp.where(keeps[it], w, 0.0)
            acc = acc + _dot(w.astype(BF16), vv_ref[_CHAINS[c][0], kbs[it]])
            states[c] = (rs0 + sums[it][0], rs1 + sums[it][1], acc)
        return states

    def more(states):
        top = states[0][0]
        for st in states:
            top = jnp.maximum(top, jnp.maximum(st[0], st[1]))
        return (jnp.max(top) > LOG2_NEGLIGIBLE).astype(jnp.int32)

    zero = jnp.zeros((TB, 1), F32)
    states = sweep([(zero, zero, jnp.zeros((TB, LANES), F32))] * len(_CHAINS), (0, 1))

    def cond(c):
        return jnp.logical_and(2 * qi + 1 - c[0] >= 0, c[2] > 0)

    def body(c):
        back, states, _ = c
        states = sweep(states, (back,))
        return back + 1, states, more(states)

    _, states, _ = lax.while_loop(cond, body, (jnp.int32(2), states, more(states)))
    for (p, s), st in zip(_CHAINS, states):
        o_ref[0, s * TB:(s + 1) * TB, p * LANES:(p + 1) * LANES] = st[2]


def _sb_attention(q, k, v, uu):
    b, s, d = q.shape
    tq = TQ
    qspec = pl.BlockSpec((1, tq, d), lambda bi, qi: (bi, qi, 0))
    kvspec = pl.BlockSpec((1, s, d), lambda bi, qi: (bi, 0, 0))
    scratch = pltpu.VMEM((N_PAIRS, s // TB, 2 * TB, LANES), BF16)
    return pl.pallas_call(
        _sb_kernel,
        grid=(b, s // tq),
        in_specs=[qspec, kvspec, kvspec, pl.BlockSpec((2 * TB, 2 * TB), lambda bi, qi: (0, 0))],
        out_specs=qspec,
        out_shape=jax.ShapeDtypeStruct((b, s, d), F32),
        scratch_shapes=[scratch, scratch],
        compiler_params=_cparams(2),
        name="sb_attention",
    )(q, k, v, uu)


def _fox_kernel(n_kb, fend_ref, zb_ref, q_ref, k_ref, v_ref, f2_ref, o_ref, kk_ref, vv_ref, sc_ref):
    bi = pl.program_id(0)
    qi = pl.program_id(1)

    @pl.when(qi == 0)
    def _():
        _stack_heads(k_ref, v_ref, kk_ref, vv_ref)

    row, col = _pair_iota()
    causal = col <= row
    lane = lax.broadcasted_iota(jnp.int32, (TB, LANES), 1)
    first = lane < HEAD_DIM
    zbound = zb_ref[0]

    def first_block(head, blk):
        base = (bi * FOX_HEADS + head) * n_kb
        f_q = fend_ref[base + jnp.maximum(blk - 1, 0)]

        def keep(kb):
            gap = zbound + f_q - fend_ref[base + jnp.maximum(kb - 1, 0)]
            return jnp.logical_and(kb > 0, gap >= LOG2_NEGLIGIBLE)

        return lax.while_loop(keep, lambda kb: kb - 1, blk)

    def scores(p, s, back, steps):
        kb = 2 * qi + s - back
        sc = _dot_nt(_q_block(q_ref, p, s), kk_ref[p, jnp.maximum(kb, 0)]) - f2_ref[0, p, pl.ds(jnp.maximum(kb, 0), 1), :]
        return jnp.where(jnp.logical_and(kb >= 0, back <= steps), sc, -1e30)

    def head_max(a):
        return (jnp.max(a[:, :TB], axis=-1, keepdims=True), jnp.max(a[:, TB:], axis=-1, keepdims=True))

    def head_sum(a):
        return (jnp.sum(a[:, :TB], axis=-1, keepdims=True), jnp.sum(a[:, TB:], axis=-1, keepdims=True))

    for p in range(N_PAIRS):
        steps = jnp.int32(0)
        for s in range(2):
            for h in range(2):
                steps = jnp.maximum(steps, 2 * qi + s - first_block(2 * p + h, 2 * qi + s))
        n_it = (steps + 1) // 2

        runmax = []
        for s in range(2):
            sc = jnp.where(causal, scores(p, s, 0, steps), -1e30)
            sc_ref[s, 0] = sc
            runmax.append(sc)

        def pass1(it, runmax):
            out = list(runmax)
            for u in (1, 2):
                back = 2 * it + u
                for s in range(2):
                    sc = scores(p, s, back, steps)
                    sc_ref[s, back] = sc
                    out[s] = jnp.maximum(out[s], sc)
            return out

        runmax = lax.fori_loop(0, n_it, pass1, runmax)
        tops = [_per_head(*head_max(runmax[s])) for s in range(2)]

        def weigh(s, back, runsum, acc):
            pr = jnp.exp2(sc_ref[s, back] - tops[s])
            kb = jnp.maximum(2 * qi + s - back, 0)
            return runsum + pr, acc + _dot(pr.astype(BF16), vv_ref[p, kb])

        state = [weigh(s, 0, jnp.zeros((TB, 2 * TB), F32), jnp.zeros((TB, LANES), F32)) for s in range(2)]

        def pass2(it, state):
            out = list(state)
            for u in (1, 2):
                for s in range(2):
                    out[s] = weigh(s, 2 * it + u, *out[s])
            return out

        state = lax.fori_loop(0, n_it, pass2, state)
        for s in range(2):
            l0, l1 = head_sum(state[s][0])
            o_ref[0, s * TB:(s + 1) * TB, p * LANES:(p + 1) * LANES] = state[s][1] / jnp.where(first, l0, l1)


def _fox_attention(q, k, v, f2, fend, zbound):
    b, s, d = q.shape
    tq = TQ
    n_kb = s // TB
    qspec = pl.BlockSpec((1, tq, d), lambda bi, qi, *_: (bi, qi, 0))
    kvspec = pl.BlockSpec((1, s, d), lambda bi, qi, *_: (bi, 0, 0))
    scratch = pltpu.VMEM((N_PAIRS, n_kb, 2 * TB, LANES), BF16)
    grid_spec = pltpu.PrefetchScalarGridSpec(
        num_scalar_prefetch=2,
        grid=(b, s // tq),
        in_specs=[qspec, kvspec, kvspec,
                  pl.BlockSpec((1, N_PAIRS, n_kb, 2 * TB), lambda bi, qi, *_: (bi, 0, 0, 0))],
        out_specs=qspec,
        scratch_shapes=[scratch, scratch, pltpu.VMEM((2, n_kb + 2, TB, 2 * TB), F32)],
    )
    return pl.pallas_call(
        functools.partial(_fox_kernel, n_kb),
        grid_spec=grid_spec,
        out_shape=jax.ShapeDtypeStruct((b, s, d), F32),
        compiler_params=_cparams(2),
        name="fox_attention",
    )(fend, zbound, q, k, v, f2)


def _outproj_kernel(tiles_per_seq, x_ref, cb_ref, v_ref, vprev_ref, ysb_ref, yfox_ref, cw_ref, og_ref, wout_ref,
                    mod_ref, g2_ref, rw_ref, rb_ref, bd_ref, ltri_ref,
                    x1_ref, h2t_ref, sel_ref, gate_ref, cnt_ref, vext_ref, carry_ref):
    tm = x_ref.shape[0]
    i = pl.program_id(0)

    @pl.when(i == 0)
    def _():
        carry_ref[...] = jnp.zeros_like(carry_ref)

    first = (i % tiles_per_seq) == 0
    vext_ref[0:SUBLANES, :] = jnp.where(first, 0.0, vprev_ref[...])
    vext_ref[SUBLANES:SUBLANES + tm, :] = v_ref[...]
    conv = (cw_ref[0:1, :] * vext_ref[SUBLANES - 2:SUBLANES - 2 + tm, :]
            + cw_ref[1:2, :] * vext_ref[SUBLANES - 1:SUBLANES - 1 + tm, :]
            + cw_ref[2:3, :] * v_ref[...])
    y = jnp.concatenate([cb_ref[...] * conv, ysb_ref[...], yfox_ref[...]], axis=-1)

    ysq = (y * y).astype(BF16)
    bd = bd_ref[...]
    ss = jnp.concatenate([_dot(ysq[:, c:c + 2 * LANES], bd) for c in range(0, D_MIX, 2 * LANES)], axis=-1)
    yn = y * lax.rsqrt(ss * (1.0 / HEAD_DIM) + EPS) * og_ref[...]
    x1 = x_ref[...] + mod_ref[0, 2:3, :] * _dot(yn.astype(BF16), wout_ref[...])
    x1_ref[...] = x1

    ms = jnp.mean(x1 * x1, axis=-1, keepdims=True)
    h2 = (x1 * lax.rsqrt(ms + EPS)) * g2_ref[...]
    h2 = h2 * (1.0 + mod_ref[0, 4:5, :]) + mod_ref[0, 3:4, :]
    for j in range(ROW_TILES):
        h2t_ref[pl.ds(j, tm, stride=SUBLANES), :] = h2[:, j * LANES:(j + 1) * LANES]

    h_hi, h_lo = _split2(h2)
    both = _dot(h_hi, rw_ref[...])
    logits = both[:, :LANES] + both[:, LANES:] + _dot(h_lo, rw_ref[:, :LANES]) + rb_ref[...]
    lane = lax.broadcasted_iota(jnp.int32, (tm, LANES), 1)
    lane_f = lane.astype(F32)
    neg = jnp.float32(-jnp.inf)
    cur = jnp.where(lane < N_EXPERTS, logits, neg)
    tops, sels = [], []
    for _ in range(TOP_K):
        mk = jnp.max(cur, axis=-1, keepdims=True)
        ik = jnp.min(jnp.where(cur == mk, lane_f, float(LANES)), axis=-1, keepdims=True)
        sel = lane_f == ik
        cur = jnp.where(sel, neg, cur)
        tops.append(mk)
        sels.append((sel, ik.astype(jnp.int32)))
    es = [jnp.exp(t - tops[0]) for t in tops]
    inv = 1.0 / (es[0] + es[1] + es[2] + es[3])

    multi = jnp.zeros((tm, LANES), F32)
    for sel, _ in sels:
        multi = jnp.where(sel, 1.0, multi)
    before = _dot(ltri_ref[...], multi.astype(BF16)) + carry_ref[0:1, :]
    carry_ref[0:1, :] = before[tm - 1:tm, :] + multi[tm - 1:tm, :]

    sel_out = jnp.zeros((tm, LANES), jnp.int32)
    gate_out = jnp.zeros((tm, LANES), F32)
    for k, (sel, ik) in enumerate(sels):
        rank = jnp.sum(jnp.where(sel, before, 0.0), axis=-1, keepdims=True).astype(jnp.int32)
        sel_out = jnp.where(lane == k, ik * (1 << RANK_BITS) + rank, sel_out)
        gate_out = jnp.where(lane == k, es[k] * inv, gate_out)
    sel_ref[...] = sel_out
    gate_ref[...] = gate_out
    cnt_ref[...] = jnp.broadcast_to(carry_ref[0:1, :], cnt_ref.shape)


def _outproj(x2, cb, v, ysb, yfox, conv_w, og, w_out_b, mod3, g2, rw, rb, bd2, ltri, seq):
    t = x2.shape[0]
    tm = TM_PROJ
    tiles_per_seq = seq // tm
    row = lambda i: (i, 0)
    const = lambda i: (0, 0)
    return pl.pallas_call(
        functools.partial(_outproj_kernel, tiles_per_seq),
        grid=(t // tm,),
        in_specs=[pl.BlockSpec((tm, D_MODEL), row),
                  pl.BlockSpec((tm, CONV_CH), row),
                  pl.BlockSpec((tm, CONV_CH), row),
                  pl.BlockSpec((SUBLANES, CONV_CH), lambda i: (jnp.maximum(i * (tm // SUBLANES) - 1, 0), 0)),
                  pl.BlockSpec((tm, SB_DIM), row),
                  pl.BlockSpec((tm, FOX_DIM), row),
                  pl.BlockSpec((CONV_W, CONV_CH), const),
                  pl.BlockSpec((1, D_MIX), const),
                  pl.BlockSpec((D_MIX, D_MODEL), const),
                  pl.BlockSpec((1, N_MOD, D_MODEL), lambda i: (i // tiles_per_seq, 0, 0)),
                  pl.BlockSpec((1, D_MODEL), const),
                  pl.BlockSpec((D_MODEL, 2 * LANES), const),
                  pl.BlockSpec((1, LANES), const),
                  pl.BlockSpec((2 * LANES, 2 * LANES), const),
                  pl.BlockSpec((tm, tm), const)],
        out_specs=[pl.BlockSpec((tm, D_MODEL), row),
                   pl.BlockSpec((tm * SUBLANES, LANES), row),
                   pl.BlockSpec((tm, LANES), row),
                   pl.BlockSpec((tm, LANES), row),
                   pl.BlockSpec((SUBLANES, LANES), const)],
        out_shape=[jax.ShapeDtypeStruct((t, D_MODEL), F32),
                   jax.ShapeDtypeStruct((t * SUBLANES, LANES), F32),
                   jax.ShapeDtypeStruct((t, LANES), jnp.int32),
                   jax.ShapeDtypeStruct((t, LANES), F32),
                   jax.ShapeDtypeStruct((SUBLANES, LANES), F32)],
        scratch_shapes=[pltpu.VMEM((tm + SUBLANES, CONV_CH), F32), pltpu.VMEM((SUBLANES, LANES), F32)],
        compiler_params=_cparams(1),
        name="outproj_router",
    )(x2, cb, v, v, ysb, yfox, conv_w, og, w_out_b, mod3, g2, rw, rb, bd2, ltri)


def _zero_fill(meta_ref, zero_ref, xg_ref, zsem):
    n_tiles = xg_ref.shape[0] // (TM_GROUP * SUBLANES)
    zero_ref[...] = jnp.zeros_like(zero_ref)

    def sweep(wait):
        def per_expert(e, c):
            pos = meta_ref[N_EXPERTS + e]
            n = meta_ref[2 * N_EXPERTS + e]
            bit = TM_GROUP // 2
            while bit:
                @pl.when((n & bit) != 0)
                def _(pos=pos, bit=bit):
                    cp = pltpu.make_async_copy(
                        zero_ref.at[pl.ds(0, bit * SUBLANES), :],
                        xg_ref.at[pl.ds(pl.multiple_of(pos * SUBLANES, SUBLANES), bit * SUBLANES), :], zsem)
                    cp.wait() if wait else cp.start()
                pos = pos + (n & bit)
                bit //= 2
            return c

        lax.fori_loop(0, N_EXPERTS, per_expert, 0)

        def per_tile(j, c):
            rows = TM_GROUP * SUBLANES
            cp = pltpu.make_async_copy(zero_ref, xg_ref.at[pl.ds(pl.multiple_of(j * rows, rows), rows), :], zsem)
            cp.wait() if wait else cp.start()
            return c

        lax.fori_loop(meta_ref[3 * N_EXPERTS], n_tiles, per_tile, 0)

    sweep(False)
    sweep(True)


def _dispatch_kernel(meta_ref, key_ref, h_ref, xg_ref, pos_ref, zero_ref, sem, zsem):
    tm = h_ref.shape[0] // SUBLANES

    @pl.when(pl.program_id(0) == 0)
    def _():
        _zero_fill(meta_ref, zero_ref, xg_ref, zsem)

    def issue(r, c):
        for k in range(TOP_K):
            key = key_ref[r * TOP_K + k]
            p = meta_ref[key >> RANK_BITS] + (key & ((1 << RANK_BITS) - 1))
            pos_ref[r * TOP_K + k] = p
            _row_copy(h_ref, r, xg_ref, p, sem).start()
        return c

    lax.fori_loop(0, tm, issue, 0)
    n = tm * TOP_K * SUBLANES
    pltpu.make_async_copy(xg_ref.at[pl.ds(0, n), :], xg_ref.at[pl.ds(0, n), :], sem).wait()


def _dispatch(meta, keys, h2t, n_rows):
    t = h2t.shape[0] // SUBLANES
    assert t <= 1 << RANK_BITS
    tm = TM_DISPATCH
    idx_spec = pl.BlockSpec((tm * TOP_K,), lambda i, *_: (i,), memory_space=pltpu.SMEM)
    grid_spec = pltpu.PrefetchScalarGridSpec(
        num_scalar_prefetch=1,
        grid=(t // tm,),
        in_specs=[idx_spec, pl.BlockSpec((tm * SUBLANES, LANES), lambda i, *_: (i, 0))],
        out_specs=[pl.BlockSpec(memory_space=pl.ANY), idx_spec],
        scratch_shapes=[pltpu.VMEM((TM_GROUP * SUBLANES, LANES), F32),
                        pltpu.SemaphoreType.DMA(()), pltpu.SemaphoreType.DMA(())],
    )
    return pl.pallas_call(
        _dispatch_kernel,
        grid_spec=grid_spec,
        out_shape=[jax.ShapeDtypeStruct((n_rows * SUBLANES, LANES), F32),
                   jax.ShapeDtypeStruct((t * TOP_K,), jnp.int32)],
        compiler_params=_cparams(1),
        name="moe_dispatch",
    )(meta, keys, h2t)


def _expert_kernel(te_ref, nv_ref, xg_ref, w1_ref, b1_ref, w2_ref, b2_ref, yg_ref, wg_ref, wl_ref, w2b_ref, wt_ref):
    tmg = xg_ref.shape[0] // SUBLANES
    j = pl.program_id(0)
    nv = nv_ref[j]

    @pl.when(jnp.logical_or(j == 0, te_ref[j] != te_ref[jnp.maximum(j - 1, 0)]))
    def _():
        ncol = wt_ref.shape[0] // ROW_TILES
        half = ncol // 2
        for n0 in range(0, 2 * D_EXPERT, ncol):
            wt = w1_ref[0, 0, :, n0:n0 + ncol].T
            for c in range(ROW_TILES):
                wt_ref[pl.ds(c, ncol, stride=ROW_TILES), :] = wt[:, c * LANES:(c + 1) * LANES]
            rows = slice(n0 // 2, n0 // 2 + half)
            for c in range(ROW_TILES):
                cols = slice(c * LANES, (c + 1) * LANES)
                wg_ref[rows, cols] = wt_ref[pl.ds(c, half, stride=2 * ROW_TILES), :].astype(BF16)
                wl_ref[rows, cols] = wt_ref[pl.ds(ROW_TILES + c, half, stride=2 * ROW_TILES), :].astype(BF16)
            w2b_ref[rows, :] = w2_ref[0, 0, rows, :].astype(BF16)

    @pl.when(nv == 0)
    def _():
        yg_ref[...] = jnp.zeros_like(yg_ref)

    @pl.when(nv > 0)
    def _():
        x = jnp.concatenate([xg_ref[pl.ds(c, tmg, stride=SUBLANES), :] for c in range(ROW_TILES)], axis=-1)
        x = x.astype(BF16)
        glu = jnp.minimum(_dot_nt(x, wg_ref[...]) + b1_ref[0, 0:1, :], SWIGLU_LIMIT)
        lin = jnp.clip(_dot_nt(x, wl_ref[...]) + b1_ref[0, 1:2, :], -SWIGLU_LIMIT, SWIGLU_LIMIT)
        a = glu * (1.0 / (1.0 + jnp.exp(-SWIGLU_ALPHA * glu))) * (lin + 1.0)
        y = _dot(a.astype(BF16), w2b_ref[...]) + b2_ref[0]
        for c in range(ROW_TILES):
            yg_ref[pl.ds(c, tmg, stride=SUBLANES), :] = y[:, c * LANES:(c + 1) * LANES]


def _experts(te, nv, xg, layer, w1, b1, w2, b2):
    tmg = TM_GROUP
    n_tiles = xg.shape[0] // (tmg * SUBLANES)
    grid_spec = pltpu.PrefetchScalarGridSpec(
        num_scalar_prefetch=2,
        grid=(n_tiles,),
        in_specs=[pl.BlockSpec((tmg * SUBLANES, LANES), lambda j, te, nv: (j, 0)),
                  pl.BlockSpec((1, 1, D_MODEL, 2 * D_EXPERT), lambda j, te, nv: (layer, te[j], 0, 0)),
                  pl.BlockSpec((1, 2, D_EXPERT), lambda j, te, nv: (te[j], 0, 0)),
                  pl.BlockSpec((1, 1, D_EXPERT, D_MODEL), lambda j, te, nv: (layer, te[j], 0, 0)),
                  pl.BlockSpec((1, 1, D_MODEL), lambda j, te, nv: (te[j], 0, 0))],
        out_specs=pl.BlockSpec((tmg * SUBLANES, LANES), lambda j, te, nv: (j, 0)),
        scratch_shapes=[pltpu.VMEM((D_EXPERT, D_MODEL), BF16), pltpu.VMEM((D_EXPERT, D_MODEL), BF16),
                        pltpu.VMEM((D_EXPERT, D_MODEL), BF16), pltpu.VMEM((2 * LANES * ROW_TILES, LANES), F32)],
    )
    return pl.pallas_call(
        _expert_kernel,
        grid_spec=grid_spec,
        out_shape=jax.ShapeDtypeStruct(xg.shape, F32),
        compiler_params=_cparams(1, EXPERT_VMEM_LIMIT),
        name="moe_experts",
    )(te, nv, xg, w1, b1, w2, b2)


def _combine_kernel(pos_ref, pos_next_ref, gate_ref, x1_ref, mod_ref, yg_ref, o_ref, buf_ref, sem):
    tm = x1_ref.shape[0]
    i = pl.program_id(0)
    slot = i % 2

    @pl.when(i == 0)
    def _():
        _gather_rows(yg_ref, pos_ref, buf_ref, sem, 0, tm)

    @pl.when(i + 1 < pl.num_programs(0))
    def _():
        _gather_rows(yg_ref, pos_next_ref, buf_ref, sem, 1 - slot, tm)

    _wait_rows(buf_ref, sem, slot)
    _combine_rows(gate_ref, buf_ref, slot, tm, x1_ref, mod_ref[0, 5:6, :], o_ref)


def _combine(pos, gates, x1, mod3, yg, seq):
    t = x1.shape[0]
    tm = TM_COMBINE
    tiles_per_seq = seq // tm
    n = t // tm
    grid_spec = pltpu.PrefetchScalarGridSpec(
        num_scalar_prefetch=0,
        grid=(n,),
        in_specs=[pl.BlockSpec((tm * TOP_K,), lambda i: (i,), memory_space=pltpu.SMEM),
                  pl.BlockSpec((tm * TOP_K,), lambda i: (jnp.minimum(i + 1, n - 1),), memory_space=pltpu.SMEM),
                  pl.BlockSpec((tm, LANES), lambda i: (i, 0)),
                  pl.BlockSpec((tm, D_MODEL), lambda i: (i, 0)),
                  pl.BlockSpec((1, N_MOD, D_MODEL), lambda i: (i // tiles_per_seq, 0, 0)),
                  pl.BlockSpec(memory_space=pl.ANY)],
        out_specs=pl.BlockSpec((tm, D_MODEL), lambda i: (i, 0)),
        scratch_shapes=[pltpu.VMEM((2, TOP_K, tm * SUBLANES, LANES), F32), pltpu.SemaphoreType.DMA((2,))],
    )
    return pl.pallas_call(
        _combine_kernel,
        grid_spec=grid_spec,
        out_shape=jax.ShapeDtypeStruct((t, D_MODEL), F32),
        compiler_params=_cparams(1),
        name="moe_combine",
    )(pos, pos, gates, x1, mod3, yg)


def _block_diag_ones(n):
    i = np.arange(n)
    return jnp.asarray((i[:, None] // HEAD_DIM) == (i[None, :] // HEAD_DIM), dtype=BF16)


def _tri(n, strict, upper):
    i = np.arange(n)
    if upper:
        m = (i[:, None] > i[None, :]) if strict else (i[:, None] >= i[None, :])
        return jnp.asarray(m, dtype=BF16)
    m = (i[None, :] < i[:, None]) if strict else (i[None, :] <= i[:, None])
    return jnp.asarray(m, dtype=BF16)


def _tile2(g):
    return jnp.tile(g, SB_DIM // HEAD_DIM)


def _layer(x_parts, c, seq, layer, last, w_mlp1, w_mlp2, norm1_g, w_ada, b_ada, w_in, conv_w, sb_q_g, sb_k_g, fox_q_g, fox_k_g, fox_f_b,
           out_norm_g, w_out, norm2_g, router_w, router_b, b_mlp1, b_mlp2):
    t = x_parts[0].shape[0] if len(x_parts) == 1 else x_parts[2].shape[0]
    nb = t // seq
    scale = HEAD_DIM ** -0.5

    w_in_p = jnp.pad(w_in, ((0, 0), (0, D_IN_PAD - w_in.shape[1]))).astype(BF16)
    fb = jnp.pad(fox_f_b, (0, LANES - FOX_HEADS)).reshape(1, LANES)
    qscale = scale * LOG2_E
    gains = jnp.stack([_tile2(sb_q_g) * qscale, _tile2(sb_k_g), _tile2(fox_q_g) * qscale, _tile2(fox_k_g)])
    gains = jnp.pad(gains, ((0, SUBLANES - 4), (0, 0)))
    rw = jnp.pad(router_w, ((0, 0), (0, LANES - N_EXPERTS)))
    rw_hi = rw.astype(BF16)
    rw = jnp.concatenate([rw_hi, (rw - rw_hi.astype(F32)).astype(BF16)], axis=1)
    rb = jnp.pad(router_b, (0, LANES - N_EXPERTS)).reshape(1, LANES)
    b1 = jnp.stack([b_mlp1[:, 0::2], b_mlp1[:, 1::2]], axis=1)
    b2 = b_mlp2.reshape(N_EXPERTS, 1, D_MODEL)

    mod3 = _modulation(c, w_ada, b_ada).reshape(nb, N_MOD, D_MODEL)

    outs = _inproj(x_parts, mod3, norm1_g.reshape(1, D_MODEL), w_in_p, _block_diag_ones(SB_DIM),
                   _tri(TM_PROJ, strict=False, upper=False), fb, gains, seq)
    x2 = x_parts[0] if len(x_parts) == 1 else outs[0]
    cb, v, sbq, sbk, sbv, fq, fk, fv, cum = outs[-9:]

    r3 = lambda a: a.reshape(nb, seq, SB_DIM)
    u = _tri(TB, strict=True, upper=True)
    zu = jnp.zeros_like(u)
    uu = jnp.concatenate([jnp.concatenate([u, zu], axis=1), jnp.concatenate([zu, u], axis=1)], axis=0)
    ysb = _sb_attention(r3(sbq), r3(sbk), r3(sbv), uu)

    n_kb = seq // TB
    f2 = cum[:, :FOX_HEADS, :].reshape(nb, FOX_DIM // LANES, 2, n_kb, TB).transpose(0, 1, 3, 2, 4)
    f2 = f2.reshape(nb, FOX_DIM // LANES, n_kb, 2 * TB)
    fend = cum[:, :FOX_HEADS, TB - 1::TB].reshape(nb * FOX_HEADS * n_kb)
    zbound = (2.0 * 1.02 * HEAD_DIM * qscale * jnp.max(jnp.abs(fox_q_g)) * jnp.max(jnp.abs(fox_k_g))).reshape(1)
    yfox = _fox_attention(r3(fq), r3(fk), r3(fv), f2, fend, zbound)

    x1, h2t, sel, gates, cnt = _outproj(
        x2, cb, v, ysb.reshape(t, SB_DIM), yfox.reshape(t, FOX_DIM), conv_w, out_norm_g.reshape(1, D_MIX),
        w_out.astype(BF16), mod3, norm2_g.reshape(1, D_MODEL), rw, rb, _block_diag_ones(2 * LANES),
        _tri(TM_PROJ, strict=True, upper=False), seq)

    counts = cnt[0, :N_EXPERTS].astype(jnp.int32)
    padded = ((counts + TM_GROUP - 1) // TM_GROUP) * TM_GROUP
    off_end = jnp.cumsum(padded)
    off = off_end - padded
    n_rows = t * TOP_K + N_EXPERTS * TM_GROUP
    n_tiles = n_rows // TM_GROUP
    start = jnp.arange(n_tiles, dtype=jnp.int32) * TM_GROUP
    te = jnp.minimum(jnp.sum(start[:, None] >= off_end[None, :], axis=1), N_EXPERTS - 1).astype(jnp.int32)
    nv = jnp.clip(off[te] + counts[te] - start, 0, TM_GROUP).astype(jnp.int32)
    meta = jnp.concatenate([off, off + counts, padded - counts, off_end[-1:] // TM_GROUP]).astype(jnp.int32)

    xg, pos = _dispatch(meta, sel[:, :TOP_K].reshape(t * TOP_K), h2t, n_rows)
    yg = _experts(te, nv, xg, layer, w_mlp1, b1, w_mlp2, b2)
    return _combine(pos, gates, x1, mod3, yg, seq) if last else (pos, gates, x1, mod3, yg)


def kernel(x, c, norm1_g, w_ada, b_ada, w_in, conv_w, sb_q_g, sb_k_g, fox_q_g, fox_k_g, fox_f_b, out_norm_g, w_out,
           norm2_g, router_w, router_b, w_mlp1, b_mlp1, w_mlp2, b_mlp2):
    b, s, d = x.shape
    x_parts = (x.reshape(b * s, d),)
    params = (norm1_g, w_ada, b_ada, w_in, conv_w, sb_q_g, sb_k_g, fox_q_g, fox_k_g, fox_f_b, out_norm_g, w_out,
              norm2_g, router_w, router_b, b_mlp1, b_mlp2)
    depth = norm1_g.shape[0]
    for layer in range(depth):
        x_parts = _layer(x_parts, c, s, layer, layer == depth - 1, w_mlp1, w_mlp2, *(p[layer] for p in params))
    return x_parts.reshape(b, s, d)
```

```python
import functools

import numpy as np
import jax
import jax.numpy as jnp
from jax import lax
from jax.experimental import pallas as pl
from jax.experimental.pallas import tpu as pltpu

F32 = jnp.float32
BF16 = jnp.bfloat16

D_MODEL = 1024
HEAD_DIM = 64
CONV_CH = 256
CONV_W = 3
SB_DIM = 384
FOX_DIM = 384
FOX_HEADS = 6
N_PAIRS = 3
D_MIX = 1024
N_MOD = 6
N_EXPERTS = 32
TOP_K = 4
D_EXPERT = 1024
SWIGLU_ALPHA = 1.702
SWIGLU_LIMIT = 7.0
EPS = 1e-6

LANES = 128
SUBLANES = 8
ROW_TILES = D_MODEL // LANES
D_IN_PAD = 3 * CONV_CH + 3 * SB_DIM + 3 * FOX_DIM + LANES
COL_SBQ = 3 * CONV_CH
COL_SBK = COL_SBQ + SB_DIM
COL_SBV = COL_SBK + SB_DIM
COL_FQ = COL_SBV + SB_DIM
COL_FK = COL_FQ + FOX_DIM
COL_FV = COL_FK + FOX_DIM
COL_FLOG = COL_FV + FOX_DIM

TM_PROJ = 512
TQ = 256
TB = 128
TM_DISPATCH = 1024
TM_COMBINE = 256
TM_GROUP = 512
RANK_BITS = 20
LOG2_E = 1.4426950408889634
LOG2_NEGLIGIBLE = -40.0
VMEM_LIMIT = 48 * 1024 * 1024
INPROJ_VMEM_LIMIT = 56 * 1024 * 1024
EXPERT_VMEM_LIMIT = 58 * 1024 * 1024


def _cparams(n_axes, vmem=VMEM_LIMIT):
    return pltpu.CompilerParams(dimension_semantics=("arbitrary",) * n_axes, vmem_limit_bytes=vmem)


def _dot(a, b):
    return jnp.dot(a, b, preferred_element_type=F32)


def _dot_nt(a, b):
    return lax.dot_general(a, b, (((1,), (1,)), ((), ())), preferred_element_type=F32)


def _split2(x):
    hi = x.astype(BF16)
    lo = (x - hi.astype(F32)).astype(BF16)
    return hi, lo


def _row_copy(src, src_row, dst, dst_row, sem):
    return pltpu.make_async_copy(src.at[pl.ds(pl.multiple_of(src_row * SUBLANES, SUBLANES), SUBLANES), :],
                                 dst.at[pl.ds(pl.multiple_of(dst_row * SUBLANES, SUBLANES), SUBLANES), :], sem)


def _log_sigmoid(x):
    return jnp.minimum(x, 0.0) - jnp.log(1.0 + jnp.exp(-jnp.abs(x)))


def _mod_kernel(c_ref, w_ref, b_ref, o_ref):
    c = c_ref[...]
    s = c / (1.0 + jnp.exp(-c))
    o_ref[...] = _dot(s.astype(BF16), w_ref[...].astype(BF16)) + b_ref[...]


def _modulation(c, w_ada, b_ada):
    b, d = c.shape
    n = w_ada.shape[1]
    tn = 1024
    return pl.pallas_call(
        _mod_kernel,
        grid=(n // tn,),
        in_specs=[pl.BlockSpec((b, d), lambda j: (0, 0)),
                  pl.BlockSpec((d, tn), lambda j: (0, j)),
                  pl.BlockSpec((1, tn), lambda j: (0, j))],
        out_specs=pl.BlockSpec((b, tn), lambda j: (0, j)),
        out_shape=jax.ShapeDtypeStruct((b, n), F32),
        compiler_params=_cparams(1),
        name="modulation",
    )(c, w_ada, b_ada.reshape(1, n))


def _gather_rows(yg_ref, idx_ref, buf_ref, sem, slot, n_rows):
    def issue(r, c):
        for k in range(TOP_K):
            _row_copy(yg_ref, idx_ref[r * TOP_K + k], buf_ref.at[slot, k], r, sem.at[slot]).start()
        return c

    lax.fori_loop(0, n_rows, issue, 0)


def _combine_rows(gate_ref, buf_ref, slot, n_rows, x1_ref, g2, out_ref):
    gates = gate_ref[...]
    gate_cols = [gates[:, k:k + 1] for k in range(TOP_K)]
    for c in range(ROW_TILES):
        acc = jnp.zeros((n_rows, LANES), F32)
        for k in range(TOP_K):
            acc = acc + gate_cols[k] * buf_ref[slot, k, pl.ds(c, n_rows, stride=SUBLANES), :]
        cols = slice(c * LANES, (c + 1) * LANES)
        out_ref[:, cols] = x1_ref[:, cols] + g2[:, cols] * acc


def _wait_rows(buf_ref, sem, slot):
    for k in range(TOP_K):
        pltpu.make_async_copy(buf_ref.at[slot, k], buf_ref.at[slot, k], sem.at[slot]).wait()


def _inproj_kernel(tiles_per_seq, fused, *refs):
    if fused:
        (pos_ref, pos_next_ref, gate_ref, x1_ref, modp_ref, yg_ref, mod_ref, g1_ref, w_ref, bd_ref, tri_ref, fb_ref,
         gains_ref, x_out_ref, cb_ref, v_ref, sbq_ref, sbk_ref, sbv_ref, fq_ref, fk_ref, fv_ref, cum_ref,
         carry_ref, x_ref, buf_ref, sem) = refs
        tm = x1_ref.shape[0]
        i = pl.program_id(0)
        slot = i % 2

        @pl.when(i == 0)
        def _():
            _gather_rows(yg_ref, pos_ref, buf_ref, sem, 0, tm)

        @pl.when(i + 1 < pl.num_programs(0))
        def _():
            _gather_rows(yg_ref, pos_next_ref, buf_ref, sem, 1 - slot, tm)

        _wait_rows(buf_ref, sem, slot)
        _combine_rows(gate_ref, buf_ref, slot, tm, x1_ref, modp_ref[0, 5:6, :], x_ref)
        x_out_ref[...] = x_ref[...]
    else:
        (x_ref, mod_ref, g1_ref, w_ref, bd_ref, tri_ref, fb_ref, gains_ref,
         cb_ref, v_ref, sbq_ref, sbk_ref, sbv_ref, fq_ref, fk_ref, fv_ref, cum_ref, carry_ref) = refs
        tm = x_ref.shape[0]
    x = x_ref[...]
    ms = jnp.mean(x * x, axis=-1, keepdims=True)
    h = (x * lax.rsqrt(ms + EPS)) * g1_ref[...]
    h = h * (1.0 + mod_ref[0, 1:2, :]) + mod_ref[0, 0:1, :]
    hb = h.astype(BF16)

    pc = _dot(hb, w_ref[:, 0:3 * CONV_CH])
    cb_ref[...] = pc[:, 0:CONV_CH]
    v_ref[...] = pc[:, CONV_CH:2 * CONV_CH] * pc[:, 2 * CONV_CH:3 * CONV_CH]

    normed = ((COL_SBQ, sbq_ref), (COL_SBK, sbk_ref), (COL_FQ, fq_ref), (COL_FK, fk_ref))
    proj = [_dot(hb, w_ref[:, col:col + SB_DIM]) for col, _ in normed]
    sbv_ref[...] = _dot(hb, w_ref[:, COL_SBV:COL_SBV + SB_DIM]).astype(BF16)
    fv_ref[...] = _dot(hb, w_ref[:, COL_FV:COL_FV + FOX_DIM]).astype(BF16)
    flog = _dot(hb, w_ref[:, COL_FLOG:COL_FLOG + LANES])
    sumsq = [_dot((p * p).astype(BF16), bd_ref[...]) for p in proj]
    for row, ((_, out_ref), p, ss) in enumerate(zip(normed, proj, sumsq)):
        out_ref[...] = (p * lax.rsqrt(ss * (1.0 / HEAD_DIM) + EPS) * gains_ref[row:row + 1, :]).astype(BF16)

    lf = _log_sigmoid(flog + fb_ref[...]) * LOG2_E
    t0 = lf.astype(BF16)
    r1 = lf - t0.astype(F32)
    t1 = r1.astype(BF16)
    t2 = (r1 - t1.astype(F32)).astype(BF16)
    tri = tri_ref[...]
    cs = _dot(tri, t0) + _dot(tri, t1) + _dot(tri, t2)

    @pl.when(pl.program_id(0) % tiles_per_seq == 0)
    def _():
        carry_ref[...] = jnp.zeros_like(carry_ref)

    cum = cs + carry_ref[0:1, :]
    carry_ref[0:1, :] = cum[tm - 1:tm, :]
    cum_ref[0] = cum.T[0:SUBLANES, :]


def _inproj(x_parts, mod3, g1, w_in_p, bd, tri, fb, gains, seq):
    fused = len(x_parts) > 1
    t = x_parts[2].shape[0] if fused else x_parts[0].shape[0]
    tm = TM_PROJ
    n = t // tm
    tiles_per_seq = seq // tm
    nb = t // seq
    row = lambda i: (i, 0)
    const = lambda i: (0, 0)
    per_seq = lambda i: (i // tiles_per_seq, 0, 0)
    outs = [jax.ShapeDtypeStruct((t, CONV_CH), F32), jax.ShapeDtypeStruct((t, CONV_CH), F32)]
    outs += [jax.ShapeDtypeStruct((t, SB_DIM), BF16)] * 6
    outs += [jax.ShapeDtypeStruct((nb, SUBLANES, seq), F32)]
    out_specs = [pl.BlockSpec((tm, CONV_CH), row)] * 2 + [pl.BlockSpec((tm, SB_DIM), row)] * 6
    out_specs += [pl.BlockSpec((1, SUBLANES, tm), lambda i: (i // tiles_per_seq, 0, i % tiles_per_seq))]
    in_specs = [pl.BlockSpec((1, N_MOD, D_MODEL), per_seq),
                pl.BlockSpec((1, D_MODEL), const),
                pl.BlockSpec((D_MODEL, D_IN_PAD), const),
                pl.BlockSpec((SB_DIM, SB_DIM), const),
                pl.BlockSpec((tm, tm), const),
                pl.BlockSpec((1, LANES), const),
                pl.BlockSpec((SUBLANES, SB_DIM), const)]
    scratch = [pltpu.VMEM((SUBLANES, LANES), F32)]
    if fused:
        pos, gates, x1, mod3_prev, yg = x_parts
        operands = (pos, pos, gates, x1, mod3_prev, yg)
        in_specs = [pl.BlockSpec((tm * TOP_K,), lambda i: (i,), memory_space=pltpu.SMEM),
                    pl.BlockSpec((tm * TOP_K,), lambda i: (jnp.minimum(i + 1, n - 1),), memory_space=pltpu.SMEM),
                    pl.BlockSpec((tm, LANES), row),
                    pl.BlockSpec((tm, D_MODEL), row),
                    pl.BlockSpec((1, N_MOD, D_MODEL), per_seq),
                    pl.BlockSpec(memory_space=pl.ANY)] + in_specs
        outs = [jax.ShapeDtypeStruct((t, D_MODEL), F32)] + outs
        out_specs = [pl.BlockSpec((tm, D_MODEL), row)] + out_specs
        scratch += [pltpu.VMEM((tm, D_MODEL), F32), pltpu.VMEM((2, TOP_K, tm * SUBLANES, LANES), F32),
                    pltpu.SemaphoreType.DMA((2,))]
    else:
        operands = x_parts
        in_specs = [pl.BlockSpec((tm, D_MODEL), row)] + in_specs
    return pl.pallas_call(
        functools.partial(_inproj_kernel, tiles_per_seq, fused),
        grid=(n,),
        in_specs=in_specs,
        out_specs=out_specs,
        out_shape=outs,
        scratch_shapes=scratch,
        compiler_params=_cparams(1, INPROJ_VMEM_LIMIT),
        name="inproj",
    )(*operands, mod3, g1, w_in_p, bd, tri, fb, gains)


def _stack_heads(k_ref, v_ref, kk_ref, vv_ref):
    n_blocks = kk_ref.shape[1]
    lane = lax.broadcasted_iota(jnp.int32, (TB, LANES), 1)
    first = lane < HEAD_DIM

    def body(j, c):
        rows = pl.ds(pl.multiple_of(j * TB, TB), TB)
        for src, dst in ((k_ref, kk_ref), (v_ref, vv_ref)):
            for p in range(N_PAIRS):
                blk = src[0, rows, p * LANES:(p + 1) * LANES]
                zero = jnp.zeros_like(blk)
                dst[p, j, 0:TB, :] = jnp.where(first, blk, zero)
                dst[p, j, TB:2 * TB, :] = jnp.where(first, zero, blk)
        return c

    lax.fori_loop(0, n_blocks, body, 0)


def _per_head(a0, a1):
    return jnp.concatenate([jnp.broadcast_to(a0, (TB, TB)), jnp.broadcast_to(a1, (TB, TB))], axis=1)


def _pair_iota():
    row = lax.broadcasted_iota(jnp.int32, (TB, 2 * TB), 0)
    col = lax.broadcasted_iota(jnp.int32, (TB, 2 * TB), 1) % TB
    return row, col


_CHAINS = [(p, s) for p in range(N_PAIRS) for s in range(2)]


def _q_block(q_ref, p, s):
    return q_ref[0, s * TB:(s + 1) * TB, p * LANES:(p + 1) * LANES]


def _sb_kernel(q_ref, k_ref, v_ref, uu_ref, o_ref, kk_ref, vv_ref):
    qi = pl.program_id(1)

    @pl.when(qi == 0)
    def _():
        _stack_heads(k_ref, v_ref, kk_ref, vv_ref)

    uu = uu_ref[...]
    row, col = _pair_iota()
    strict = col < row

    def sweep(states, backs):
        items = [(c, b) for b in range(len(backs)) for c in range(len(_CHAINS))]
        kbs, keeps, z = {}, {}, {}
        for c, b in items:
            p, s = _CHAINS[c]
            back = backs[b]
            kb = 2 * qi + s - back
            if isinstance(back, int) and back == 0:
                keeps[c, b] = strict
            elif isinstance(back, int) and back == 1 and s == 1:
                keeps[c, b] = None
            else:
                keeps[c, b] = kb >= 0
            kbs[c, b] = jnp.maximum(kb, 0)
            z[c, b] = _dot_nt(_q_block(q_ref, p, s), kk_ref[p, kbs[c, b]])
        log_beta, log_1mb, sums = {}, {}, {}
        for it in items:
            log_beta[it] = jnp.minimum(z[it], 0.0) - jnp.log2(1.0 + jnp.exp2(-jnp.abs(z[it])))
            l1 = log_beta[it] - z[it]
            log_1mb[it] = l1 if keeps[it] is None else jnp.where(keeps[it], l1, 0.0)
            sums[it] = (jnp.sum(log_1mb[it][:, :TB], axis=-1, keepdims=True),
                        jnp.sum(log_1mb[it][:, TB:], axis=-1, keepdims=True))
        later = {}
        for it in items:
            hi, lo = _split2(log_1mb[it])
            later[it] = _dot(hi, uu) + _dot(lo, uu)
        states = list(states)
        for it in items:
            c = it[0]
            rs0, rs1, acc = states[c]
            w = jnp.exp2(log_beta[it] + later[it] + _per_head(rs0, rs1))
            if keeps[it] is not None:
                w = jnp.where(keeps[it], w, 0.0)
            acc = acc + _dot(w.astype(BF16), vv_ref[_CHAINS[c][0], kbs[it]])
            states[c] = (rs0 + sums[it][0], rs1 + sums[it][1], acc)
        return states

    def more(states):
        top = states[0][0]
        for st in states:
            top = jnp.maximum(top, jnp.maximum(st[0], st[1]))
        return (jnp.max(top) > LOG2_NEGLIGIBLE).astype(jnp.int32)

    zero = jnp.zeros((TB, 1), F32)
    states = sweep([(zero, zero, jnp.zeros((TB, LANES), F32))] * len(_CHAINS), (0, 1))

    def cond(c):
        return jnp.logical_and(2 * qi + 1 - c[0] >= 0, c[2] > 0)

    def body(c):
        back, states, _ = c
        states = sweep(states, (back,))
        return back + 1, states, more(states)

    _, states, _ = lax.while_loop(cond, body, (jnp.int32(2), states, more(states)))
    for (p, s), st in zip(_CHAINS, states):
        o_ref[0, s * TB:(s + 1) * TB, p * LANES:(p + 1) * LANES] = st[2]


def _sb_attention(q, k, v, uu):
    b, s, d = q.shape
    tq = TQ
    qspec = pl.BlockSpec((1, tq, d), lambda bi, qi: (bi, qi, 0))
    kvspec = pl.BlockSpec((1, s, d), lambda bi, qi: (bi, 0, 0))
    scratch = pltpu.VMEM((N_PAIRS, s // TB, 2 * TB, LANES), BF16)
    return pl.pallas_call(
        _sb_kernel,
        grid=(b, s // tq),
        in_specs=[qspec, kvspec, kvspec, pl.BlockSpec((2 * TB, 2 * TB), lambda bi, qi: (0, 0))],
        out_specs=qspec,
        out_shape=jax.ShapeDtypeStruct((b, s, d), F32),
        scratch_shapes=[scratch, scratch],
        compiler_params=_cparams(2),
        name="sb_attention",
    )(q, k, v, uu)


def _fox_kernel(n_kb, fend_ref, zb_ref, q_ref, k_ref, v_ref, f2_ref, o_ref, kk_ref, vv_ref, sc_ref):
    bi = pl.program_id(0)
    qi = pl.program_id(1)

    @pl.when(qi == 0)
    def _():
        _stack_heads(k_ref, v_ref, kk_ref, vv_ref)

    row, col = _pair_iota()
    causal = col <= row
    lane = lax.broadcasted_iota(jnp.int32, (TB, LANES), 1)
    first = lane < HEAD_DIM
    zbound = zb_ref[0]

    def first_block(head, blk):
        base = (bi * FOX_HEADS + head) * n_kb
        f_q = fend_ref[base + jnp.maximum(blk - 1, 0)]

        def keep(kb):
            gap = zbound + f_q - fend_ref[base + jnp.maximum(kb - 1, 0)]
            return jnp.logical_and(kb > 0, gap >= LOG2_NEGLIGIBLE)

        return lax.while_loop(keep, lambda kb: kb - 1, blk)

    def scores(p, s, back, steps):
        kb = 2 * qi + s - back
        sc = _dot_nt(_q_block(q_ref, p, s), kk_ref[p, jnp.maximum(kb, 0)]) - f2_ref[0, p, pl.ds(jnp.maximum(kb, 0), 1), :]
        return jnp.where(jnp.logical_and(kb >= 0, back <= steps), sc, -1e30)

    def head_max(a):
        return (jnp.max(a[:, :TB], axis=-1, keepdims=True), jnp.max(a[:, TB:], axis=-1, keepdims=True))

    def head_sum(a):
        return (jnp.sum(a[:, :TB], axis=-1, keepdims=True), jnp.sum(a[:, TB:], axis=-1, keepdims=True))

    for p in range(N_PAIRS):
        steps = jnp.int32(0)
        for s in range(2):
            for h in range(2):
                steps = jnp.maximum(steps, 2 * qi + s - first_block(2 * p + h, 2 * qi + s))
        n_it = (steps + 1) // 2

        runmax = []
        for s in range(2):
            sc = jnp.where(causal, scores(p, s, 0, steps), -1e30)
            sc_ref[s, 0] = sc
            runmax.append(sc)

        def pass1(it, runmax):
            out = list(runmax)
            for u in (1, 2):
                back = 2 * it + u
                for s in range(2):
                    sc = scores(p, s, back, steps)
                    sc_ref[s, back] = sc
                    out[s] = jnp.maximum(out[s], sc)
            return out

        runmax = lax.fori_loop(0, n_it, pass1, runmax)
        tops = [_per_head(*head_max(runmax[s])) for s in range(2)]

        def weigh(s, back, runsum, acc):
            pr = jnp.exp2(sc_ref[s, back] - tops[s])
            kb = jnp.maximum(2 * qi + s - back, 0)
            return runsum + pr, acc + _dot(pr.astype(BF16), vv_ref[p, kb])

        state = [weigh(s, 0, jnp.zeros((TB, 2 * TB), F32), jnp.zeros((TB, LANES), F32)) for s in range(2)]

        def pass2(it, state):
            out = list(state)
            for u in (1, 2):
                for s in range(2):
                    out[s] = weigh(s, 2 * it + u, *out[s])
            return out

        state = lax.fori_loop(0, n_it, pass2, state)
        for s in range(2):
            l0, l1 = head_sum(state[s][0])
            o_ref[0, s * TB:(s + 1) * TB, p * LANES:(p + 1) * LANES] = state[s][1] / jnp.where(first, l0, l1)


def _fox_attention(q, k, v, f2, fend, zbound):
    b, s, d = q.shape
    tq = TQ
    n_kb = s // TB
    qspec = pl.BlockSpec((1, tq, d), lambda bi, qi, *_: (bi, qi, 0))
    kvspec = pl.BlockSpec((1, s, d), lambda bi, qi, *_: (bi, 0, 0))
    scratch = pltpu.VMEM((N_PAIRS, n_kb, 2 * TB, LANES), BF16)
    grid_spec = pltpu.PrefetchScalarGridSpec(
        num_scalar_prefetch=2,
        grid=(b, s // tq),
        in_specs=[qspec, kvspec, kvspec,
                  pl.BlockSpec((1, N_PAIRS, n_kb, 2 * TB), lambda bi, qi, *_: (bi, 0, 0, 0))],
        out_specs=qspec,
        scratch_shapes=[scratch, scratch, pltpu.VMEM((2, n_kb + 2, TB, 2 * TB), F32)],
    )
    return pl.pallas_call(
        functools.partial(_fox_kernel, n_kb),
        grid_spec=grid_spec,
        out_shape=jax.ShapeDtypeStruct((b, s, d), F32),
        compiler_params=_cparams(2),
        name="fox_attention",
    )(fend, zbound, q, k, v, f2)


def _outproj_kernel(tiles_per_seq, x_ref, cb_ref, v_ref, vprev_ref, ysb_ref, yfox_ref, cw_ref, og_ref, wout_ref,
                    mod_ref, g2_ref, rw_ref, rb_ref, bd_ref, ltri_ref,
                    x1_ref, h2t_ref, sel_ref, gate_ref, cnt_ref, vext_ref, carry_ref):
    tm = x_ref.shape[0]
    i = pl.program_id(0)

    @pl.when(i == 0)
    def _():
        carry_ref[...] = jnp.zeros_like(carry_ref)

    first = (i % tiles_per_seq) == 0
    vext_ref[0:SUBLANES, :] = jnp.where(first, 0.0, vprev_ref[...])
    vext_ref[SUBLANES:SUBLANES + tm, :] = v_ref[...]
    conv = (cw_ref[0:1, :] * vext_ref[SUBLANES - 2:SUBLANES - 2 + tm, :]
            + cw_ref[1:2, :] * vext_ref[SUBLANES - 1:SUBLANES - 1 + tm, :]
            + cw_ref[2:3, :] * v_ref[...])
    y = jnp.concatenate([cb_ref[...] * conv, ysb_ref[...], yfox_ref[...]], axis=-1)

    ysq = (y * y).astype(BF16)
    bd = bd_ref[...]
    ss = jnp.concatenate([_dot(ysq[:, c:c + 2 * LANES], bd) for c in range(0, D_MIX, 2 * LANES)], axis=-1)
    yn = y * lax.rsqrt(ss * (1.0 / HEAD_DIM) + EPS) * og_ref[...]
    x1 = x_ref[...] + mod_ref[0, 2:3, :] * _dot(yn.astype(BF16), wout_ref[...])
    x1_ref[...] = x1

    ms = jnp.mean(x1 * x1, axis=-1, keepdims=True)
    h2 = (x1 * lax.rsqrt(ms + EPS)) * g2_ref[...]
    h2 = h2 * (1.0 + mod_ref[0, 4:5, :]) + mod_ref[0, 3:4, :]
    for j in range(ROW_TILES):
        h2t_ref[pl.ds(j, tm, stride=SUBLANES), :] = h2[:, j * LANES:(j + 1) * LANES]

    h_hi, h_lo = _split2(h2)
    both = _dot(h_hi, rw_ref[...])
    logits = both[:, :LANES] + both[:, LANES:] + _dot(h_lo, rw_ref[:, :LANES]) + rb_ref[...]
    lane = lax.broadcasted_iota(jnp.int32, (tm, LANES), 1)
    lane_f = lane.astype(F32)
    neg = jnp.float32(-jnp.inf)
    cur = jnp.where(lane < N_EXPERTS, logits, neg)
    tops, sels = [], []
    for _ in range(TOP_K):
        mk = jnp.max(cur, axis=-1, keepdims=True)
        ik = jnp.min(jnp.where(cur == mk, lane_f, float(LANES)), axis=-1, keepdims=True)
        sel = lane_f == ik
        cur = jnp.where(sel, neg, cur)
        tops.append(mk)
        sels.append((sel, ik.astype(jnp.int32)))
    es = [jnp.exp(t - tops[0]) for t in tops]
    inv = 1.0 / (es[0] + es[1] + es[2] + es[3])

    multi = jnp.zeros((tm, LANES), F32)
    for sel, _ in sels:
        multi = jnp.where(sel, 1.0, multi)
    before = _dot(ltri_ref[...], multi.astype(BF16)) + carry_ref[0:1, :]
    carry_ref[0:1, :] = before[tm - 1:tm, :] + multi[tm - 1:tm, :]

    sel_out = jnp.zeros((tm, LANES), jnp.int32)
    gate_out = jnp.zeros((tm, LANES), F32)
    for k, (sel, ik) in enumerate(sels):
        rank = jnp.sum(jnp.where(sel, before, 0.0), axis=-1, keepdims=True).astype(jnp.int32)
        sel_out = jnp.where(lane == k, ik * (1 << RANK_BITS) + rank, sel_out)
        gate_out = jnp.where(lane == k, es[k] * inv, gate_out)
    sel_ref[...] = sel_out
    gate_ref[...] = gate_out
    cnt_ref[...] = jnp.broadcast_to(carry_ref[0:1, :], cnt_ref.shape)


def _outproj(x2, cb, v, ysb, yfox, conv_w, og, w_out_b, mod3, g2, rw, rb, bd2, ltri, seq):
    t = x2.shape[0]
    tm = TM_PROJ
    tiles_per_seq = seq // tm
    row = lambda i: (i, 0)
    const = lambda i: (0, 0)
    return pl.pallas_call(
        functools.partial(_outproj_kernel, tiles_per_seq),
        grid=(t // tm,),
        in_specs=[pl.BlockSpec((tm, D_MODEL), row),
                  pl.BlockSpec((tm, CONV_CH), row),
                  pl.BlockSpec((tm, CONV_CH), row),
                  pl.BlockSpec((SUBLANES, CONV_CH), lambda i: (jnp.maximum(i * (tm // SUBLANES) - 1, 0), 0)),
                  pl.BlockSpec((tm, SB_DIM), row),
                  pl.BlockSpec((tm, FOX_DIM), row),
                  pl.BlockSpec((CONV_W, CONV_CH), const),
                  pl.BlockSpec((1, D_MIX), const),
                  pl.BlockSpec((D_MIX, D_MODEL), const),
                  pl.BlockSpec((1, N_MOD, D_MODEL), lambda i: (i // tiles_per_seq, 0, 0)),
                  pl.BlockSpec((1, D_MODEL), const),
                  pl.BlockSpec((D_MODEL, 2 * LANES), const),
                  pl.BlockSpec((1, LANES), const),
                  pl.BlockSpec((2 * LANES, 2 * LANES), const),
                  pl.BlockSpec((tm, tm), const)],
        out_specs=[pl.BlockSpec((tm, D_MODEL), row),
                   pl.BlockSpec((tm * SUBLANES, LANES), row),
                   pl.BlockSpec((tm, LANES), row),
                   pl.BlockSpec((tm, LANES), row),
                   pl.BlockSpec((SUBLANES, LANES), const)],
        out_shape=[jax.ShapeDtypeStruct((t, D_MODEL), F32),
                   jax.ShapeDtypeStruct((t * SUBLANES, LANES), F32),
                   jax.ShapeDtypeStruct((t, LANES), jnp.int32),
                   jax.ShapeDtypeStruct((t, LANES), F32),
                   jax.ShapeDtypeStruct((SUBLANES, LANES), F32)],
        scratch_shapes=[pltpu.VMEM((tm + SUBLANES, CONV_CH), F32), pltpu.VMEM((SUBLANES, LANES), F32)],
        compiler_params=_cparams(1),
        name="outproj_router",
    )(x2, cb, v, v, ysb, yfox, conv_w, og, w_out_b, mod3, g2, rw, rb, bd2, ltri)


def _zero_fill(meta_ref, zero_ref, xg_ref, zsem):
    n_tiles = xg_ref.shape[0] // (TM_GROUP * SUBLANES)
    zero_ref[...] = jnp.zeros_like(zero_ref)

    def sweep(wait):
        def per_expert(e, c):
            pos = meta_ref[N_EXPERTS + e]
            n = meta_ref[2 * N_EXPERTS + e]
            bit = TM_GROUP // 2
            while bit:
                @pl.when((n & bit) != 0)
                def _(pos=pos, bit=bit):
                    cp = pltpu.make_async_copy(
                        zero_ref.at[pl.ds(0, bit * SUBLANES), :],
                        xg_ref.at[pl.ds(pl.multiple_of(pos * SUBLANES, SUBLANES), bit * SUBLANES), :], zsem)
                    cp.wait() if wait else cp.start()
                pos = pos + (n & bit)
                bit //= 2
            return c

        lax.fori_loop(0, N_EXPERTS, per_expert, 0)

        def per_tile(j, c):
            rows = TM_GROUP * SUBLANES
            cp = pltpu.make_async_copy(zero_ref, xg_ref.at[pl.ds(pl.multiple_of(j * rows, rows), rows), :], zsem)
            cp.wait() if wait else cp.start()
            return c

        lax.fori_loop(meta_ref[3 * N_EXPERTS], n_tiles, per_tile, 0)

    sweep(False)
    sweep(True)


def _dispatch_kernel(meta_ref, pos_ref, h_ref, xg_ref, zero_ref, sem, zsem):
    tm = h_ref.shape[0] // SUBLANES

    @pl.when(pl.program_id(0) == 0)
    def _():
        _zero_fill(meta_ref, zero_ref, xg_ref, zsem)

    def issue(r, c):
        for k in range(TOP_K):
            _row_copy(h_ref, r, xg_ref, pos_ref[r * TOP_K + k], sem).start()
        return c

    lax.fori_loop(0, tm, issue, 0)
    n = tm * TOP_K * SUBLANES
    pltpu.make_async_copy(xg_ref.at[pl.ds(0, n), :], xg_ref.at[pl.ds(0, n), :], sem).wait()


def _dispatch(meta, pos, h2t, n_rows):
    t = h2t.shape[0] // SUBLANES
    tm = TM_DISPATCH
    grid_spec = pltpu.PrefetchScalarGridSpec(
        num_scalar_prefetch=1,
        grid=(t // tm,),
        in_specs=[pl.BlockSpec((tm * TOP_K,), lambda i, *_: (i,), memory_space=pltpu.SMEM),
                  pl.BlockSpec((tm * SUBLANES, LANES), lambda i, *_: (i, 0))],
        out_specs=pl.BlockSpec(memory_space=pl.ANY),
        scratch_shapes=[pltpu.VMEM((TM_GROUP * SUBLANES, LANES), F32),
                        pltpu.SemaphoreType.DMA(()), pltpu.SemaphoreType.DMA(())],
    )
    return pl.pallas_call(
        _dispatch_kernel,
        grid_spec=grid_spec,
        out_shape=jax.ShapeDtypeStruct((n_rows * SUBLANES, LANES), F32),
        compiler_params=_cparams(1),
        name="moe_dispatch",
    )(meta, pos, h2t)


def _expert_kernel(te_ref, nv_ref, xg_ref, w1_ref, b1_ref, w2_ref, b2_ref, yg_ref, wg_ref, wl_ref, w2b_ref, wt_ref):
    tmg = xg_ref.shape[0] // SUBLANES
    j = pl.program_id(0)
    nv = nv_ref[j]

    @pl.when(jnp.logical_or(j == 0, te_ref[j] != te_ref[jnp.maximum(j - 1, 0)]))
    def _():
        ncol = wt_ref.shape[0] // ROW_TILES
        half = ncol // 2
        for n0 in range(0, 2 * D_EXPERT, ncol):
            wt = w1_ref[0, 0, :, n0:n0 + ncol].T
            for c in range(ROW_TILES):
                wt_ref[pl.ds(c, ncol, stride=ROW_TILES), :] = wt[:, c * LANES:(c + 1) * LANES]
            rows = slice(n0 // 2, n0 // 2 + half)
            for c in range(ROW_TILES):
                cols = slice(c * LANES, (c + 1) * LANES)
                wg_ref[rows, cols] = wt_ref[pl.ds(c, half, stride=2 * ROW_TILES), :].astype(BF16)
                wl_ref[rows, cols] = wt_ref[pl.ds(ROW_TILES + c, half, stride=2 * ROW_TILES), :].astype(BF16)
            w2b_ref[rows, :] = w2_ref[0, 0, rows, :].astype(BF16)

    @pl.when(nv == 0)
    def _():
        yg_ref[...] = jnp.zeros_like(yg_ref)

    @pl.when(nv > 0)
    def _():
        x = jnp.concatenate([xg_ref[pl.ds(c, tmg, stride=SUBLANES), :] for c in range(ROW_TILES)], axis=-1)
        x = x.astype(BF16)
        glu = jnp.minimum(_dot_nt(x, wg_ref[...]) + b1_ref[0, 0:1, :], SWIGLU_LIMIT)
        lin = jnp.clip(_dot_nt(x, wl_ref[...]) + b1_ref[0, 1:2, :], -SWIGLU_LIMIT, SWIGLU_LIMIT)
        a = glu * (1.0 / (1.0 + jnp.exp(-SWIGLU_ALPHA * glu))) * (lin + 1.0)
        y = _dot(a.astype(BF16), w2b_ref[...]) + b2_ref[0]
        for c in range(ROW_TILES):
            yg_ref[pl.ds(c, tmg, stride=SUBLANES), :] = y[:, c * LANES:(c + 1) * LANES]


def _experts(te, nv, xg, layer, w1, b1, w2, b2):
    tmg = TM_GROUP
    n_tiles = xg.shape[0] // (tmg * SUBLANES)
    grid_spec = pltpu.PrefetchScalarGridSpec(
        num_scalar_prefetch=2,
        grid=(n_tiles,),
        in_specs=[pl.BlockSpec((tmg * SUBLANES, LANES), lambda j, te, nv: (j, 0)),
                  pl.BlockSpec((1, 1, D_MODEL, 2 * D_EXPERT), lambda j, te, nv: (layer, te[j], 0, 0)),
                  pl.BlockSpec((1, 2, D_EXPERT), lambda j, te, nv: (te[j], 0, 0)),
                  pl.BlockSpec((1, 1, D_EXPERT, D_MODEL), lambda j, te, nv: (layer, te[j], 0, 0)),
                  pl.BlockSpec((1, 1, D_MODEL), lambda j, te, nv: (te[j], 0, 0))],
        out_specs=pl.BlockSpec((tmg * SUBLANES, LANES), lambda j, te, nv: (j, 0)),
        scratch_shapes=[pltpu.VMEM((D_EXPERT, D_MODEL), BF16), pltpu.VMEM((D_EXPERT, D_MODEL), BF16),
                        pltpu.VMEM((D_EXPERT, D_MODEL), BF16), pltpu.VMEM((2 * LANES * ROW_TILES, LANES), F32)],
    )
    return pl.pallas_call(
        _expert_kernel,
        grid_spec=grid_spec,
        out_shape=jax.ShapeDtypeStruct(xg.shape, F32),
        compiler_params=_cparams(1, EXPERT_VMEM_LIMIT),
        name="moe_experts",
    )(te, nv, xg, w1, b1, w2, b2)


def _combine_kernel(pos_ref, pos_next_ref, gate_ref, x1_ref, mod_ref, yg_ref, o_ref, buf_ref, sem):
    tm = x1_ref.shape[0]
    i = pl.program_id(0)
    slot = i % 2

    @pl.when(i == 0)
    def _():
        _gather_rows(yg_ref, pos_ref, buf_ref, sem, 0, tm)

    @pl.when(i + 1 < pl.num_programs(0))
    def _():
        _gather_rows(yg_ref, pos_next_ref, buf_ref, sem, 1 - slot, tm)

    _wait_rows(buf_ref, sem, slot)
    _combine_rows(gate_ref, buf_ref, slot, tm, x1_ref, mod_ref[0, 5:6, :], o_ref)


def _combine(pos, gates, x1, mod3, yg, seq):
    t = x1.shape[0]
    tm = TM_COMBINE
    tiles_per_seq = seq // tm
    n = t // tm
    grid_spec = pltpu.PrefetchScalarGridSpec(
        num_scalar_prefetch=0,
        grid=(n,),
        in_specs=[pl.BlockSpec((tm * TOP_K,), lambda i: (i,), memory_space=pltpu.SMEM),
                  pl.BlockSpec((tm * TOP_K,), lambda i: (jnp.minimum(i + 1, n - 1),), memory_space=pltpu.SMEM),
                  pl.BlockSpec((tm, LANES), lambda i: (i, 0)),
                  pl.BlockSpec((tm, D_MODEL), lambda i: (i, 0)),
                  pl.BlockSpec((1, N_MOD, D_MODEL), lambda i: (i // tiles_per_seq, 0, 0)),
                  pl.BlockSpec(memory_space=pl.ANY)],
        out_specs=pl.BlockSpec((tm, D_MODEL), lambda i: (i, 0)),
        scratch_shapes=[pltpu.VMEM((2, TOP_K, tm * SUBLANES, LANES), F32), pltpu.SemaphoreType.DMA((2,))],
    )
    return pl.pallas_call(
        _combine_kernel,
        grid_spec=grid_spec,
        out_shape=jax.ShapeDtypeStruct((t, D_MODEL), F32),
        compiler_params=_cparams(1),
        name="moe_combine",
    )(pos, pos, gates, x1, mod3, yg)


def _block_diag_ones(n):
    i = np.arange(n)
    return jnp.asarray((i[:, None] // HEAD_DIM) == (i[None, :] // HEAD_DIM), dtype=BF16)


def _tri(n, strict, upper):
    i = np.arange(n)
    if upper:
        m = (i[:, None] > i[None, :]) if strict else (i[:, None] >= i[None, :])
        return jnp.asarray(m, dtype=BF16)
    m = (i[None, :] < i[:, None]) if strict else (i[None, :] <= i[:, None])
    return jnp.asarray(m, dtype=BF16)


def _tile2(g):
    return jnp.tile(g, SB_DIM // HEAD_DIM)


def _layer(x_parts, c, seq, layer, last, w_mlp1, w_mlp2, norm1_g, w_ada, b_ada, w_in, conv_w, sb_q_g, sb_k_g, fox_q_g, fox_k_g, fox_f_b,
           out_norm_g, w_out, norm2_g, router_w, router_b, b_mlp1, b_mlp2):
    t = x_parts[0].shape[0] if len(x_parts) == 1 else x_parts[2].shape[0]
    nb = t // seq
    scale = HEAD_DIM ** -0.5

    w_in_p = jnp.pad(w_in, ((0, 0), (0, D_IN_PAD - w_in.shape[1]))).astype(BF16)
    fb = jnp.pad(fox_f_b, (0, LANES - FOX_HEADS)).reshape(1, LANES)
    qscale = scale * LOG2_E
    gains = jnp.stack([_tile2(sb_q_g) * qscale, _tile2(sb_k_g), _tile2(fox_q_g) * qscale, _tile2(fox_k_g)])
    gains = jnp.pad(gains, ((0, SUBLANES - 4), (0, 0)))
    rw = jnp.pad(router_w, ((0, 0), (0, LANES - N_EXPERTS)))
    rw_hi = rw.astype(BF16)
    rw = jnp.concatenate([rw_hi, (rw - rw_hi.astype(F32)).astype(BF16)], axis=1)
    rb = jnp.pad(router_b, (0, LANES - N_EXPERTS)).reshape(1, LANES)
    b1 = jnp.stack([b_mlp1[:, 0::2], b_mlp1[:, 1::2]], axis=1)
    b2 = b_mlp2.reshape(N_EXPERTS, 1, D_MODEL)

    mod3 = _modulation(c, w_ada, b_ada).reshape(nb, N_MOD, D_MODEL)

    outs = _inproj(x_parts, mod3, norm1_g.reshape(1, D_MODEL), w_in_p, _block_diag_ones(SB_DIM),
                   _tri(TM_PROJ, strict=False, upper=False), fb, gains, seq)
    x2 = x_parts[0] if len(x_parts) == 1 else outs[0]
    cb, v, sbq, sbk, sbv, fq, fk, fv, cum = outs[-9:]

    r3 = lambda a: a.reshape(nb, seq, SB_DIM)
    u = _tri(TB, strict=True, upper=True)
    zu = jnp.zeros_like(u)
    uu = jnp.concatenate([jnp.concatenate([u, zu], axis=1), jnp.concatenate([zu, u], axis=1)], axis=0)
    ysb = _sb_attention(r3(sbq), r3(sbk), r3(sbv), uu)

    n_kb = seq // TB
    f2 = cum[:, :FOX_HEADS, :].reshape(nb, FOX_DIM // LANES, 2, n_kb, TB).transpose(0, 1, 3, 2, 4)
    f2 = f2.reshape(nb, FOX_DIM // LANES, n_kb, 2 * TB)
    fend = cum[:, :FOX_HEADS, TB - 1::TB].reshape(nb * FOX_HEADS * n_kb)
    zbound = (2.0 * 1.02 * HEAD_DIM * qscale * jnp.max(jnp.abs(fox_q_g)) * jnp.max(jnp.abs(fox_k_g))).reshape(1)
    yfox = _fox_attention(r3(fq), r3(fk), r3(fv), f2, fend, zbound)

    x1, h2t, sel, gates, cnt = _outproj(
        x2, cb, v, ysb.reshape(t, SB_DIM), yfox.reshape(t, FOX_DIM), conv_w, out_norm_g.reshape(1, D_MIX),
        w_out.astype(BF16), mod3, norm2_g.reshape(1, D_MODEL), rw, rb, _block_diag_ones(2 * LANES),
        _tri(TM_PROJ, strict=True, upper=False), seq)

    counts = cnt[0, :N_EXPERTS].astype(jnp.int32)
    padded = ((counts + TM_GROUP - 1) // TM_GROUP) * TM_GROUP
    off_end = jnp.cumsum(padded)
    off = off_end - padded
    n_rows = t * TOP_K + N_EXPERTS * TM_GROUP
    n_tiles = n_rows // TM_GROUP
    start = jnp.arange(n_tiles, dtype=jnp.int32) * TM_GROUP
    te = jnp.minimum(jnp.sum(start[:, None] >= off_end[None, :], axis=1), N_EXPERTS - 1).astype(jnp.int32)
    nv = jnp.clip(off[te] + counts[te] - start, 0, TM_GROUP).astype(jnp.int32)
    meta = jnp.concatenate([off, off + counts, padded - counts, off_end[-1:] // TM_GROUP]).astype(jnp.int32)

    assert t <= 1 << RANK_BITS
    words = sel[:, :TOP_K].reshape(t * TOP_K // LANES, LANES)
    picked = jnp.where((words >> RANK_BITS)[..., None] == jnp.arange(N_EXPERTS), off, 0).sum(-1)
    pos = (picked + (words & ((1 << RANK_BITS) - 1))).astype(jnp.int32).reshape(t * TOP_K)
    xg = _dispatch(meta, pos, h2t, n_rows)
    yg = _experts(te, nv, xg, layer, w_mlp1, b1, w_mlp2, b2)
    return _combine(pos, gates, x1, mod3, yg, seq) if last else (pos, gates, x1, mod3, yg)


def kernel(x, c, norm1_g, w_ada, b_ada, w_in, conv_w, sb_q_g, sb_k_g, fox_q_g, fox_k_g, fox_f_b, out_norm_g, w_out,
           norm2_g, router_w, router_b, w_mlp1, b_mlp1, w_mlp2, b_mlp2):
    b, s, d = x.shape
    x_parts = (x.reshape(b * s, d),)
    params = (norm1_g, w_ada, b_ada, w_in, conv_w, sb_q_g, sb_k_g, fox_q_g, fox_k_g, fox_f_b, out_norm_g, w_out,
              norm2_g, router_w, router_b, b_mlp1, b_mlp2)
    depth = norm1_g.shape[0]
    for layer in range(depth):
        x_parts = _layer(x_parts, c, s, layer, layer == depth - 1, w_mlp1, w_mlp2, *(p[layer] for p in params))
    return x_parts.reshape(b, s, d)
```

```python
import functools

import numpy as np
import jax
import jax.numpy as jnp
from jax import lax
from jax.experimental import pallas as pl
from jax.experimental.pallas import tpu as pltpu

F32 = jnp.float32
BF16 = jnp.bfloat16

D_MODEL = 1024
HEAD_DIM = 64
CONV_CH = 256
CONV_W = 3
SB_DIM = 384
FOX_DIM = 384
FOX_HEADS = 6
N_PAIRS = 3
D_MIX = 1024
N_MOD = 6
N_EXPERTS = 32
TOP_K = 4
D_EXPERT = 1024
SWIGLU_ALPHA = 1.702
SWIGLU_LIMIT = 7.0
EPS = 1e-6

LANES = 128
SUBLANES = 8
ROW_TILES = D_MODEL // LANES
D_IN_PAD = 3 * CONV_CH + 3 * SB_DIM + 3 * FOX_DIM + LANES
COL_SBQ = 3 * CONV_CH
COL_SBK = COL_SBQ + SB_DIM
COL_SBV = COL_SBK + SB_DIM
COL_FQ = COL_SBV + SB_DIM
COL_FK = COL_FQ + FOX_DIM
COL_FV = COL_FK + FOX_DIM
COL_FLOG = COL_FV + FOX_DIM

TM_PROJ = 512
TQ = 256
TB = 128
TM_DISPATCH = 1024
TM_COMBINE = 256
TM_GROUP = 512
RANK_BITS = 20
LOG2_E = 1.4426950408889634
LOG2_NEGLIGIBLE = -40.0
VMEM_LIMIT = 48 * 1024 * 1024
INPROJ_VMEM_LIMIT = 56 * 1024 * 1024
EXPERT_VMEM_LIMIT = 58 * 1024 * 1024


def _cparams(n_axes, vmem=VMEM_LIMIT):
    return pltpu.CompilerParams(dimension_semantics=("arbitrary",) * n_axes, vmem_limit_bytes=vmem)


def _dot(a, b):
    return jnp.dot(a, b, preferred_element_type=F32)


def _dot_nt(a, b):
    return lax.dot_general(a, b, (((1,), (1,)), ((), ())), preferred_element_type=F32)


def _split2(x):
    hi = x.astype(BF16)
    lo = (x - hi.astype(F32)).astype(BF16)
    return hi, lo


def _row_copy(src, src_row, dst, dst_row, sem):
    return pltpu.make_async_copy(src.at[pl.ds(pl.multiple_of(src_row * SUBLANES, SUBLANES), SUBLANES), :],
                                 dst.at[pl.ds(pl.multiple_of(dst_row * SUBLANES, SUBLANES), SUBLANES), :], sem)


def _log_sigmoid(x):
    return jnp.minimum(x, 0.0) - jnp.log(1.0 + jnp.exp(-jnp.abs(x)))


def _mod_kernel(c_ref, w_ref, b_ref, o_ref):
    c = c_ref[...]
    s = c / (1.0 + jnp.exp(-c))
    o_ref[...] = _dot(s.astype(BF16), w_ref[...].astype(BF16)) + b_ref[...]


def _modulation(c, w_ada, b_ada):
    b, d = c.shape
    n = w_ada.shape[1]
    tn = 1024
    return pl.pallas_call(
        _mod_kernel,
        grid=(n // tn,),
        in_specs=[pl.BlockSpec((b, d), lambda j: (0, 0)),
                  pl.BlockSpec((d, tn), lambda j: (0, j)),
                  pl.BlockSpec((1, tn), lambda j: (0, j))],
        out_specs=pl.BlockSpec((b, tn), lambda j: (0, j)),
        out_shape=jax.ShapeDtypeStruct((b, n), F32),
        compiler_params=_cparams(1),
        name="modulation",
    )(c, w_ada, b_ada.reshape(1, n))


def _gather_rows(yg_ref, idx_ref, buf_ref, sem, slot, n_rows):
    def issue(r, c):
        for k in range(TOP_K):
            _row_copy(yg_ref, idx_ref[r * TOP_K + k], buf_ref.at[slot, k], r, sem.at[slot]).start()
        return c

    lax.fori_loop(0, n_rows, issue, 0)


def _combine_rows(gate_ref, buf_ref, slot, n_rows, x1_ref, g2, out_ref):
    gates = gate_ref[...]
    gate_cols = [gates[:, k:k + 1] for k in range(TOP_K)]
    for c in range(ROW_TILES):
        acc = jnp.zeros((n_rows, LANES), F32)
        for k in range(TOP_K):
            acc = acc + gate_cols[k] * buf_ref[slot, k, pl.ds(c, n_rows, stride=SUBLANES), :]
        cols = slice(c * LANES, (c + 1) * LANES)
        out_ref[:, cols] = x1_ref[:, cols] + g2[:, cols] * acc


def _wait_rows(buf_ref, sem, slot):
    for k in range(TOP_K):
        pltpu.make_async_copy(buf_ref.at[slot, k], buf_ref.at[slot, k], sem.at[slot]).wait()


def _inproj_kernel(tiles_per_seq, fused, *refs):
    if fused:
        (pos_ref, pos_next_ref, gate_ref, x1_ref, modp_ref, yg_ref, mod_ref, g1_ref, w_ref, bd_ref, tri_ref, fb_ref,
         gains_ref, x_out_ref, cb_ref, v_ref, sbq_ref, sbk_ref, sbv_ref, fq_ref, fk_ref, fv_ref, cum_ref,
         carry_ref, x_ref, buf_ref, sem) = refs
        tm = x1_ref.shape[0]
        i = pl.program_id(0)
        slot = i % 2

        @pl.when(i == 0)
        def _():
            _gather_rows(yg_ref, pos_ref, buf_ref, sem, 0, tm)

        @pl.when(i + 1 < pl.num_programs(0))
        def _():
            _gather_rows(yg_ref, pos_next_ref, buf_ref, sem, 1 - slot, tm)

        _wait_rows(buf_ref, sem, slot)
        _combine_rows(gate_ref, buf_ref, slot, tm, x1_ref, modp_ref[0, 5:6, :], x_ref)
        x_out_ref[...] = x_ref[...]
    else:
        (x_ref, mod_ref, g1_ref, w_ref, bd_ref, tri_ref, fb_ref, gains_ref,
         cb_ref, v_ref, sbq_ref, sbk_ref, sbv_ref, fq_ref, fk_ref, fv_ref, cum_ref, carry_ref) = refs
        tm = x_ref.shape[0]
    x = x_ref[...]
    ms = jnp.mean(x * x, axis=-1, keepdims=True)
    h = (x * lax.rsqrt(ms + EPS)) * g1_ref[...]
    h = h * (1.0 + mod_ref[0, 1:2, :]) + mod_ref[0, 0:1, :]
    hb = h.astype(BF16)

    pc = _dot(hb, w_ref[:, 0:3 * CONV_CH])
    cb_ref[...] = pc[:, 0:CONV_CH]
    v_ref[...] = pc[:, CONV_CH:2 * CONV_CH] * pc[:, 2 * CONV_CH:3 * CONV_CH]

    normed = ((COL_SBQ, sbq_ref), (COL_SBK, sbk_ref), (COL_FQ, fq_ref), (COL_FK, fk_ref))
    proj = [_dot(hb, w_ref[:, col:col + SB_DIM]) for col, _ in normed]
    sbv_ref[...] = _dot(hb, w_ref[:, COL_SBV:COL_SBV + SB_DIM]).astype(BF16)
    fv_ref[...] = _dot(hb, w_ref[:, COL_FV:COL_FV + FOX_DIM]).astype(BF16)
    flog = _dot(hb, w_ref[:, COL_FLOG:COL_FLOG + LANES])
    sumsq = [_dot((p * p).astype(BF16), bd_ref[...]) for p in proj]
    for row, ((_, out_ref), p, ss) in enumerate(zip(normed, proj, sumsq)):
        out_ref[...] = (p * lax.rsqrt(ss * (1.0 / HEAD_DIM) + EPS) * gains_ref[row:row + 1, :]).astype(BF16)

    lf = _log_sigmoid(flog + fb_ref[...]) * LOG2_E
    t0 = lf.astype(BF16)
    r1 = lf - t0.astype(F32)
    t1 = r1.astype(BF16)
    t2 = (r1 - t1.astype(F32)).astype(BF16)
    tri = tri_ref[...]
    cs = _dot(tri, t0) + _dot(tri, t1) + _dot(tri, t2)

    @pl.when(pl.program_id(0) % tiles_per_seq == 0)
    def _():
        carry_ref[...] = jnp.zeros_like(carry_ref)

    cum = cs + carry_ref[0:1, :]
    carry_ref[0:1, :] = cum[tm - 1:tm, :]
    cum_ref[0] = cum.T[0:SUBLANES, :]


def _inproj(x_parts, mod3, g1, w_in_p, bd, tri, fb, gains, seq):
    fused = len(x_parts) > 1
    t = x_parts[2].shape[0] if fused else x_parts[0].shape[0]
    tm = TM_PROJ
    n = t // tm
    tiles_per_seq = seq // tm
    nb = t // seq
    row = lambda i: (i, 0)
    const = lambda i: (0, 0)
    per_seq = lambda i: (i // tiles_per_seq, 0, 0)
    outs = [jax.ShapeDtypeStruct((t, CONV_CH), F32), jax.ShapeDtypeStruct((t, CONV_CH), F32)]
    outs += [jax.ShapeDtypeStruct((t, SB_DIM), BF16)] * 6
    outs += [jax.ShapeDtypeStruct((nb, SUBLANES, seq), F32)]
    out_specs = [pl.BlockSpec((tm, CONV_CH), row)] * 2 + [pl.BlockSpec((tm, SB_DIM), row)] * 6
    out_specs += [pl.BlockSpec((1, SUBLANES, tm), lambda i: (i // tiles_per_seq, 0, i % tiles_per_seq))]
    in_specs = [pl.BlockSpec((1, N_MOD, D_MODEL), per_seq),
                pl.BlockSpec((1, D_MODEL), const),
                pl.BlockSpec((D_MODEL, D_IN_PAD), const),
                pl.BlockSpec((SB_DIM, SB_DIM), const),
                pl.BlockSpec((tm, tm), const),
                pl.BlockSpec((1, LANES), const),
                pl.BlockSpec((SUBLANES, SB_DIM), const)]
    scratch = [pltpu.VMEM((SUBLANES, LANES), F32)]
    if fused:
        pos, gates, x1, mod3_prev, yg = x_parts
        operands = (pos, pos, gates, x1, mod3_prev, yg)
        in_specs = [pl.BlockSpec((tm * TOP_K,), lambda i: (i,), memory_space=pltpu.SMEM),
                    pl.BlockSpec((tm * TOP_K,), lambda i: (jnp.minimum(i + 1, n - 1),), memory_space=pltpu.SMEM),
                    pl.BlockSpec((tm, LANES), row),
                    pl.BlockSpec((tm, D_MODEL), row),
                    pl.BlockSpec((1, N_MOD, D_MODEL), per_seq),
                    pl.BlockSpec(memory_space=pl.ANY)] + in_specs
        outs = [jax.ShapeDtypeStruct((t, D_MODEL), F32)] + outs
        out_specs = [pl.BlockSpec((tm, D_MODEL), row)] + out_specs
        scratch += [pltpu.VMEM((tm, D_MODEL), F32), pltpu.VMEM((2, TOP_K, tm * SUBLANES, LANES), F32),
                    pltpu.SemaphoreType.DMA((2,))]
    else:
        operands = x_parts
        in_specs = [pl.BlockSpec((tm, D_MODEL), row)] + in_specs
    return pl.pallas_call(
        functools.partial(_inproj_kernel, tiles_per_seq, fused),
        grid=(n,),
        in_specs=in_specs,
        out_specs=out_specs,
        out_shape=outs,
        scratch_shapes=scratch,
        compiler_params=_cparams(1, INPROJ_VMEM_LIMIT),
        name="inproj",
    )(*operands, mod3, g1, w_in_p, bd, tri, fb, gains)


def _stack_heads(k_ref, v_ref, kk_ref, vv_ref):
    n_blocks = kk_ref.shape[1]
    lane = lax.broadcasted_iota(jnp.int32, (TB, LANES), 1)
    first = lane < HEAD_DIM

    def body(j, c):
        rows = pl.ds(pl.multiple_of(j * TB, TB), TB)
        for src, dst in ((k_ref, kk_ref), (v_ref, vv_ref)):
            for p in range(N_PAIRS):
                blk = src[0, rows, p * LANES:(p + 1) * LANES]
                zero = jnp.zeros_like(blk)
                dst[p, j, 0:TB, :] = jnp.where(first, blk, zero)
                dst[p, j, TB:2 * TB, :] = jnp.where(first, zero, blk)
        return c

    lax.fori_loop(0, n_blocks, body, 0)


def _per_head(a0, a1):
    return jnp.concatenate([jnp.broadcast_to(a0, (TB, TB)), jnp.broadcast_to(a1, (TB, TB))], axis=1)


def _pair_iota():
    row = lax.broadcasted_iota(jnp.int32, (TB, 2 * TB), 0)
    col = lax.broadcasted_iota(jnp.int32, (TB, 2 * TB), 1) % TB
    return row, col


_CHAINS = [(p, s) for p in range(N_PAIRS) for s in range(2)]


def _q_block(q_ref, p, s):
    return q_ref[0, s * TB:(s + 1) * TB, p * LANES:(p + 1) * LANES]


def _sb_kernel(q_ref, k_ref, v_ref, uu_ref, o_ref, kk_ref, vv_ref):
    qi = pl.program_id(1)

    @pl.when(qi == 0)
    def _():
        _stack_heads(k_ref, v_ref, kk_ref, vv_ref)

    uu = uu_ref[...]
    row, col = _pair_iota()
    strict = col < row

    def sweep(states, backs):
        items = [(c, b) for b in range(len(backs)) for c in range(len(_CHAINS))]
        kbs, keeps, z = {}, {}, {}
        for c, b in items:
            p, s = _CHAINS[c]
            back = backs[b]
            kb = 2 * qi + s - back
            if isinstance(back, int) and back == 0:
                keeps[c, b] = strict
            elif isinstance(back, int) and back == 1 and s == 1:
                keeps[c, b] = None
            else:
                keeps[c, b] = kb >= 0
            kbs[c, b] = jnp.maximum(kb, 0)
            z[c, b] = _dot_nt(_q_block(q_ref, p, s), kk_ref[p, kbs[c, b]])
        log_beta, log_1mb, sums = {}, {}, {}
        for it in items:
            log_beta[it] = jnp.minimum(z[it], 0.0) - jnp.log2(1.0 + jnp.exp2(-jnp.abs(z[it])))
            l1 = log_beta[it] - z[it]
            log_1mb[it] = l1 if keeps[it] is None else jnp.where(keeps[it], l1, 0.0)
            sums[it] = (jnp.sum(log_1mb[it][:, :TB], axis=-1, keepdims=True),
                        jnp.sum(log_1mb[it][:, TB:], axis=-1, keepdims=True))
        later = {}
        for it in items:
            hi, lo = _split2(log_1mb[it])
            later[it] = _dot(hi, uu) + _dot(lo, uu)
        states = list(states)
        for it in items:
            c = it[0]
            rs0, rs1, acc = states[c]
            w = jnp.exp2(log_beta[it] + later[it] + _per_head(rs0, rs1))
            if keeps[it] is not None:
                w = jnp.where(keeps[it], w, 0.0)
            acc = acc + _dot(w.astype(BF16), vv_ref[_CHAINS[c][0], kbs[it]])
            states[c] = (rs0 + sums[it][0], rs1 + sums[it][1], acc)
        return states

    def more(states):
        top = states[0][0]
        for st in states:
            top = jnp.maximum(top, jnp.maximum(st[0], st[1]))
        return (jnp.max(top) > LOG2_NEGLIGIBLE).astype(jnp.int32)

    zero = jnp.zeros((TB, 1), F32)
    states = sweep([(zero, zero, jnp.zeros((TB, LANES), F32))] * len(_CHAINS), (0, 1))

    def cond(c):
        return jnp.logical_and(2 * qi + 1 - c[0] >= 0, c[2] > 0)

    def body(c):
        back, states, _ = c
        states = sweep(states, (back,))
        return back + 1, states, more(states)

    _, states, _ = lax.while_loop(cond, body, (jnp.int32(2), states, more(states)))
    for (p, s), st in zip(_CHAINS, states):
        o_ref[0, s * TB:(s + 1) * TB, p * LANES:(p + 1) * LANES] = st[2]


def _sb_attention(q, k, v, uu):
    b, s, d = q.shape
    tq = TQ
    qspec = pl.BlockSpec((1, tq, d), lambda bi, qi: (bi, qi, 0))
    kvspec = pl.BlockSpec((1, s, d), lambda bi, qi: (bi, 0, 0))
    scratch = pltpu.VMEM((N_PAIRS, s // TB, 2 * TB, LANES), BF16)
    return pl.pallas_call(
        _sb_kernel,
        grid=(b, s // tq),
        in_specs=[qspec, kvspec, kvspec, pl.BlockSpec((2 * TB, 2 * TB), lambda bi, qi: (0, 0))],
        out_specs=qspec,
        out_shape=jax.ShapeDtypeStruct((b, s, d), F32),
        scratch_shapes=[scratch, scratch],
        compiler_params=_cparams(2),
        name="sb_attention",
    )(q, k, v, uu)


def _fox_kernel(n_kb, fend_ref, zb_ref, q_ref, k_ref, v_ref, f2_ref, o_ref, kk_ref, vv_ref, sc_ref):
    bi = pl.program_id(0)
    qi = pl.program_id(1)

    @pl.when(qi == 0)
    def _():
        _stack_heads(k_ref, v_ref, kk_ref, vv_ref)

    row, col = _pair_iota()
    causal = col <= row
    lane = lax.broadcasted_iota(jnp.int32, (TB, LANES), 1)
    first = lane < HEAD_DIM
    zbound = zb_ref[0]

    def first_block(head, blk):
        base = (bi * FOX_HEADS + head) * n_kb
        f_q = fend_ref[base + jnp.maximum(blk - 1, 0)]

        def keep(kb):
            gap = zbound + f_q - fend_ref[base + jnp.maximum(kb - 1, 0)]
            return jnp.logical_and(kb > 0, gap >= LOG2_NEGLIGIBLE)

        return lax.while_loop(keep, lambda kb: kb - 1, blk)

    def scores(p, s, back, steps):
        kb = 2 * qi + s - back
        sc = _dot_nt(_q_block(q_ref, p, s), kk_ref[p, jnp.maximum(kb, 0)]) - f2_ref[0, p, pl.ds(jnp.maximum(kb, 0), 1), :]
        return jnp.where(jnp.logical_and(kb >= 0, back <= steps), sc, -1e30)

    def head_max(a):
        return (jnp.max(a[:, :TB], axis=-1, keepdims=True), jnp.max(a[:, TB:], axis=-1, keepdims=True))

    def head_sum(a):
        return (jnp.sum(a[:, :TB], axis=-1, keepdims=True), jnp.sum(a[:, TB:], axis=-1, keepdims=True))

    for p in range(N_PAIRS):
        steps = jnp.int32(0)
        for s in range(2):
            for h in range(2):
                steps = jnp.maximum(steps, 2 * qi + s - first_block(2 * p + h, 2 * qi + s))
        n_it = (steps + 1) // 2

        runmax = []
        for s in range(2):
            sc = jnp.where(causal, scores(p, s, 0, steps), -1e30)
            sc_ref[s, 0] = sc
            runmax.append(sc)

        def pass1(it, runmax):
            out = list(runmax)
            for u in (1, 2):
                back = 2 * it + u
                for s in range(2):
                    sc = scores(p, s, back, steps)
                    sc_ref[s, back] = sc
                    out[s] = jnp.maximum(out[s], sc)
            return out

        runmax = lax.fori_loop(0, n_it, pass1, runmax)
        tops = [_per_head(*head_max(runmax[s])) for s in range(2)]

        def weigh(s, back, runsum, acc):
            pr = jnp.exp2(sc_ref[s, back] - tops[s])
            kb = jnp.maximum(2 * qi + s - back, 0)
            return runsum + pr, acc + _dot(pr.astype(BF16), vv_ref[p, kb])

        state = [weigh(s, 0, jnp.zeros((TB, 2 * TB), F32), jnp.zeros((TB, LANES), F32)) for s in range(2)]

        def pass2(it, state):
            out = list(state)
            for u in (1, 2):
                for s in range(2):
                    out[s] = weigh(s, 2 * it + u, *out[s])
            return out

        state = lax.fori_loop(0, n_it, pass2, state)
        for s in range(2):
            l0, l1 = head_sum(state[s][0])
            o_ref[0, s * TB:(s + 1) * TB, p * LANES:(p + 1) * LANES] = state[s][1] / jnp.where(first, l0, l1)


def _fox_attention(q, k, v, f2, fend, zbound):
    b, s, d = q.shape
    tq = TQ
    n_kb = s // TB
    qspec = pl.BlockSpec((1, tq, d), lambda bi, qi, *_: (bi, qi, 0))
    kvspec = pl.BlockSpec((1, s, d), lambda bi, qi, *_: (bi, 0, 0))
    scratch = pltpu.VMEM((N_PAIRS, n_kb, 2 * TB, LANES), BF16)
    grid_spec = pltpu.PrefetchScalarGridSpec(
        num_scalar_prefetch=2,
        grid=(b, s // tq),
        in_specs=[qspec, kvspec, kvspec,
                  pl.BlockSpec((1, N_PAIRS, n_kb, 2 * TB), lambda bi, qi, *_: (bi, 0, 0, 0))],
        out_specs=qspec,
        scratch_shapes=[scratch, scratch, pltpu.VMEM((2, n_kb + 2, TB, 2 * TB), F32)],
    )
    return pl.pallas_call(
        functools.partial(_fox_kernel, n_kb),
        grid_spec=grid_spec,
        out_shape=jax.ShapeDtypeStruct((b, s, d), F32),
        compiler_params=_cparams(2),
        name="fox_attention",
    )(fend, zbound, q, k, v, f2)


def _outproj_kernel(tiles_per_seq, x_ref, cb_ref, v_ref, vprev_ref, ysb_ref, yfox_ref, cw_ref, og_ref, wout_ref,
                    mod_ref, g2_ref, rw_ref, rb_ref, bd_ref, ltri_ref,
                    x1_ref, h2t_ref, sel_ref, gate_ref, cnt_ref, vext_ref, carry_ref):
    tm = x_ref.shape[0]
    i = pl.program_id(0)

    @pl.when(i == 0)
    def _():
        carry_ref[...] = jnp.zeros_like(carry_ref)

    first = (i % tiles_per_seq) == 0
    vext_ref[0:SUBLANES, :] = jnp.where(first, 0.0, vprev_ref[...])
    vext_ref[SUBLANES:SUBLANES + tm, :] = v_ref[...]
    conv = (cw_ref[0:1, :] * vext_ref[SUBLANES - 2:SUBLANES - 2 + tm, :]
            + cw_ref[1:2, :] * vext_ref[SUBLANES - 1:SUBLANES - 1 + tm, :]
            + cw_ref[2:3, :] * v_ref[...])
    y = jnp.concatenate([cb_ref[...] * conv, ysb_ref[...], yfox_ref[...]], axis=-1)

    ysq = (y * y).astype(BF16)
    bd = bd_ref[...]
    ss = jnp.concatenate([_dot(ysq[:, c:c + 2 * LANES], bd) for c in range(0, D_MIX, 2 * LANES)], axis=-1)
    yn = y * lax.rsqrt(ss * (1.0 / HEAD_DIM) + EPS) * og_ref[...]
    x1 = x_ref[...] + mod_ref[0, 2:3, :] * _dot(yn.astype(BF16), wout_ref[...])
    x1_ref[...] = x1

    ms = jnp.mean(x1 * x1, axis=-1, keepdims=True)
    h2 = (x1 * lax.rsqrt(ms + EPS)) * g2_ref[...]
    h2 = h2 * (1.0 + mod_ref[0, 4:5, :]) + mod_ref[0, 3:4, :]
    for j in range(ROW_TILES):
        h2t_ref[pl.ds(j, tm, stride=SUBLANES), :] = h2[:, j * LANES:(j + 1) * LANES]

    h_hi, h_lo = _split2(h2)
    both = _dot(h_hi, rw_ref[...])
    logits = both[:, :LANES] + both[:, LANES:] + _dot(h_lo, rw_ref[:, :LANES]) + rb_ref[...]
    lane = lax.broadcasted_iota(jnp.int32, (tm, LANES), 1)
    lane_f = lane.astype(F32)
    neg = jnp.float32(-jnp.inf)
    cur = jnp.where(lane < N_EXPERTS, logits, neg)
    tops, sels = [], []
    for _ in range(TOP_K):
        mk = jnp.max(cur, axis=-1, keepdims=True)
        ik = jnp.min(jnp.where(cur == mk, lane_f, float(LANES)), axis=-1, keepdims=True)
        sel = lane_f == ik
        cur = jnp.where(sel, neg, cur)
        tops.append(mk)
        sels.append((sel, ik.astype(jnp.int32)))
    es = [jnp.exp(t - tops[0]) for t in tops]
    inv = 1.0 / (es[0] + es[1] + es[2] + es[3])

    multi = jnp.zeros((tm, LANES), F32)
    for sel, _ in sels:
        multi = jnp.where(sel, 1.0, multi)
    before = _dot(ltri_ref[...], multi.astype(BF16)) + carry_ref[0:1, :]
    carry_ref[0:1, :] = before[tm - 1:tm, :] + multi[tm - 1:tm, :]

    sel_out = jnp.zeros((tm, LANES), jnp.int32)
    gate_out = jnp.zeros((tm, LANES), F32)
    for k, (sel, ik) in enumerate(sels):
        rank = jnp.sum(jnp.where(sel, before, 0.0), axis=-1, keepdims=True).astype(jnp.int32)
        sel_out = jnp.where(lane == k, ik * (1 << RANK_BITS) + rank, sel_out)
        gate_out = jnp.where(lane == k, es[k] * inv, gate_out)
    sel_ref[...] = sel_out
    gate_ref[...] = gate_out
    cnt_ref[...] = jnp.broadcast_to(carry_ref[0:1, :], cnt_ref.shape)


def _outproj(x2, cb, v, ysb, yfox, conv_w, og, w_out_b, mod3, g2, rw, rb, bd2, ltri, seq):
    t = x2.shape[0]
    tm = TM_PROJ
    tiles_per_seq = seq // tm
    row = lambda i: (i, 0)
    const = lambda i: (0, 0)
    return pl.pallas_call(
        functools.partial(_outproj_kernel, tiles_per_seq),
        grid=(t // tm,),
        in_specs=[pl.BlockSpec((tm, D_MODEL), row),
                  pl.BlockSpec((tm, CONV_CH), row),
                  pl.BlockSpec((tm, CONV_CH), row),
                  pl.BlockSpec((SUBLANES, CONV_CH), lambda i: (jnp.maximum(i * (tm // SUBLANES) - 1, 0), 0)),
                  pl.BlockSpec((tm, SB_DIM), row),
                  pl.BlockSpec((tm, FOX_DIM), row),
                  pl.BlockSpec((CONV_W, CONV_CH), const),
                  pl.BlockSpec((1, D_MIX), const),
                  pl.BlockSpec((D_MIX, D_MODEL), const),
                  pl.BlockSpec((1, N_MOD, D_MODEL), lambda i: (i // tiles_per_seq, 0, 0)),
                  pl.BlockSpec((1, D_MODEL), const),
                  pl.BlockSpec((D_MODEL, 2 * LANES), const),
                  pl.BlockSpec((1, LANES), const),
                  pl.BlockSpec((2 * LANES, 2 * LANES), const),
                  pl.BlockSpec((tm, tm), const)],
        out_specs=[pl.BlockSpec((tm, D_MODEL), row),
                   pl.BlockSpec((tm * SUBLANES, LANES), row),
                   pl.BlockSpec((tm, LANES), row),
                   pl.BlockSpec((tm, LANES), row),
                   pl.BlockSpec((SUBLANES, LANES), const)],
        out_shape=[jax.ShapeDtypeStruct((t, D_MODEL), F32),
                   jax.ShapeDtypeStruct((t * SUBLANES, LANES), F32),
                   jax.ShapeDtypeStruct((t, LANES), jnp.int32),
                   jax.ShapeDtypeStruct((t, LANES), F32),
                   jax.ShapeDtypeStruct((SUBLANES, LANES), F32)],
        scratch_shapes=[pltpu.VMEM((tm + SUBLANES, CONV_CH), F32), pltpu.VMEM((SUBLANES, LANES), F32)],
        compiler_params=_cparams(1),
        name="outproj_router",
    )(x2, cb, v, v, ysb, yfox, conv_w, og, w_out_b, mod3, g2, rw, rb, bd2, ltri)


def _zero_fill(meta_ref, zero_ref, xg_ref, zsem):
    n_tiles = xg_ref.shape[0] // (TM_GROUP * SUBLANES)
    zero_ref[...] = jnp.zeros_like(zero_ref)

    def sweep(wait):
        def per_expert(e, c):
            pos = meta_ref[N_EXPERTS + e]
            n = meta_ref[2 * N_EXPERTS + e]
            bit = TM_GROUP // 2
            while bit:
                @pl.when((n & bit) != 0)
                def _(pos=pos, bit=bit):
                    cp = pltpu.make_async_copy(
                        zero_ref.at[pl.ds(0, bit * SUBLANES), :],
                        xg_ref.at[pl.ds(pl.multiple_of(pos * SUBLANES, SUBLANES), bit * SUBLANES), :], zsem)
                    cp.wait() if wait else cp.start()
                pos = pos + (n & bit)
                bit //= 2
            return c

        lax.fori_loop(0, N_EXPERTS, per_expert, 0)

        def per_tile(j, c):
            rows = TM_GROUP * SUBLANES
            cp = pltpu.make_async_copy(zero_ref, xg_ref.at[pl.ds(pl.multiple_of(j * rows, rows), rows), :], zsem)
            cp.wait() if wait else cp.start()
            return c

        lax.fori_loop(meta_ref[3 * N_EXPERTS], n_tiles, per_tile, 0)

    sweep(False)
    sweep(True)


def _dispatch_kernel(meta_ref, pos_ref, h_ref, xg_ref, zero_ref, sem, zsem):
    tm = h_ref.shape[0] // SUBLANES

    @pl.when(pl.program_id(0) == 0)
    def _():
        _zero_fill(meta_ref, zero_ref, xg_ref, zsem)

    def issue(r, c):
        for k in range(TOP_K):
            _row_copy(h_ref, r, xg_ref, pos_ref[r * TOP_K + k], sem).start()
        return c

    lax.fori_loop(0, tm, issue, 0)
    n = tm * TOP_K * SUBLANES
    pltpu.make_async_copy(xg_ref.at[pl.ds(0, n), :], xg_ref.at[pl.ds(0, n), :], sem).wait()


def _dispatch(meta, pos, h2t, n_rows):
    t = h2t.shape[0] // SUBLANES
    tm = TM_DISPATCH
    grid_spec = pltpu.PrefetchScalarGridSpec(
        num_scalar_prefetch=1,
        grid=(t // tm,),
        in_specs=[pl.BlockSpec((tm * TOP_K,), lambda i, *_: (i,), memory_space=pltpu.SMEM),
                  pl.BlockSpec((tm * SUBLANES, LANES), lambda i, *_: (i, 0))],
        out_specs=pl.BlockSpec(memory_space=pl.ANY),
        scratch_shapes=[pltpu.VMEM((TM_GROUP * SUBLANES, LANES), F32),
                        pltpu.SemaphoreType.DMA(()), pltpu.SemaphoreType.DMA(())],
    )
    return pl.pallas_call(
        _dispatch_kernel,
        grid_spec=grid_spec,
        out_shape=jax.ShapeDtypeStruct((n_rows * SUBLANES, LANES), F32),
        compiler_params=_cparams(1),
        name="moe_dispatch",
    )(meta, pos, h2t)


def _expert_kernel(layer, te_ref, nv_ref, nxt_ref, run_ref, xg_ref, w1_hbm, b1_ref, w2_hbm, b2_ref, yg_ref,
                   wg_ref, wl_ref, w2b_ref, wt_ref, w1_ref, w2_ref, wsem):
    tmg = xg_ref.shape[0] // SUBLANES
    j = pl.program_id(0)
    nv = nv_ref[j]
    slot = run_ref[j] % 2

    def weight_copies(expert, to_slot):
        return (pltpu.make_async_copy(w1_hbm.at[layer, expert], w1_ref.at[to_slot], wsem.at[to_slot, 0]),
                pltpu.make_async_copy(w2_hbm.at[layer, expert], w2_ref.at[to_slot], wsem.at[to_slot, 1]))

    @pl.when(j == 0)
    def _():
        for cp in weight_copies(te_ref[0], slot):
            cp.start()

    @pl.when(jnp.logical_or(j == 0, te_ref[j] != te_ref[jnp.maximum(j - 1, 0)]))
    def _():
        for cp in weight_copies(te_ref[j], slot):
            cp.wait()

        @pl.when(nxt_ref[j] >= 0)
        def _():
            for cp in weight_copies(nxt_ref[j], 1 - slot):
                cp.start()

        ncol = wt_ref.shape[0] // ROW_TILES
        half = ncol // 2
        for n0 in range(0, 2 * D_EXPERT, ncol):
            wt = w1_ref[slot, :, n0:n0 + ncol].T
            for c in range(ROW_TILES):
                wt_ref[pl.ds(c, ncol, stride=ROW_TILES), :] = wt[:, c * LANES:(c + 1) * LANES]
            rows = slice(n0 // 2, n0 // 2 + half)
            for c in range(ROW_TILES):
                cols = slice(c * LANES, (c + 1) * LANES)
                wg_ref[rows, cols] = wt_ref[pl.ds(c, half, stride=2 * ROW_TILES), :].astype(BF16)
                wl_ref[rows, cols] = wt_ref[pl.ds(ROW_TILES + c, half, stride=2 * ROW_TILES), :].astype(BF16)
            w2b_ref[rows, :] = w2_ref[slot, rows, :].astype(BF16)

    @pl.when(nv == 0)
    def _():
        yg_ref[...] = jnp.zeros_like(yg_ref)

    @pl.when(nv > 0)
    def _():
        x = jnp.concatenate([xg_ref[pl.ds(c, tmg, stride=SUBLANES), :] for c in range(ROW_TILES)], axis=-1)
        x = x.astype(BF16)
        glu = jnp.minimum(_dot_nt(x, wg_ref[...]) + b1_ref[0, 0:1, :], SWIGLU_LIMIT)
        lin = jnp.clip(_dot_nt(x, wl_ref[...]) + b1_ref[0, 1:2, :], -SWIGLU_LIMIT, SWIGLU_LIMIT)
        a = glu * (1.0 / (1.0 + jnp.exp(-SWIGLU_ALPHA * glu))) * (lin + 1.0)
        y = _dot(a.astype(BF16), w2b_ref[...]) + b2_ref[0]
        for c in range(ROW_TILES):
            yg_ref[pl.ds(c, tmg, stride=SUBLANES), :] = y[:, c * LANES:(c + 1) * LANES]


def _experts(te, nv, xg, layer, w1, b1, w2, b2):
    tmg = TM_GROUP
    n_tiles = xg.shape[0] // (tmg * SUBLANES)
    tiles = jnp.arange(n_tiles, dtype=jnp.int32)
    differs = jnp.logical_and(tiles[None, :] > tiles[:, None], te[None, :] != te[:, None])
    nxt_tile = jnp.min(jnp.where(differs, tiles[None, :], n_tiles), axis=1)
    nxt = jnp.where(nxt_tile < n_tiles, te[jnp.minimum(nxt_tile, n_tiles - 1)], -1).astype(jnp.int32)
    run = jnp.cumsum(jnp.concatenate([jnp.zeros((1,), jnp.int32), (te[1:] != te[:-1]).astype(jnp.int32)]))
    per_tile = lambda j, te, *_: (j, 0)
    per_expert = lambda j, te, *_: (te[j], 0, 0)
    grid_spec = pltpu.PrefetchScalarGridSpec(
        num_scalar_prefetch=4,
        grid=(n_tiles,),
        in_specs=[pl.BlockSpec((tmg * SUBLANES, LANES), per_tile),
                  pl.BlockSpec(memory_space=pl.ANY),
                  pl.BlockSpec((1, 2, D_EXPERT), per_expert),
                  pl.BlockSpec(memory_space=pl.ANY),
                  pl.BlockSpec((1, 1, D_MODEL), per_expert)],
        out_specs=pl.BlockSpec((tmg * SUBLANES, LANES), per_tile),
        scratch_shapes=[pltpu.VMEM((D_EXPERT, D_MODEL), BF16), pltpu.VMEM((D_EXPERT, D_MODEL), BF16),
                        pltpu.VMEM((D_EXPERT, D_MODEL), BF16), pltpu.VMEM((2 * LANES * ROW_TILES, LANES), F32),
                        pltpu.VMEM((2, D_MODEL, 2 * D_EXPERT), F32), pltpu.VMEM((2, D_EXPERT, D_MODEL), F32),
                        pltpu.SemaphoreType.DMA((2, 2))],
    )
    return pl.pallas_call(
        functools.partial(_expert_kernel, layer),
        grid_spec=grid_spec,
        out_shape=jax.ShapeDtypeStruct(xg.shape, F32),
        compiler_params=_cparams(1, EXPERT_VMEM_LIMIT),
        name="moe_experts",
    )(te, nv, nxt, run.astype(jnp.int32), xg, w1, b1, w2, b2)


def _combine_kernel(pos_ref, pos_next_ref, gate_ref, x1_ref, mod_ref, yg_ref, o_ref, buf_ref, sem):
    tm = x1_ref.shape[0]
    i = pl.program_id(0)
    slot = i % 2

    @pl.when(i == 0)
    def _():
        _gather_rows(yg_ref, pos_ref, buf_ref, sem, 0, tm)

    @pl.when(i + 1 < pl.num_programs(0))
    def _():
        _gather_rows(yg_ref, pos_next_ref, buf_ref, sem, 1 - slot, tm)

    _wait_rows(buf_ref, sem, slot)
    _combine_rows(gate_ref, buf_ref, slot, tm, x1_ref, mod_ref[0, 5:6, :], o_ref)


def _combine(pos, gates, x1, mod3, yg, seq):
    t = x1.shape[0]
    tm = TM_COMBINE
    tiles_per_seq = seq // tm
    n = t // tm
    grid_spec = pltpu.PrefetchScalarGridSpec(
        num_scalar_prefetch=0,
        grid=(n,),
        in_specs=[pl.BlockSpec((tm * TOP_K,), lambda i: (i,), memory_space=pltpu.SMEM),
                  pl.BlockSpec((tm * TOP_K,), lambda i: (jnp.minimum(i + 1, n - 1),), memory_space=pltpu.SMEM),
                  pl.BlockSpec((tm, LANES), lambda i: (i, 0)),
                  pl.BlockSpec((tm, D_MODEL), lambda i: (i, 0)),
                  pl.BlockSpec((1, N_MOD, D_MODEL), lambda i: (i // tiles_per_seq, 0, 0)),
                  pl.BlockSpec(memory_space=pl.ANY)],
        out_specs=pl.BlockSpec((tm, D_MODEL), lambda i: (i, 0)),
        scratch_shapes=[pltpu.VMEM((2, TOP_K, tm * SUBLANES, LANES), F32), pltpu.SemaphoreType.DMA((2,))],
    )
    return pl.pallas_call(
        _combine_kernel,
        grid_spec=grid_spec,
        out_shape=jax.ShapeDtypeStruct((t, D_MODEL), F32),
        compiler_params=_cparams(1),
        name="moe_combine",
    )(pos, pos, gates, x1, mod3, yg)


def _block_diag_ones(n):
    i = np.arange(n)
    return jnp.asarray((i[:, None] // HEAD_DIM) == (i[None, :] // HEAD_DIM), dtype=BF16)


def _tri(n, strict, upper):
    i = np.arange(n)
    if upper:
        m = (i[:, None] > i[None, :]) if strict else (i[:, None] >= i[None, :])
        return jnp.asarray(m, dtype=BF16)
    m = (i[None, :] < i[:, None]) if strict else (i[None, :] <= i[:, None])
    return jnp.asarray(m, dtype=BF16)


def _tile2(g):
    return jnp.tile(g, SB_DIM // HEAD_DIM)


def _layer(x_parts, c, seq, layer, last, w_mlp1, w_mlp2, norm1_g, w_ada, b_ada, w_in, conv_w, sb_q_g, sb_k_g, fox_q_g, fox_k_g, fox_f_b,
           out_norm_g, w_out, norm2_g, router_w, router_b, b_mlp1, b_mlp2):
    t = x_parts[0].shape[0] if len(x_parts) == 1 else x_parts[2].shape[0]
    nb = t // seq
    scale = HEAD_DIM ** -0.5

    w_in_p = jnp.pad(w_in, ((0, 0), (0, D_IN_PAD - w_in.shape[1]))).astype(BF16)
    fb = jnp.pad(fox_f_b, (0, LANES - FOX_HEADS)).reshape(1, LANES)
    qscale = scale * LOG2_E
    gains = jnp.stack([_tile2(sb_q_g) * qscale, _tile2(sb_k_g), _tile2(fox_q_g) * qscale, _tile2(fox_k_g)])
    gains = jnp.pad(gains, ((0, SUBLANES - 4), (0, 0)))
    rw = jnp.pad(router_w, ((0, 0), (0, LANES - N_EXPERTS)))
    rw_hi = rw.astype(BF16)
    rw = jnp.concatenate([rw_hi, (rw - rw_hi.astype(F32)).astype(BF16)], axis=1)
    rb = jnp.pad(router_b, (0, LANES - N_EXPERTS)).reshape(1, LANES)
    b1 = jnp.stack([b_mlp1[:, 0::2], b_mlp1[:, 1::2]], axis=1)
    b2 = b_mlp2.reshape(N_EXPERTS, 1, D_MODEL)

    mod3 = _modulation(c, w_ada, b_ada).reshape(nb, N_MOD, D_MODEL)

    outs = _inproj(x_parts, mod3, norm1_g.reshape(1, D_MODEL), w_in_p, _block_diag_ones(SB_DIM),
                   _tri(TM_PROJ, strict=False, upper=False), fb, gains, seq)
    x2 = x_parts[0] if len(x_parts) == 1 else outs[0]
    cb, v, sbq, sbk, sbv, fq, fk, fv, cum = outs[-9:]

    r3 = lambda a: a.reshape(nb, seq, SB_DIM)
    u = _tri(TB, strict=True, upper=True)
    zu = jnp.zeros_like(u)
    uu = jnp.concatenate([jnp.concatenate([u, zu], axis=1), jnp.concatenate([zu, u], axis=1)], axis=0)
    ysb = _sb_attention(r3(sbq), r3(sbk), r3(sbv), uu)

    n_kb = seq // TB
    f2 = cum[:, :FOX_HEADS, :].reshape(nb, FOX_DIM // LANES, 2, n_kb, TB).transpose(0, 1, 3, 2, 4)
    f2 = f2.reshape(nb, FOX_DIM // LANES, n_kb, 2 * TB)
    fend = cum[:, :FOX_HEADS, TB - 1::TB].reshape(nb * FOX_HEADS * n_kb)
    zbound = (2.0 * 1.02 * HEAD_DIM * qscale * jnp.max(jnp.abs(fox_q_g)) * jnp.max(jnp.abs(fox_k_g))).reshape(1)
    yfox = _fox_attention(r3(fq), r3(fk), r3(fv), f2, fend, zbound)

    x1, h2t, sel, gates, cnt = _outproj(
        x2, cb, v, ysb.reshape(t, SB_DIM), yfox.reshape(t, FOX_DIM), conv_w, out_norm_g.reshape(1, D_MIX),
        w_out.astype(BF16), mod3, norm2_g.reshape(1, D_MODEL), rw, rb, _block_diag_ones(2 * LANES),
        _tri(TM_PROJ, strict=True, upper=False), seq)

    counts = cnt[0, :N_EXPERTS].astype(jnp.int32)
    padded = ((counts + TM_GROUP - 1) // TM_GROUP) * TM_GROUP
    off_end = jnp.cumsum(padded)
    off = off_end - padded
    n_rows = t * TOP_K + N_EXPERTS * TM_GROUP
    n_tiles = n_rows // TM_GROUP
    start = jnp.arange(n_tiles, dtype=jnp.int32) * TM_GROUP
    te = jnp.minimum(jnp.sum(start[:, None] >= off_end[None, :], axis=1), N_EXPERTS - 1).astype(jnp.int32)
    nv = jnp.clip(off[te] + counts[te] - start, 0, TM_GROUP).astype(jnp.int32)
    meta = jnp.concatenate([off, off + counts, padded - counts, off_end[-1:] // TM_GROUP]).astype(jnp.int32)

    assert t <= 1 << RANK_BITS
    words = sel[:, :TOP_K].reshape(t * TOP_K // LANES, LANES)
    picked = jnp.where((words >> RANK_BITS)[..., None] == jnp.arange(N_EXPERTS), off, 0).sum(-1)
    pos = (picked + (words & ((1 << RANK_BITS) - 1))).astype(jnp.int32).reshape(t * TOP_K)
    xg = _dispatch(meta, pos, h2t, n_rows)
    yg = _experts(te, nv, xg, layer, w_mlp1, b1, w_mlp2, b2)
    return _combine(pos, gates, x1, mod3, yg, seq) if last else (pos, gates, x1, mod3, yg)


def kernel(x, c, norm1_g, w_ada, b_ada, w_in, conv_w, sb_q_g, sb_k_g, fox_q_g, fox_k_g, fox_f_b, out_norm_g, w_out,
           norm2_g, router_w, router_b, w_mlp1, b_mlp1, w_mlp2, b_mlp2):
    b, s, d = x.shape
    x_parts = (x.reshape(b * s, d),)
    params = (norm1_g, w_ada, b_ada, w_in, conv_w, sb_q_g, sb_k_g, fox_q_g, fox_k_g, fox_f_b, out_norm_g, w_out,
              norm2_g, router_w, router_b, b_mlp1, b_mlp2)
    depth = norm1_g.shape[0]
    for layer in range(depth):
        x_parts = _layer(x_parts, c, s, layer, layer == depth - 1, w_mlp1, w_mlp2, *(p[layer] for p in params))
    return x_parts.reshape(b, s, d)
```

```python
import functools

import numpy as np
import jax
import jax.numpy as jnp
from jax import lax
from jax.experimental import pallas as pl
from jax.experimental.pallas import tpu as pltpu

F32 = jnp.float32
BF16 = jnp.bfloat16

D_MODEL = 1024
HEAD_DIM = 64
CONV_CH = 256
CONV_W = 3
SB_DIM = 384
FOX_DIM = 384
FOX_HEADS = 6
N_PAIRS = 3
D_MIX = 1024
N_MOD = 6
N_EXPERTS = 32
TOP_K = 4
D_EXPERT = 1024
SWIGLU_ALPHA = 1.702
SWIGLU_LIMIT = 7.0
EPS = 1e-6

LANES = 128
SUBLANES = 8
ROW_TILES = D_MODEL // LANES
D_IN_PAD = 3 * CONV_CH + 3 * SB_DIM + 3 * FOX_DIM + LANES
COL_SBQ = 3 * CONV_CH
COL_SBK = COL_SBQ + SB_DIM
COL_SBV = COL_SBK + SB_DIM
COL_FQ = COL_SBV + SB_DIM
COL_FK = COL_FQ + FOX_DIM
COL_FV = COL_FK + FOX_DIM
COL_FLOG = COL_FV + FOX_DIM

TM_PROJ = 512
TQ = 256
TB = 128
TM_DISPATCH = 1024
TM_COMBINE = 256
TM_GROUP = 512
RANK_BITS = 20
LOG2_E = 1.4426950408889634
LOG2_NEGLIGIBLE = -40.0
VMEM_LIMIT = 48 * 1024 * 1024
INPROJ_VMEM_LIMIT = 56 * 1024 * 1024
EXPERT_VMEM_LIMIT = 58 * 1024 * 1024


def _cparams(n_axes, vmem=VMEM_LIMIT):
    return pltpu.CompilerParams(dimension_semantics=("arbitrary",) * n_axes, vmem_limit_bytes=vmem)


def _dot(a, b):
    return jnp.dot(a, b, preferred_element_type=F32)


def _dot_nt(a, b):
    return lax.dot_general(a, b, (((1,), (1,)), ((), ())), preferred_element_type=F32)


def _split2(x):
    hi = x.astype(BF16)
    lo = (x - hi.astype(F32)).astype(BF16)
    return hi, lo


def _row_copy(src, src_row, dst, dst_row, sem):
    return pltpu.make_async_copy(src.at[pl.ds(pl.multiple_of(src_row * SUBLANES, SUBLANES), SUBLANES), :],
                                 dst.at[pl.ds(pl.multiple_of(dst_row * SUBLANES, SUBLANES), SUBLANES), :], sem)


def _log_sigmoid(x):
    return jnp.minimum(x, 0.0) - jnp.log(1.0 + jnp.exp(-jnp.abs(x)))


def _mod_kernel(c_ref, w_ref, b_ref, o_ref):
    c = c_ref[...]
    s = c / (1.0 + jnp.exp(-c))
    o_ref[...] = _dot(s.astype(BF16), w_ref[...].astype(BF16)) + b_ref[...]


def _modulation(c, w_ada, b_ada):
    b, d = c.shape
    n = w_ada.shape[1]
    tn = 1024
    return pl.pallas_call(
        _mod_kernel,
        grid=(n // tn,),
        in_specs=[pl.BlockSpec((b, d), lambda j: (0, 0)),
                  pl.BlockSpec((d, tn), lambda j: (0, j)),
                  pl.BlockSpec((1, tn), lambda j: (0, j))],
        out_specs=pl.BlockSpec((b, tn), lambda j: (0, j)),
        out_shape=jax.ShapeDtypeStruct((b, n), F32),
        compiler_params=_cparams(1),
        name="modulation",
    )(c, w_ada, b_ada.reshape(1, n))


def _gather_rows(yg_ref, idx_ref, buf_ref, sem, slot, n_rows):
    def issue(r, c):
        for k in range(TOP_K):
            _row_copy(yg_ref, idx_ref[r * TOP_K + k], buf_ref.at[slot, k], r, sem.at[slot]).start()
        return c

    lax.fori_loop(0, n_rows, issue, 0)


def _combine_rows(gate_ref, buf_ref, slot, n_rows, x1_ref, g2, out_ref):
    gates = gate_ref[...]
    gate_cols = [gates[:, k:k + 1] for k in range(TOP_K)]
    for c in range(ROW_TILES):
        acc = jnp.zeros((n_rows, LANES), F32)
        for k in range(TOP_K):
            acc = acc + gate_cols[k] * buf_ref[slot, k, pl.ds(c, n_rows, stride=SUBLANES), :]
        cols = slice(c * LANES, (c + 1) * LANES)
        out_ref[:, cols] = x1_ref[:, cols] + g2[:, cols] * acc


def _wait_rows(buf_ref, sem, slot):
    for k in range(TOP_K):
        pltpu.make_async_copy(buf_ref.at[slot, k], buf_ref.at[slot, k], sem.at[slot]).wait()


def _inproj_kernel(tiles_per_seq, fused, *refs):
    if fused:
        (pos_ref, pos_next_ref, gate_ref, x1_ref, modp_ref, yg_ref, mod_ref, g1_ref, w_ref, bd_ref, tri_ref, fb_ref,
         gains_ref, x_out_ref, cb_ref, v_ref, sbq_ref, sbk_ref, sbv_ref, fq_ref, fk_ref, fv_ref, cum_ref,
         carry_ref, x_ref, buf_ref, sem) = refs
        tm = x1_ref.shape[0]
        i = pl.program_id(0)
        slot = i % 2

        @pl.when(i == 0)
        def _():
            _gather_rows(yg_ref, pos_ref, buf_ref, sem, 0, tm)

        @pl.when(i + 1 < pl.num_programs(0))
        def _():
            _gather_rows(yg_ref, pos_next_ref, buf_ref, sem, 1 - slot, tm)

        _wait_rows(buf_ref, sem, slot)
        _combine_rows(gate_ref, buf_ref, slot, tm, x1_ref, modp_ref[0, 5:6, :], x_ref)
        x_out_ref[...] = x_ref[...]
    else:
        (x_ref, mod_ref, g1_ref, w_ref, bd_ref, tri_ref, fb_ref, gains_ref,
         cb_ref, v_ref, sbq_ref, sbk_ref, sbv_ref, fq_ref, fk_ref, fv_ref, cum_ref, carry_ref) = refs
        tm = x_ref.shape[0]
    x = x_ref[...]
    ms = jnp.mean(x * x, axis=-1, keepdims=True)
    h = (x * lax.rsqrt(ms + EPS)) * g1_ref[...]
    h = h * (1.0 + mod_ref[0, 1:2, :]) + mod_ref[0, 0:1, :]
    hb = h.astype(BF16)

    pc = _dot(hb, w_ref[:, 0:3 * CONV_CH])
    cb_ref[...] = pc[:, 0:CONV_CH]
    v_ref[...] = pc[:, CONV_CH:2 * CONV_CH] * pc[:, 2 * CONV_CH:3 * CONV_CH]

    normed = ((COL_SBQ, sbq_ref), (COL_SBK, sbk_ref), (COL_FQ, fq_ref), (COL_FK, fk_ref))
    proj = [_dot(hb, w_ref[:, col:col + SB_DIM]) for col, _ in normed]
    sbv_ref[...] = _dot(hb, w_ref[:, COL_SBV:COL_SBV + SB_DIM]).astype(BF16)
    fv_ref[...] = _dot(hb, w_ref[:, COL_FV:COL_FV + FOX_DIM]).astype(BF16)
    flog = _dot(hb, w_ref[:, COL_FLOG:COL_FLOG + LANES])
    def head_sumsq(p):
        sq = (p * p).astype(BF16)
        wide = 2 * LANES
        return jnp.concatenate([_dot(sq[:, :wide], bd_ref[:wide, :wide]), _dot(sq[:, wide:], bd_ref[wide:, wide:])], axis=-1)

    sumsq = [head_sumsq(p) for p in proj]
    for row, ((_, out_ref), p, ss) in enumerate(zip(normed, proj, sumsq)):
        out_ref[...] = (p * lax.rsqrt(ss * (1.0 / HEAD_DIM) + EPS) * gains_ref[row:row + 1, :]).astype(BF16)

    lf = _log_sigmoid(flog + fb_ref[...]) * LOG2_E
    t0 = lf.astype(BF16)
    r1 = lf - t0.astype(F32)
    t1 = r1.astype(BF16)
    t2 = (r1 - t1.astype(F32)).astype(BF16)
    tri = tri_ref[...]
    cs = _dot(tri, t0) + _dot(tri, t1) + _dot(tri, t2)

    @pl.when(pl.program_id(0) % tiles_per_seq == 0)
    def _():
        carry_ref[...] = jnp.zeros_like(carry_ref)

    cum = cs + carry_ref[0:1, :]
    carry_ref[0:1, :] = cum[tm - 1:tm, :]
    cum_ref[0] = cum.T[0:SUBLANES, :]


def _inproj(x_parts, mod3, g1, w_in_p, bd, tri, fb, gains, seq):
    fused = len(x_parts) > 1
    t = x_parts[2].shape[0] if fused else x_parts[0].shape[0]
    tm = TM_PROJ
    n = t // tm
    tiles_per_seq = seq // tm
    nb = t // seq
    row = lambda i: (i, 0)
    const = lambda i: (0, 0)
    per_seq = lambda i: (i // tiles_per_seq, 0, 0)
    outs = [jax.ShapeDtypeStruct((t, CONV_CH), F32), jax.ShapeDtypeStruct((t, CONV_CH), F32)]
    outs += [jax.ShapeDtypeStruct((t, SB_DIM), BF16)] * 6
    outs += [jax.ShapeDtypeStruct((nb, SUBLANES, seq), F32)]
    out_specs = [pl.BlockSpec((tm, CONV_CH), row)] * 2 + [pl.BlockSpec((tm, SB_DIM), row)] * 6
    out_specs += [pl.BlockSpec((1, SUBLANES, tm), lambda i: (i // tiles_per_seq, 0, i % tiles_per_seq))]
    in_specs = [pl.BlockSpec((1, N_MOD, D_MODEL), per_seq),
                pl.BlockSpec((1, D_MODEL), const),
                pl.BlockSpec((D_MODEL, D_IN_PAD), const),
                pl.BlockSpec((SB_DIM, SB_DIM), const),
                pl.BlockSpec((tm, tm), const),
                pl.BlockSpec((1, LANES), const),
                pl.BlockSpec((SUBLANES, SB_DIM), const)]
    scratch = [pltpu.VMEM((SUBLANES, LANES), F32)]
    if fused:
        pos, gates, x1, mod3_prev, yg = x_parts
        operands = (pos, pos, gates, x1, mod3_prev, yg)
        in_specs = [pl.BlockSpec((tm * TOP_K,), lambda i: (i,), memory_space=pltpu.SMEM),
                    pl.BlockSpec((tm * TOP_K,), lambda i: (jnp.minimum(i + 1, n - 1),), memory_space=pltpu.SMEM),
                    pl.BlockSpec((tm, LANES), row),
                    pl.BlockSpec((tm, D_MODEL), row),
                    pl.BlockSpec((1, N_MOD, D_MODEL), per_seq),
                    pl.BlockSpec(memory_space=pl.ANY)] + in_specs
        outs = [jax.ShapeDtypeStruct((t, D_MODEL), F32)] + outs
        out_specs = [pl.BlockSpec((tm, D_MODEL), row)] + out_specs
        scratch += [pltpu.VMEM((tm, D_MODEL), F32), pltpu.VMEM((2, TOP_K, tm * SUBLANES, LANES), F32),
                    pltpu.SemaphoreType.DMA((2,))]
    else:
        operands = x_parts
        in_specs = [pl.BlockSpec((tm, D_MODEL), row)] + in_specs
    return pl.pallas_call(
        functools.partial(_inproj_kernel, tiles_per_seq, fused),
        grid=(n,),
        in_specs=in_specs,
        out_specs=out_specs,
        out_shape=outs,
        scratch_shapes=scratch,
        compiler_params=_cparams(1, INPROJ_VMEM_LIMIT),
        name="inproj",
    )(*operands, mod3, g1, w_in_p, bd, tri, fb, gains)


def _stack_heads(k_ref, v_ref, kk_ref, vv_ref):
    n_blocks = kk_ref.shape[1]
    lane = lax.broadcasted_iota(jnp.int32, (TB, LANES), 1)
    first = lane < HEAD_DIM

    def body(j, c):
        rows = pl.ds(pl.multiple_of(j * TB, TB), TB)
        for src, dst in ((k_ref, kk_ref), (v_ref, vv_ref)):
            for p in range(N_PAIRS):
                blk = src[0, rows, p * LANES:(p + 1) * LANES]
                zero = jnp.zeros_like(blk)
                dst[p, j, 0:TB, :] = jnp.where(first, blk, zero)
                dst[p, j, TB:2 * TB, :] = jnp.where(first, zero, blk)
        return c

    lax.fori_loop(0, n_blocks, body, 0)


def _per_head(a0, a1):
    return jnp.concatenate([jnp.broadcast_to(a0, (TB, TB)), jnp.broadcast_to(a1, (TB, TB))], axis=1)


def _pair_iota():
    row = lax.broadcasted_iota(jnp.int32, (TB, 2 * TB), 0)
    col = lax.broadcasted_iota(jnp.int32, (TB, 2 * TB), 1) % TB
    return row, col


_CHAINS = [(p, s) for p in range(N_PAIRS) for s in range(2)]


def _q_block(q_ref, p, s):
    return q_ref[0, s * TB:(s + 1) * TB, p * LANES:(p + 1) * LANES]


def _sb_kernel(q_ref, k_ref, v_ref, uu_ref, o_ref, kk_ref, vv_ref):
    qi = pl.program_id(1)

    @pl.when(qi == 0)
    def _():
        _stack_heads(k_ref, v_ref, kk_ref, vv_ref)

    uu = uu_ref[...]
    row, col = _pair_iota()
    strict = col < row

    def sweep(states, backs):
        items = [(c, b) for b in range(len(backs)) for c in range(len(_CHAINS))]
        kbs, keeps, z = {}, {}, {}
        for c, b in items:
            p, s = _CHAINS[c]
            back = backs[b]
            kb = 2 * qi + s - back
            if isinstance(back, int) and back == 0:
                keeps[c, b] = strict
            elif isinstance(back, int) and back == 1 and s == 1:
                keeps[c, b] = None
            else:
                keeps[c, b] = kb >= 0
            kbs[c, b] = jnp.maximum(kb, 0)
            z[c, b] = _dot_nt(_q_block(q_ref, p, s), kk_ref[p, kbs[c, b]])
        log_beta, log_1mb, sums = {}, {}, {}
        for it in items:
            log_beta[it] = jnp.minimum(z[it], 0.0) - jnp.log2(1.0 + jnp.exp2(-jnp.abs(z[it])))
            l1 = log_beta[it] - z[it]
            log_1mb[it] = l1 if keeps[it] is None else jnp.where(keeps[it], l1, 0.0)
            sums[it] = (jnp.sum(log_1mb[it][:, :TB], axis=-1, keepdims=True),
                        jnp.sum(log_1mb[it][:, TB:], axis=-1, keepdims=True))
        later = {}
        for it in items:
            hi, lo = _split2(log_1mb[it])
            later[it] = _dot(hi, uu) + _dot(lo, uu)
        states = list(states)
        for it in items:
            c = it[0]
            rs0, rs1, acc = states[c]
            w = jnp.exp2(log_beta[it] + later[it] + _per_head(rs0, rs1))
            if keeps[it] is not None:
                w = jnp.where(keeps[it], w, 0.0)
            acc = acc + _dot(w.astype(BF16), vv_ref[_CHAINS[c][0], kbs[it]])
            states[c] = (rs0 + sums[it][0], rs1 + sums[it][1], acc)
        return states

    def more(states):
        top = states[0][0]
        for st in states:
            top = jnp.maximum(top, jnp.maximum(st[0], st[1]))
        return (jnp.max(top) > LOG2_NEGLIGIBLE).astype(jnp.int32)

    zero = jnp.zeros((TB, 1), F32)
    states = sweep([(zero, zero, jnp.zeros((TB, LANES), F32))] * len(_CHAINS), (0, 1))

    def cond(c):
        return jnp.logical_and(2 * qi + 1 - c[0] >= 0, c[2] > 0)

    def body(c):
        back, states, _ = c
        states = sweep(states, (back,))
        return back + 1, states, more(states)

    _, states, _ = lax.while_loop(cond, body, (jnp.int32(2), states, more(states)))
    for (p, s), st in zip(_CHAINS, states):
        o_ref[0, s * TB:(s + 1) * TB, p * LANES:(p + 1) * LANES] = st[2]


def _sb_attention(q, k, v, uu):
    b, s, d = q.shape
    tq = TQ
    qspec = pl.BlockSpec((1, tq, d), lambda bi, qi: (bi, qi, 0))
    kvspec = pl.BlockSpec((1, s, d), lambda bi, qi: (bi, 0, 0))
    scratch = pltpu.VMEM((N_PAIRS, s // TB, 2 * TB, LANES), BF16)
    return pl.pallas_call(
        _sb_kernel,
        grid=(b, s // tq),
        in_specs=[qspec, kvspec, kvspec, pl.BlockSpec((2 * TB, 2 * TB), lambda bi, qi: (0, 0))],
        out_specs=qspec,
        out_shape=jax.ShapeDtypeStruct((b, s, d), F32),
        scratch_shapes=[scratch, scratch],
        compiler_params=_cparams(2),
        name="sb_attention",
    )(q, k, v, uu)


def _fox_kernel(steps_ref, q_ref, k_ref, v_ref, f2_ref, o_ref, kk_ref, vv_ref, sc_ref):
    bi = pl.program_id(0)
    qi = pl.program_id(1)

    @pl.when(qi == 0)
    def _():
        _stack_heads(k_ref, v_ref, kk_ref, vv_ref)

    row, col = _pair_iota()
    causal = col <= row
    lane = lax.broadcasted_iota(jnp.int32, (TB, LANES), 1)
    first = lane < HEAD_DIM

    def scores(p, s, back, steps):
        kb = 2 * qi + s - back
        sc = _dot_nt(_q_block(q_ref, p, s), kk_ref[p, jnp.maximum(kb, 0)]) - f2_ref[0, p, pl.ds(jnp.maximum(kb, 0), 1), :]
        return jnp.where(jnp.logical_and(kb >= 0, back <= steps), sc, -1e30)

    def head_max(a):
        return (jnp.max(a[:, :TB], axis=-1, keepdims=True), jnp.max(a[:, TB:], axis=-1, keepdims=True))

    def head_sum(a):
        return (jnp.sum(a[:, :TB], axis=-1, keepdims=True), jnp.sum(a[:, TB:], axis=-1, keepdims=True))

    for p in range(N_PAIRS):
        steps = steps_ref[(bi * N_PAIRS + p) * pl.num_programs(1) + qi]
        n_it = (steps + 1) // 2

        runmax = []
        for s in range(2):
            sc = jnp.where(causal, scores(p, s, 0, steps), -1e30)
            sc_ref[s, 0] = sc
            runmax.append(sc)

        def pass1(it, runmax):
            out = list(runmax)
            for u in (1, 2):
                back = 2 * it + u
                for s in range(2):
                    sc = scores(p, s, back, steps)
                    sc_ref[s, back] = sc
                    out[s] = jnp.maximum(out[s], sc)
            return out

        runmax = lax.fori_loop(0, n_it, pass1, runmax)
        tops = [_per_head(*head_max(runmax[s])) for s in range(2)]

        def weigh(s, back, runsum, acc):
            pr = jnp.exp2(sc_ref[s, back] - tops[s])
            kb = jnp.maximum(2 * qi + s - back, 0)
            return runsum + pr, acc + _dot(pr.astype(BF16), vv_ref[p, kb])

        state = [weigh(s, 0, jnp.zeros((TB, 2 * TB), F32), jnp.zeros((TB, LANES), F32)) for s in range(2)]

        def pass2(it, state):
            out = list(state)
            for u in (1, 2):
                for s in range(2):
                    out[s] = weigh(s, 2 * it + u, *out[s])
            return out

        state = lax.fori_loop(0, n_it, pass2, state)
        for s in range(2):
            l0, l1 = head_sum(state[s][0])
            o_ref[0, s * TB:(s + 1) * TB, p * LANES:(p + 1) * LANES] = state[s][1] / jnp.where(first, l0, l1)


def _fox_attention(q, k, v, f2, steps):
    b, s, d = q.shape
    tq = TQ
    n_kb = s // TB
    qspec = pl.BlockSpec((1, tq, d), lambda bi, qi, *_: (bi, qi, 0))
    kvspec = pl.BlockSpec((1, s, d), lambda bi, qi, *_: (bi, 0, 0))
    scratch = pltpu.VMEM((N_PAIRS, n_kb, 2 * TB, LANES), BF16)
    grid_spec = pltpu.PrefetchScalarGridSpec(
        num_scalar_prefetch=1,
        grid=(b, s // tq),
        in_specs=[qspec, kvspec, kvspec,
                  pl.BlockSpec((1, N_PAIRS, n_kb, 2 * TB), lambda bi, qi, *_: (bi, 0, 0, 0))],
        out_specs=qspec,
        scratch_shapes=[scratch, scratch, pltpu.VMEM((2, n_kb + 2, TB, 2 * TB), F32)],
    )
    return pl.pallas_call(
        _fox_kernel,
        grid_spec=grid_spec,
        out_shape=jax.ShapeDtypeStruct((b, s, d), F32),
        compiler_params=_cparams(2),
        name="fox_attention",
    )(steps, q, k, v, f2)


def _outproj_kernel(tiles_per_seq, x_ref, cb_ref, v_ref, vprev_ref, ysb_ref, yfox_ref, cw_ref, og_ref, wout_ref,
                    mod_ref, g2_ref, rw_ref, rb_ref, bd_ref, ltri_ref,
                    x1_ref, h2t_ref, sel_ref, gate_ref, cnt_ref, vext_ref, carry_ref):
    tm = x_ref.shape[0]
    i = pl.program_id(0)

    @pl.when(i == 0)
    def _():
        carry_ref[...] = jnp.zeros_like(carry_ref)

    first = (i % tiles_per_seq) == 0
    vext_ref[0:SUBLANES, :] = jnp.where(first, 0.0, vprev_ref[...])
    vext_ref[SUBLANES:SUBLANES + tm, :] = v_ref[...]
    conv = (cw_ref[0:1, :] * vext_ref[SUBLANES - 2:SUBLANES - 2 + tm, :]
            + cw_ref[1:2, :] * vext_ref[SUBLANES - 1:SUBLANES - 1 + tm, :]
            + cw_ref[2:3, :] * v_ref[...])
    y = jnp.concatenate([cb_ref[...] * conv, ysb_ref[...], yfox_ref[...]], axis=-1)

    ysq = (y * y).astype(BF16)
    bd = bd_ref[...]
    ss = jnp.concatenate([_dot(ysq[:, c:c + 2 * LANES], bd) for c in range(0, D_MIX, 2 * LANES)], axis=-1)
    yn = y * lax.rsqrt(ss * (1.0 / HEAD_DIM) + EPS) * og_ref[...]
    x1 = x_ref[...] + mod_ref[0, 2:3, :] * _dot(yn.astype(BF16), wout_ref[...])
    x1_ref[...] = x1

    ms = jnp.mean(x1 * x1, axis=-1, keepdims=True)
    h2 = (x1 * lax.rsqrt(ms + EPS)) * g2_ref[...]
    h2 = h2 * (1.0 + mod_ref[0, 4:5, :]) + mod_ref[0, 3:4, :]
    for j in range(ROW_TILES):
        h2t_ref[pl.ds(j, tm, stride=SUBLANES), :] = h2[:, j * LANES:(j + 1) * LANES]

    h_hi, h_lo = _split2(h2)
    both = _dot(h_hi, rw_ref[...])
    logits = both[:, :LANES] + both[:, LANES:] + _dot(h_lo, rw_ref[:, :LANES]) + rb_ref[...]
    lane = lax.broadcasted_iota(jnp.int32, (tm, LANES), 1)
    lane_f = lane.astype(F32)
    neg = jnp.float32(-jnp.inf)
    cur = jnp.where(lane < N_EXPERTS, logits, neg)
    tops, sels = [], []
    for _ in range(TOP_K):
        mk = jnp.max(cur, axis=-1, keepdims=True)
        ik = jnp.min(jnp.where(cur == mk, lane_f, float(LANES)), axis=-1, keepdims=True)
        sel = lane_f == ik
        cur = jnp.where(sel, neg, cur)
        tops.append(mk)
        sels.append((sel, ik.astype(jnp.int32)))
    es = [jnp.exp(t - tops[0]) for t in tops]
    inv = 1.0 / (es[0] + es[1] + es[2] + es[3])

    multi = jnp.zeros((tm, LANES), F32)
    for sel, _ in sels:
        multi = jnp.where(sel, 1.0, multi)
    before = _dot(ltri_ref[...], multi.astype(BF16)) + carry_ref[0:1, :]
    carry_ref[0:1, :] = before[tm - 1:tm, :] + multi[tm - 1:tm, :]

    sel_out = jnp.zeros((tm, LANES), jnp.int32)
    gate_out = jnp.zeros((tm, LANES), F32)
    for k, (sel, ik) in enumerate(sels):
        rank = jnp.sum(jnp.where(sel, before, 0.0), axis=-1, keepdims=True).astype(jnp.int32)
        sel_out = jnp.where(lane == k, ik * (1 << RANK_BITS) + rank, sel_out)
        gate_out = jnp.where(lane == k, es[k] * inv, gate_out)
    sel_ref[...] = sel_out
    gate_ref[...] = gate_out
    cnt_ref[...] = jnp.broadcast_to(carry_ref[0:1, :], cnt_ref.shape)


def _outproj(x2, cb, v, ysb, yfox, conv_w, og, w_out_b, mod3, g2, rw, rb, bd2, ltri, seq):
    t = x2.shape[0]
    tm = TM_PROJ
    tiles_per_seq = seq // tm
    row = lambda i: (i, 0)
    const = lambda i: (0, 0)
    return pl.pallas_call(
        functools.partial(_outproj_kernel, tiles_per_seq),
        grid=(t // tm,),
        in_specs=[pl.BlockSpec((tm, D_MODEL), row),
                  pl.BlockSpec((tm, CONV_CH), row),
                  pl.BlockSpec((tm, CONV_CH), row),
                  pl.BlockSpec((SUBLANES, CONV_CH), lambda i: (jnp.maximum(i * (tm // SUBLANES) - 1, 0), 0)),
                  pl.BlockSpec((tm, SB_DIM), row),
                  pl.BlockSpec((tm, FOX_DIM), row),
                  pl.BlockSpec((CONV_W, CONV_CH), const),
                  pl.BlockSpec((1, D_MIX), const),
                  pl.BlockSpec((D_MIX, D_MODEL), const),
                  pl.BlockSpec((1, N_MOD, D_MODEL), lambda i: (i // tiles_per_seq, 0, 0)),
                  pl.BlockSpec((1, D_MODEL), const),
                  pl.BlockSpec((D_MODEL, 2 * LANES), const),
                  pl.BlockSpec((1, LANES), const),
                  pl.BlockSpec((2 * LANES, 2 * LANES), const),
                  pl.BlockSpec((tm, tm), const)],
        out_specs=[pl.BlockSpec((tm, D_MODEL), row),
                   pl.BlockSpec((tm * SUBLANES, LANES), row),
                   pl.BlockSpec((tm, LANES), row),
                   pl.BlockSpec((tm, LANES), row),
                   pl.BlockSpec((SUBLANES, LANES), const)],
        out_shape=[jax.ShapeDtypeStruct((t, D_MODEL), F32),
                   jax.ShapeDtypeStruct((t * SUBLANES, LANES), F32),
                   jax.ShapeDtypeStruct((t, LANES), jnp.int32),
                   jax.ShapeDtypeStruct((t, LANES), F32),
                   jax.ShapeDtypeStruct((SUBLANES, LANES), F32)],
        scratch_shapes=[pltpu.VMEM((tm + SUBLANES, CONV_CH), F32), pltpu.VMEM((SUBLANES, LANES), F32)],
        compiler_params=_cparams(1),
        name="outproj_router",
    )(x2, cb, v, v, ysb, yfox, conv_w, og, w_out_b, mod3, g2, rw, rb, bd2, ltri)


def _zero_fill(meta_ref, zero_ref, xg_ref, zsem):
    n_tiles = xg_ref.shape[0] // (TM_GROUP * SUBLANES)
    zero_ref[...] = jnp.zeros_like(zero_ref)

    def sweep(wait):
        def per_expert(e, c):
            pos = meta_ref[N_EXPERTS + e]
            n = meta_ref[2 * N_EXPERTS + e]
            bit = TM_GROUP // 2
            while bit:
                @pl.when((n & bit) != 0)
                def _(pos=pos, bit=bit):
                    cp = pltpu.make_async_copy(
                        zero_ref.at[pl.ds(0, bit * SUBLANES), :],
                        xg_ref.at[pl.ds(pl.multiple_of(pos * SUBLANES, SUBLANES), bit * SUBLANES), :], zsem)
                    cp.wait() if wait else cp.start()
                pos = pos + (n & bit)
                bit //= 2
            return c

        lax.fori_loop(0, N_EXPERTS, per_expert, 0)

        def per_tile(j, c):
            rows = TM_GROUP * SUBLANES
            cp = pltpu.make_async_copy(zero_ref, xg_ref.at[pl.ds(pl.multiple_of(j * rows, rows), rows), :], zsem)
            cp.wait() if wait else cp.start()
            return c

        lax.fori_loop(meta_ref[3 * N_EXPERTS], n_tiles, per_tile, 0)

    sweep(False)
    sweep(True)


def _dispatch_kernel(meta_ref, pos_ref, h_ref, xg_ref, zero_ref, sem, zsem):
    tm = h_ref.shape[0] // SUBLANES

    @pl.when(pl.program_id(0) == 0)
    def _():
        _zero_fill(meta_ref, zero_ref, xg_ref, zsem)

    def issue(r, c):
        for k in range(TOP_K):
            _row_copy(h_ref, r, xg_ref, pos_ref[r * TOP_K + k], sem).start()
        return c

    lax.fori_loop(0, tm, issue, 0)
    n = tm * TOP_K * SUBLANES
    pltpu.make_async_copy(xg_ref.at[pl.ds(0, n), :], xg_ref.at[pl.ds(0, n), :], sem).wait()


def _dispatch(meta, pos, h2t, n_rows):
    t = h2t.shape[0] // SUBLANES
    tm = TM_DISPATCH
    grid_spec = pltpu.PrefetchScalarGridSpec(
        num_scalar_prefetch=1,
        grid=(t // tm,),
        in_specs=[pl.BlockSpec((tm * TOP_K,), lambda i, *_: (i,), memory_space=pltpu.SMEM),
                  pl.BlockSpec((tm * SUBLANES, LANES), lambda i, *_: (i, 0))],
        out_specs=pl.BlockSpec(memory_space=pl.ANY),
        scratch_shapes=[pltpu.VMEM((TM_GROUP * SUBLANES, LANES), F32),
                        pltpu.SemaphoreType.DMA(()), pltpu.SemaphoreType.DMA(())],
    )
    return pl.pallas_call(
        _dispatch_kernel,
        grid_spec=grid_spec,
        out_shape=jax.ShapeDtypeStruct((n_rows * SUBLANES, LANES), F32),
        compiler_params=_cparams(1),
        name="moe_dispatch",
    )(meta, pos, h2t)


def _expert_kernel(layer, te_ref, nv_ref, nxt_ref, run_ref, xg_ref, w1_hbm, b1_ref, w2_hbm, b2_ref, yg_ref,
                   wg_ref, wl_ref, w2b_ref, wt_ref, w1_ref, w2_ref, wsem):
    tmg = xg_ref.shape[0] // SUBLANES
    j = pl.program_id(0)
    nv = nv_ref[j]
    slot = run_ref[j] % 2

    def weight_copies(expert, to_slot):
        return (pltpu.make_async_copy(w1_hbm.at[layer, expert], w1_ref.at[to_slot], wsem.at[to_slot, 0]),
                pltpu.make_async_copy(w2_hbm.at[layer, expert], w2_ref.at[to_slot], wsem.at[to_slot, 1]))

    @pl.when(j == 0)
    def _():
        for cp in weight_copies(te_ref[0], slot):
            cp.start()

    @pl.when(jnp.logical_or(j == 0, te_ref[j] != te_ref[jnp.maximum(j - 1, 0)]))
    def _():
        for cp in weight_copies(te_ref[j], slot):
            cp.wait()

        @pl.when(nxt_ref[j] >= 0)
        def _():
            for cp in weight_copies(nxt_ref[j], 1 - slot):
                cp.start()

        ncol = wt_ref.shape[0] // ROW_TILES
        half = ncol // 2
        for n0 in range(0, 2 * D_EXPERT, ncol):
            wt = w1_ref[slot, :, n0:n0 + ncol].T
            for c in range(ROW_TILES):
                wt_ref[pl.ds(c, ncol, stride=ROW_TILES), :] = wt[:, c * LANES:(c + 1) * LANES]
            rows = slice(n0 // 2, n0 // 2 + half)
            for c in range(ROW_TILES):
                cols = slice(c * LANES, (c + 1) * LANES)
                wg_ref[rows, cols] = wt_ref[pl.ds(c, half, stride=2 * ROW_TILES), :].astype(BF16)
                wl_ref[rows, cols] = wt_ref[pl.ds(ROW_TILES + c, half, stride=2 * ROW_TILES), :].astype(BF16)
            w2b_ref[rows, :] = w2_ref[slot, rows, :].astype(BF16)

    @pl.when(nv == 0)
    def _():
        yg_ref[...] = jnp.zeros_like(yg_ref)

    @pl.when(nv > 0)
    def _():
        x = jnp.concatenate([xg_ref[pl.ds(c, tmg, stride=SUBLANES), :] for c in range(ROW_TILES)], axis=-1)
        x = x.astype(BF16)
        glu = jnp.minimum(_dot_nt(x, wg_ref[...]) + b1_ref[0, 0:1, :], SWIGLU_LIMIT)
        lin = jnp.clip(_dot_nt(x, wl_ref[...]) + b1_ref[0, 1:2, :], -SWIGLU_LIMIT, SWIGLU_LIMIT)
        a = glu * (1.0 / (1.0 + jnp.exp(-SWIGLU_ALPHA * glu))) * (lin + 1.0)
        y = _dot(a.astype(BF16), w2b_ref[...]) + b2_ref[0]
        for c in range(ROW_TILES):
            yg_ref[pl.ds(c, tmg, stride=SUBLANES), :] = y[:, c * LANES:(c + 1) * LANES]


def _experts(te, nv, xg, layer, w1, b1, w2, b2):
    tmg = TM_GROUP
    n_tiles = xg.shape[0] // (tmg * SUBLANES)
    tiles = jnp.arange(n_tiles, dtype=jnp.int32)
    differs = jnp.logical_and(tiles[None, :] > tiles[:, None], te[None, :] != te[:, None])
    nxt_tile = jnp.min(jnp.where(differs, tiles[None, :], n_tiles), axis=1)
    nxt = jnp.where(nxt_tile < n_tiles, te[jnp.minimum(nxt_tile, n_tiles - 1)], -1).astype(jnp.int32)
    run = jnp.cumsum(jnp.concatenate([jnp.zeros((1,), jnp.int32), (te[1:] != te[:-1]).astype(jnp.int32)]))
    per_tile = lambda j, te, *_: (j, 0)
    per_expert = lambda j, te, *_: (te[j], 0, 0)
    grid_spec = pltpu.PrefetchScalarGridSpec(
        num_scalar_prefetch=4,
        grid=(n_tiles,),
        in_specs=[pl.BlockSpec((tmg * SUBLANES, LANES), per_tile),
                  pl.BlockSpec(memory_space=pl.ANY),
                  pl.BlockSpec((1, 2, D_EXPERT), per_expert),
                  pl.BlockSpec(memory_space=pl.ANY),
                  pl.BlockSpec((1, 1, D_MODEL), per_expert)],
        out_specs=pl.BlockSpec((tmg * SUBLANES, LANES), per_tile),
        scratch_shapes=[pltpu.VMEM((D_EXPERT, D_MODEL), BF16), pltpu.VMEM((D_EXPERT, D_MODEL), BF16),
                        pltpu.VMEM((D_EXPERT, D_MODEL), BF16), pltpu.VMEM((2 * LANES * ROW_TILES, LANES), F32),
                        pltpu.VMEM((2, D_MODEL, 2 * D_EXPERT), F32), pltpu.VMEM((2, D_EXPERT, D_MODEL), F32),
                        pltpu.SemaphoreType.DMA((2, 2))],
    )
    return pl.pallas_call(
        functools.partial(_expert_kernel, layer),
        grid_spec=grid_spec,
        out_shape=jax.ShapeDtypeStruct(xg.shape, F32),
        compiler_params=_cparams(1, EXPERT_VMEM_LIMIT),
        name="moe_experts",
    )(te, nv, nxt, run.astype(jnp.int32), xg, w1, b1, w2, b2)


def _combine_kernel(pos_ref, pos_next_ref, gate_ref, x1_ref, mod_ref, yg_ref, o_ref, buf_ref, sem):
    tm = x1_ref.shape[0]
    i = pl.program_id(0)
    slot = i % 2

    @pl.when(i == 0)
    def _():
        _gather_rows(yg_ref, pos_ref, buf_ref, sem, 0, tm)

    @pl.when(i + 1 < pl.num_programs(0))
    def _():
        _gather_rows(yg_ref, pos_next_ref, buf_ref, sem, 1 - slot, tm)

    _wait_rows(buf_ref, sem, slot)
    _combine_rows(gate_ref, buf_ref, slot, tm, x1_ref, mod_ref[0, 5:6, :], o_ref)


def _combine(pos, gates, x1, mod3, yg, seq):
    t = x1.shape[0]
    tm = TM_COMBINE
    tiles_per_seq = seq // tm
    n = t // tm
    grid_spec = pltpu.PrefetchScalarGridSpec(
        num_scalar_prefetch=0,
        grid=(n,),
        in_specs=[pl.BlockSpec((tm * TOP_K,), lambda i: (i,), memory_space=pltpu.SMEM),
                  pl.BlockSpec((tm * TOP_K,), lambda i: (jnp.minimum(i + 1, n - 1),), memory_space=pltpu.SMEM),
                  pl.BlockSpec((tm, LANES), lambda i: (i, 0)),
                  pl.BlockSpec((tm, D_MODEL), lambda i: (i, 0)),
                  pl.BlockSpec((1, N_MOD, D_MODEL), lambda i: (i // tiles_per_seq, 0, 0)),
                  pl.BlockSpec(memory_space=pl.ANY)],
        out_specs=pl.BlockSpec((tm, D_MODEL), lambda i: (i, 0)),
        scratch_shapes=[pltpu.VMEM((2, TOP_K, tm * SUBLANES, LANES), F32), pltpu.SemaphoreType.DMA((2,))],
    )
    return pl.pallas_call(
        _combine_kernel,
        grid_spec=grid_spec,
        out_shape=jax.ShapeDtypeStruct((t, D_MODEL), F32),
        compiler_params=_cparams(1),
        name="moe_combine",
    )(pos, pos, gates, x1, mod3, yg)


def _block_diag_ones(n):
    i = np.arange(n)
    return jnp.asarray((i[:, None] // HEAD_DIM) == (i[None, :] // HEAD_DIM), dtype=BF16)


def _tri(n, strict, upper):
    i = np.arange(n)
    if upper:
        m = (i[:, None] > i[None, :]) if strict else (i[:, None] >= i[None, :])
        return jnp.asarray(m, dtype=BF16)
    m = (i[None, :] < i[:, None]) if strict else (i[None, :] <= i[:, None])
    return jnp.asarray(m, dtype=BF16)


def _tile2(g):
    return jnp.tile(g, SB_DIM // HEAD_DIM)


def _layer(x_parts, c, seq, layer, last, w_mlp1, w_mlp2, norm1_g, w_ada, b_ada, w_in, conv_w, sb_q_g, sb_k_g, fox_q_g, fox_k_g, fox_f_b,
           out_norm_g, w_out, norm2_g, router_w, router_b, b_mlp1, b_mlp2):
    t = x_parts[0].shape[0] if len(x_parts) == 1 else x_parts[2].shape[0]
    nb = t // seq
    scale = HEAD_DIM ** -0.5

    w_in_p = jnp.pad(w_in, ((0, 0), (0, D_IN_PAD - w_in.shape[1]))).astype(BF16)
    fb = jnp.pad(fox_f_b, (0, LANES - FOX_HEADS)).reshape(1, LANES)
    qscale = scale * LOG2_E
    gains = jnp.stack([_tile2(sb_q_g) * qscale, _tile2(sb_k_g), _tile2(fox_q_g) * qscale, _tile2(fox_k_g)])
    gains = jnp.pad(gains, ((0, SUBLANES - 4), (0, 0)))
    rw = jnp.pad(router_w, ((0, 0), (0, LANES - N_EXPERTS)))
    rw_hi = rw.astype(BF16)
    rw = jnp.concatenate([rw_hi, (rw - rw_hi.astype(F32)).astype(BF16)], axis=1)
    rb = jnp.pad(router_b, (0, LANES - N_EXPERTS)).reshape(1, LANES)
    b1 = jnp.stack([b_mlp1[:, 0::2], b_mlp1[:, 1::2]], axis=1)
    b2 = b_mlp2.reshape(N_EXPERTS, 1, D_MODEL)

    mod3 = _modulation(c, w_ada, b_ada).reshape(nb, N_MOD, D_MODEL)

    outs = _inproj(x_parts, mod3, norm1_g.reshape(1, D_MODEL), w_in_p, _block_diag_ones(SB_DIM),
                   _tri(TM_PROJ, strict=False, upper=False), fb, gains, seq)
    x2 = x_parts[0] if len(x_parts) == 1 else outs[0]
    cb, v, sbq, sbk, sbv, fq, fk, fv, cum = outs[-9:]

    r3 = lambda a: a.reshape(nb, seq, SB_DIM)
    u = _tri(TB, strict=True, upper=True)
    zu = jnp.zeros_like(u)
    uu = jnp.concatenate([jnp.concatenate([u, zu], axis=1), jnp.concatenate([zu, u], axis=1)], axis=0)
    ysb = _sb_attention(r3(sbq), r3(sbk), r3(sbv), uu)

    n_kb = seq // TB
    f2 = cum[:, :FOX_HEADS, :].reshape(nb, FOX_DIM // LANES, 2, n_kb, TB).transpose(0, 1, 3, 2, 4)
    f2 = f2.reshape(nb, FOX_DIM // LANES, n_kb, 2 * TB)
    zbound = 2.0 * 1.02 * HEAD_DIM * qscale * jnp.max(jnp.abs(fox_q_g)) * jnp.max(jnp.abs(fox_k_g))
    fend = cum[:, :FOX_HEADS, TB - 1::TB]
    f_query = jnp.concatenate([fend[..., :1], fend[..., :-1]], axis=-1)
    blk = jnp.arange(n_kb)
    kept = jnp.logical_and(zbound + f_query[..., :, None] - fend[..., None, :] >= LOG2_NEGLIGIBLE,
                           blk[None, :] < blk[:, None])
    steps = kept.sum(-1).reshape(nb, N_PAIRS, 2, n_kb // 2, 2).max(axis=(2, 4))
    steps = steps.astype(jnp.int32).reshape(nb * N_PAIRS * (n_kb // 2))
    yfox = _fox_attention(r3(fq), r3(fk), r3(fv), f2, steps)

    x1, h2t, sel, gates, cnt = _outproj(
        x2, cb, v, ysb.reshape(t, SB_DIM), yfox.reshape(t, FOX_DIM), conv_w, out_norm_g.reshape(1, D_MIX),
        w_out.astype(BF16), mod3, norm2_g.reshape(1, D_MODEL), rw, rb, _block_diag_ones(2 * LANES),
        _tri(TM_PROJ, strict=True, upper=False), seq)

    counts = cnt[0, :N_EXPERTS].astype(jnp.int32)
    padded = ((counts + TM_GROUP - 1) // TM_GROUP) * TM_GROUP
    off_end = jnp.cumsum(padded)
    off = off_end - padded
    n_rows = t * TOP_K + N_EXPERTS * TM_GROUP
    n_tiles = n_rows // TM_GROUP
    start = jnp.arange(n_tiles, dtype=jnp.int32) * TM_GROUP
    te = jnp.minimum(jnp.sum(start[:, None] >= off_end[None, :], axis=1), N_EXPERTS - 1).astype(jnp.int32)
    nv = jnp.clip(off[te] + counts[te] - start, 0, TM_GROUP).astype(jnp.int32)
    meta = jnp.concatenate([off, off + counts, padded - counts, off_end[-1:] // TM_GROUP]).astype(jnp.int32)

    assert t <= 1 << RANK_BITS
    words = sel[:, :TOP_K].reshape(t * TOP_K // LANES, LANES)
    picked = jnp.where((words >> RANK_BITS)[..., None] == jnp.arange(N_EXPERTS), off, 0).sum(-1)
    pos = (picked + (words & ((1 << RANK_BITS) - 1))).astype(jnp.int32).reshape(t * TOP_K)
    xg = _dispatch(meta, pos, h2t, n_rows)
    yg = _experts(te, nv, xg, layer, w_mlp1, b1, w_mlp2, b2)
    return _combine(pos, gates, x1, mod3, yg, seq) if last else (pos, gates, x1, mod3, yg)


def kernel(x, c, norm1_g, w_ada, b_ada, w_in, conv_w, sb_q_g, sb_k_g, fox_q_g, fox_k_g, fox_f_b, out_norm_g, w_out,
           norm2_g, router_w, router_b, w_mlp1, b_mlp1, w_mlp2, b_mlp2):
    b, s, d = x.shape
    x_parts = (x.reshape(b * s, d),)
    params = (norm1_g, w_ada, b_ada, w_in, conv_w, sb_q_g, sb_k_g, fox_q_g, fox_k_g, fox_f_b, out_norm_g, w_out,
              norm2_g, router_w, router_b, b_mlp1, b_mlp2)
    depth = norm1_g.shape[0]
    for layer in range(depth):
        x_parts = _layer(x_parts, c, s, layer, layer == depth - 1, w_mlp1, w_mlp2, *(p[layer] for p in params))
    return x_parts.reshape(b, s, d)
```

```python
import functools

import numpy as np
import jax
import jax.numpy as jnp
from jax import lax
from jax.experimental import pallas as pl
from jax.experimental.pallas import tpu as pltpu

F32 = jnp.float32
BF16 = jnp.bfloat16

D_MODEL = 1024
HEAD_DIM = 64
CONV_CH = 256
CONV_W = 3
SB_DIM = 384
FOX_DIM = 384
FOX_HEADS = 6
N_PAIRS = 3
D_MIX = 1024
N_MOD = 6
N_EXPERTS = 32
TOP_K = 4
D_EXPERT = 1024
SWIGLU_ALPHA = 1.702
SWIGLU_LIMIT = 7.0
EPS = 1e-6

LANES = 128
SUBLANES = 8
ROW_TILES = D_MODEL // LANES
D_IN_PAD = 3 * CONV_CH + 3 * SB_DIM + 3 * FOX_DIM + LANES
COL_SBQ = 3 * CONV_CH
COL_SBK = COL_SBQ + SB_DIM
COL_SBV = COL_SBK + SB_DIM
COL_FQ = COL_SBV + SB_DIM
COL_FK = COL_FQ + FOX_DIM
COL_FV = COL_FK + FOX_DIM
COL_FLOG = COL_FV + FOX_DIM

MOD_TILE = 1024
TM_PROJ = 512
TQ = 256
TB = 128
TM_DISPATCH = 1024
TM_COMBINE = 256
TM_GROUP = 512
RANK_BITS = 20
LOG2_E = 1.4426950408889634
LOG2_NEGLIGIBLE = -40.0
VMEM_LIMIT = 48 * 1024 * 1024
INPROJ_VMEM_LIMIT = 56 * 1024 * 1024
EXPERT_VMEM_LIMIT = 58 * 1024 * 1024


def _cparams(n_axes, vmem=VMEM_LIMIT):
    return pltpu.CompilerParams(dimension_semantics=("arbitrary",) * n_axes, vmem_limit_bytes=vmem)


def _dot(a, b):
    return jnp.dot(a, b, preferred_element_type=F32)


def _dot_nt(a, b):
    return lax.dot_general(a, b, (((1,), (1,)), ((), ())), preferred_element_type=F32)


def _split2(x):
    hi = x.astype(BF16)
    lo = (x - hi.astype(F32)).astype(BF16)
    return hi, lo


def _row_copy(src, src_row, dst, dst_row, sem):
    return pltpu.make_async_copy(src.at[pl.ds(pl.multiple_of(src_row * SUBLANES, SUBLANES), SUBLANES), :],
                                 dst.at[pl.ds(pl.multiple_of(dst_row * SUBLANES, SUBLANES), SUBLANES), :], sem)


def _log_sigmoid(x):
    return jnp.minimum(x, 0.0) - jnp.log(1.0 + jnp.exp(-jnp.abs(x)))


def _mod_kernel(c_ref, w_ref, b_ref, o_ref):
    c = c_ref[...]
    s = c / (1.0 + jnp.exp(-c))
    o_ref[...] = _dot(s.astype(BF16), w_ref[0].astype(BF16)) + b_ref[...]


def _modulation(c, layer, w_ada, b_ada):
    b, d = c.shape
    n = w_ada.shape[2]
    tn = MOD_TILE
    return pl.pallas_call(
        _mod_kernel,
        grid=(n // tn,),
        in_specs=[pl.BlockSpec((b, d), lambda j: (0, 0)),
                  pl.BlockSpec((1, d, tn), lambda j: (layer, 0, j)),
                  pl.BlockSpec((1, tn), lambda j: (0, j))],
        out_specs=pl.BlockSpec((b, tn), lambda j: (0, j)),
        out_shape=jax.ShapeDtypeStruct((b, n), F32),
        compiler_params=_cparams(1),
        name="modulation",
    )(c, w_ada, b_ada.reshape(1, n))


def _gather_rows(yg_ref, idx_ref, buf_ref, sem, slot, n_rows):
    def issue(r, c):
        for k in range(TOP_K):
            _row_copy(yg_ref, idx_ref[r * TOP_K + k], buf_ref.at[slot, k], r, sem.at[slot]).start()
        return c

    lax.fori_loop(0, n_rows, issue, 0)


def _combine_rows(gate_ref, buf_ref, slot, n_rows, x1_ref, g2, out_ref):
    gates = gate_ref[...]
    gate_cols = [gates[:, k:k + 1] for k in range(TOP_K)]
    for c in range(ROW_TILES):
        acc = jnp.zeros((n_rows, LANES), F32)
        for k in range(TOP_K):
            acc = acc + gate_cols[k] * buf_ref[slot, k, pl.ds(c, n_rows, stride=SUBLANES), :]
        cols = slice(c * LANES, (c + 1) * LANES)
        out_ref[:, cols] = x1_ref[:, cols] + g2[:, cols] * acc


def _wait_rows(buf_ref, sem, slot):
    for k in range(TOP_K):
        pltpu.make_async_copy(buf_ref.at[slot, k], buf_ref.at[slot, k], sem.at[slot]).wait()


def _inproj_kernel(tiles_per_seq, fused, *refs):
    if fused:
        (pos_ref, pos_next_ref, gate_ref, x1_ref, modp_ref, yg_ref, mod_ref, g1_ref, w_ref, bd_ref, tri_ref, fb_ref,
         gains_ref, x_out_ref, cb_ref, v_ref, sbq_ref, sbk_ref, sbv_ref, fq_ref, fk_ref, fv_ref, cum_ref,
         carry_ref, x_ref, buf_ref, sem) = refs
        tm = x1_ref.shape[0]
        i = pl.program_id(0)
        slot = i % 2

        @pl.when(i == 0)
        def _():
            _gather_rows(yg_ref, pos_ref, buf_ref, sem, 0, tm)

        @pl.when(i + 1 < pl.num_programs(0))
        def _():
            _gather_rows(yg_ref, pos_next_ref, buf_ref, sem, 1 - slot, tm)

        _wait_rows(buf_ref, sem, slot)
        _combine_rows(gate_ref, buf_ref, slot, tm, x1_ref, modp_ref[0, 5:6, :], x_ref)
        x_out_ref[...] = x_ref[...]
    else:
        (x_ref, mod_ref, g1_ref, w_ref, bd_ref, tri_ref, fb_ref, gains_ref,
         cb_ref, v_ref, sbq_ref, sbk_ref, sbv_ref, fq_ref, fk_ref, fv_ref, cum_ref, carry_ref) = refs
        tm = x_ref.shape[0]
    x = x_ref[...]
    ms = jnp.mean(x * x, axis=-1, keepdims=True)
    h = (x * lax.rsqrt(ms + EPS)) * g1_ref[...]
    h = h * (1.0 + mod_ref[0, 1:2, :]) + mod_ref[0, 0:1, :]
    hb = h.astype(BF16)

    pc = _dot(hb, w_ref[:, 0:3 * CONV_CH])
    cb_ref[...] = pc[:, 0:CONV_CH]
    v_ref[...] = pc[:, CONV_CH:2 * CONV_CH] * pc[:, 2 * CONV_CH:3 * CONV_CH]

    normed = ((COL_SBQ, sbq_ref), (COL_SBK, sbk_ref), (COL_FQ, fq_ref), (COL_FK, fk_ref))
    proj = [_dot(hb, w_ref[:, col:col + SB_DIM]) for col, _ in normed]
    sbv_ref[...] = _dot(hb, w_ref[:, COL_SBV:COL_SBV + SB_DIM]).astype(BF16)
    fv_ref[...] = _dot(hb, w_ref[:, COL_FV:COL_FV + FOX_DIM]).astype(BF16)
    flog = _dot(hb, w_ref[:, COL_FLOG:COL_FLOG + LANES])
    def head_sumsq(p):
        sq = (p * p).astype(BF16)
        wide = 2 * LANES
        return jnp.concatenate([_dot(sq[:, :wide], bd_ref[:wide, :wide]), _dot(sq[:, wide:], bd_ref[wide:, wide:])], axis=-1)

    sumsq = [head_sumsq(p) for p in proj]
    for row, ((_, out_ref), p, ss) in enumerate(zip(normed, proj, sumsq)):
        out_ref[...] = (p * lax.rsqrt(ss * (1.0 / HEAD_DIM) + EPS) * gains_ref[row:row + 1, :]).astype(BF16)

    lf = _log_sigmoid(flog + fb_ref[...]) * LOG2_E
    t0 = lf.astype(BF16)
    r1 = lf - t0.astype(F32)
    t1 = r1.astype(BF16)
    t2 = (r1 - t1.astype(F32)).astype(BF16)
    tri = tri_ref[...]
    cs = _dot(tri, t0) + _dot(tri, t1) + _dot(tri, t2)

    @pl.when(pl.program_id(0) % tiles_per_seq == 0)
    def _():
        carry_ref[...] = jnp.zeros_like(carry_ref)

    cum = cs + carry_ref[0:1, :]
    carry_ref[0:1, :] = cum[tm - 1:tm, :]
    cum_ref[0] = cum.T[0:SUBLANES, :]


def _inproj(x_parts, mod3, g1, w_in_p, bd, tri, fb, gains, seq):
    fused = len(x_parts) > 1
    t = x_parts[2].shape[0] if fused else x_parts[0].shape[0]
    tm = TM_PROJ
    n = t // tm
    tiles_per_seq = seq // tm
    nb = t // seq
    row = lambda i: (i, 0)
    const = lambda i: (0, 0)
    per_seq = lambda i: (i // tiles_per_seq, 0, 0)
    outs = [jax.ShapeDtypeStruct((t, CONV_CH), F32), jax.ShapeDtypeStruct((t, CONV_CH), F32)]
    outs += [jax.ShapeDtypeStruct((t, SB_DIM), BF16)] * 6
    outs += [jax.ShapeDtypeStruct((nb, SUBLANES, seq), F32)]
    out_specs = [pl.BlockSpec((tm, CONV_CH), row)] * 2 + [pl.BlockSpec((tm, SB_DIM), row)] * 6
    out_specs += [pl.BlockSpec((1, SUBLANES, tm), lambda i: (i // tiles_per_seq, 0, i % tiles_per_seq))]
    in_specs = [pl.BlockSpec((1, N_MOD, D_MODEL), per_seq),
                pl.BlockSpec((1, D_MODEL), const),
                pl.BlockSpec((D_MODEL, D_IN_PAD), const),
                pl.BlockSpec((SB_DIM, SB_DIM), const),
                pl.BlockSpec((tm, tm), const),
                pl.BlockSpec((1, LANES), const),
                pl.BlockSpec((SUBLANES, SB_DIM), const)]
    scratch = [pltpu.VMEM((SUBLANES, LANES), F32)]
    if fused:
        pos, gates, x1, mod3_prev, yg = x_parts
        operands = (pos, pos, gates, x1, mod3_prev, yg)
        in_specs = [pl.BlockSpec((tm * TOP_K,), lambda i: (i,), memory_space=pltpu.SMEM),
                    pl.BlockSpec((tm * TOP_K,), lambda i: (jnp.minimum(i + 1, n - 1),), memory_space=pltpu.SMEM),
                    pl.BlockSpec((tm, LANES), row),
                    pl.BlockSpec((tm, D_MODEL), row),
                    pl.BlockSpec((1, N_MOD, D_MODEL), per_seq),
                    pl.BlockSpec(memory_space=pl.ANY)] + in_specs
        outs = [jax.ShapeDtypeStruct((t, D_MODEL), F32)] + outs
        out_specs = [pl.BlockSpec((tm, D_MODEL), row)] + out_specs
        scratch += [pltpu.VMEM((tm, D_MODEL), F32), pltpu.VMEM((2, TOP_K, tm * SUBLANES, LANES), F32),
                    pltpu.SemaphoreType.DMA((2,))]
    else:
        operands = x_parts
        in_specs = [pl.BlockSpec((tm, D_MODEL), row)] + in_specs
    return pl.pallas_call(
        functools.partial(_inproj_kernel, tiles_per_seq, fused),
        grid=(n,),
        in_specs=in_specs,
        out_specs=out_specs,
        out_shape=outs,
        scratch_shapes=scratch,
        compiler_params=_cparams(1, INPROJ_VMEM_LIMIT),
        name="inproj",
    )(*operands, mod3, g1, w_in_p, bd, tri, fb, gains)


def _stack_heads(k_ref, v_ref, kk_ref, vv_ref):
    n_blocks = kk_ref.shape[1]
    lane = lax.broadcasted_iota(jnp.int32, (TB, LANES), 1)
    first = lane < HEAD_DIM

    def body(j, c):
        rows = pl.ds(pl.multiple_of(j * TB, TB), TB)
        for src, dst in ((k_ref, kk_ref), (v_ref, vv_ref)):
            for p in range(N_PAIRS):
                blk = src[0, rows, p * LANES:(p + 1) * LANES]
                zero = jnp.zeros_like(blk)
                dst[p, j, 0:TB, :] = jnp.where(first, blk, zero)
                dst[p, j, TB:2 * TB, :] = jnp.where(first, zero, blk)
        return c

    lax.fori_loop(0, n_blocks, body, 0)


def _per_head(a0, a1):
    return jnp.concatenate([jnp.broadcast_to(a0, (TB, TB)), jnp.broadcast_to(a1, (TB, TB))], axis=1)


def _pair_iota():
    row = lax.broadcasted_iota(jnp.int32, (TB, 2 * TB), 0)
    col = lax.broadcasted_iota(jnp.int32, (TB, 2 * TB), 1) % TB
    return row, col


_CHAINS = [(p, s) for p in range(N_PAIRS) for s in range(2)]


def _q_block(q_ref, p, s):
    return q_ref[0, s * TB:(s + 1) * TB, p * LANES:(p + 1) * LANES]


def _sb_kernel(q_ref, k_ref, v_ref, uu_ref, o_ref, kk_ref, vv_ref):
    qi = pl.program_id(1)

    @pl.when(qi == 0)
    def _():
        _stack_heads(k_ref, v_ref, kk_ref, vv_ref)

    uu = uu_ref[...]
    row, col = _pair_iota()
    strict = col < row

    def sweep(states, backs):
        items = [(c, b) for b in range(len(backs)) for c in range(len(_CHAINS))]
        kbs, keeps, z = {}, {}, {}
        for c, b in items:
            p, s = _CHAINS[c]
            back = backs[b]
            kb = 2 * qi + s - back
            if isinstance(back, int) and back == 0:
                keeps[c, b] = strict
            elif isinstance(back, int) and back == 1 and s == 1:
                keeps[c, b] = None
            else:
                keeps[c, b] = kb >= 0
            kbs[c, b] = jnp.maximum(kb, 0)
            z[c, b] = _dot_nt(_q_block(q_ref, p, s), kk_ref[p, kbs[c, b]])
        log_beta, log_1mb, sums = {}, {}, {}
        for it in items:
            log_beta[it] = jnp.minimum(z[it], 0.0) - jnp.log2(1.0 + jnp.exp2(-jnp.abs(z[it])))
            l1 = log_beta[it] - z[it]
            log_1mb[it] = l1 if keeps[it] is None else jnp.where(keeps[it], l1, 0.0)
            sums[it] = (jnp.sum(log_1mb[it][:, :TB], axis=-1, keepdims=True),
                        jnp.sum(log_1mb[it][:, TB:], axis=-1, keepdims=True))
        later = {}
        for it in items:
            hi, lo = _split2(log_1mb[it])
            later[it] = _dot(hi, uu) + _dot(lo, uu)
        states = list(states)
        for it in items:
            c = it[0]
            rs0, rs1, acc = states[c]
            w = jnp.exp2(log_beta[it] + later[it] + _per_head(rs0, rs1))
            if keeps[it] is not None:
                w = jnp.where(keeps[it], w, 0.0)
            acc = acc + _dot(w.astype(BF16), vv_ref[_CHAINS[c][0], kbs[it]])
            states[c] = (rs0 + sums[it][0], rs1 + sums[it][1], acc)
        return states

    def more(states):
        top = states[0][0]
        for st in states:
            top = jnp.maximum(top, jnp.maximum(st[0], st[1]))
        return (jnp.max(top) > LOG2_NEGLIGIBLE).astype(jnp.int32)

    zero = jnp.zeros((TB, 1), F32)
    states = sweep([(zero, zero, jnp.zeros((TB, LANES), F32))] * len(_CHAINS), (0, 1))

    def cond(c):
        return jnp.logical_and(2 * qi + 1 - c[0] >= 0, c[2] > 0)

    def body(c):
        back, states, _ = c
        states = sweep(states, (back,))
        return back + 1, states, more(states)

    _, states, _ = lax.while_loop(cond, body, (jnp.int32(2), states, more(states)))
    for (p, s), st in zip(_CHAINS, states):
        o_ref[0, s * TB:(s + 1) * TB, p * LANES:(p + 1) * LANES] = st[2]


def _sb_attention(q, k, v, uu):
    b, s, d = q.shape
    tq = TQ
    qspec = pl.BlockSpec((1, tq, d), lambda bi, qi: (bi, qi, 0))
    kvspec = pl.BlockSpec((1, s, d), lambda bi, qi: (bi, 0, 0))
    scratch = pltpu.VMEM((N_PAIRS, s // TB, 2 * TB, LANES), BF16)
    return pl.pallas_call(
        _sb_kernel,
        grid=(b, s // tq),
        in_specs=[qspec, kvspec, kvspec, pl.BlockSpec((2 * TB, 2 * TB), lambda bi, qi: (0, 0))],
        out_specs=qspec,
        out_shape=jax.ShapeDtypeStruct((b, s, d), F32),
        scratch_shapes=[scratch, scratch],
        compiler_params=_cparams(2),
        name="sb_attention",
    )(q, k, v, uu)


def _fox_kernel(steps_ref, q_ref, k_ref, v_ref, f2_ref, o_ref, kk_ref, vv_ref, sc_ref):
    bi = pl.program_id(0)
    qi = pl.program_id(1)

    @pl.when(qi == 0)
    def _():
        _stack_heads(k_ref, v_ref, kk_ref, vv_ref)

    row, col = _pair_iota()
    causal = col <= row
    lane = lax.broadcasted_iota(jnp.int32, (TB, LANES), 1)
    first = lane < HEAD_DIM

    def scores(p, s, back, steps):
        kb = 2 * qi + s - back
        sc = _dot_nt(_q_block(q_ref, p, s), kk_ref[p, jnp.maximum(kb, 0)]) - f2_ref[0, p, pl.ds(jnp.maximum(kb, 0), 1), :]
        return jnp.where(jnp.logical_and(kb >= 0, back <= steps), sc, -1e30)

    def head_max(a):
        return (jnp.max(a[:, :TB], axis=-1, keepdims=True), jnp.max(a[:, TB:], axis=-1, keepdims=True))

    def head_sum(a):
        return (jnp.sum(a[:, :TB], axis=-1, keepdims=True), jnp.sum(a[:, TB:], axis=-1, keepdims=True))

    for p in range(N_PAIRS):
        steps = steps_ref[(bi * N_PAIRS + p) * pl.num_programs(1) + qi]
        n_it = (steps + 1) // 2

        runmax = []
        for s in range(2):
            sc = jnp.where(causal, scores(p, s, 0, steps), -1e30)
            sc_ref[s, 0] = sc
            runmax.append(sc)

        def pass1(it, runmax):
            out = list(runmax)
            for u in (1, 2):
                back = 2 * it + u
                for s in range(2):
                    sc = scores(p, s, back, steps)
                    sc_ref[s, back] = sc
                    out[s] = jnp.maximum(out[s], sc)
            return out

        runmax = lax.fori_loop(0, n_it, pass1, runmax)
        tops = [_per_head(*head_max(runmax[s])) for s in range(2)]

        def weigh(s, back, runsum, acc):
            pr = jnp.exp2(sc_ref[s, back] - tops[s])
            kb = jnp.maximum(2 * qi + s - back, 0)
            return runsum + pr, acc + _dot(pr.astype(BF16), vv_ref[p, kb])

        state = [weigh(s, 0, jnp.zeros((TB, 2 * TB), F32), jnp.zeros((TB, LANES), F32)) for s in range(2)]

        def pass2(it, state):
            out = list(state)
            for u in (1, 2):
                for s in range(2):
                    out[s] = weigh(s, 2 * it + u, *out[s])
            return out

        state = lax.fori_loop(0, n_it, pass2, state)
        for s in range(2):
            l0, l1 = head_sum(state[s][0])
            o_ref[0, s * TB:(s + 1) * TB, p * LANES:(p + 1) * LANES] = state[s][1] / jnp.where(first, l0, l1)


def _fox_attention(q, k, v, f2, steps):
    b, s, d = q.shape
    tq = TQ
    n_kb = s // TB
    qspec = pl.BlockSpec((1, tq, d), lambda bi, qi, *_: (bi, qi, 0))
    kvspec = pl.BlockSpec((1, s, d), lambda bi, qi, *_: (bi, 0, 0))
    scratch = pltpu.VMEM((N_PAIRS, n_kb, 2 * TB, LANES), BF16)
    grid_spec = pltpu.PrefetchScalarGridSpec(
        num_scalar_prefetch=1,
        grid=(b, s // tq),
        in_specs=[qspec, kvspec, kvspec,
                  pl.BlockSpec((1, N_PAIRS, n_kb, 2 * TB), lambda bi, qi, *_: (bi, 0, 0, 0))],
        out_specs=qspec,
        scratch_shapes=[scratch, scratch, pltpu.VMEM((2, n_kb + 2, TB, 2 * TB), F32)],
    )
    return pl.pallas_call(
        _fox_kernel,
        grid_spec=grid_spec,
        out_shape=jax.ShapeDtypeStruct((b, s, d), F32),
        compiler_params=_cparams(2),
        name="fox_attention",
    )(steps, q, k, v, f2)


def _outproj_kernel(tiles_per_seq, x_ref, cb_ref, v_ref, vprev_ref, ysb_ref, yfox_ref, cw_ref, og_ref, wout_ref,
                    mod_ref, g2_ref, rw_ref, rb_ref, bd_ref, ltri_ref,
                    x1_ref, h2t_ref, sel_ref, gate_ref, cnt_ref, vext_ref, carry_ref):
    tm = x_ref.shape[0]
    i = pl.program_id(0)

    @pl.when(i == 0)
    def _():
        carry_ref[...] = jnp.zeros_like(carry_ref)

    first = (i % tiles_per_seq) == 0
    vext_ref[0:SUBLANES, :] = jnp.where(first, 0.0, vprev_ref[...])
    vext_ref[SUBLANES:SUBLANES + tm, :] = v_ref[...]
    conv = (cw_ref[0:1, :] * vext_ref[SUBLANES - 2:SUBLANES - 2 + tm, :]
            + cw_ref[1:2, :] * vext_ref[SUBLANES - 1:SUBLANES - 1 + tm, :]
            + cw_ref[2:3, :] * v_ref[...])
    y = jnp.concatenate([cb_ref[...] * conv, ysb_ref[...], yfox_ref[...]], axis=-1)

    ysq = (y * y).astype(BF16)
    bd = bd_ref[...]
    ss = jnp.concatenate([_dot(ysq[:, c:c + 2 * LANES], bd) for c in range(0, D_MIX, 2 * LANES)], axis=-1)
    yn = y * lax.rsqrt(ss * (1.0 / HEAD_DIM) + EPS) * og_ref[...]
    x1 = x_ref[...] + mod_ref[0, 2:3, :] * _dot(yn.astype(BF16), wout_ref[...])
    x1_ref[...] = x1

    ms = jnp.mean(x1 * x1, axis=-1, keepdims=True)
    h2 = (x1 * lax.rsqrt(ms + EPS)) * g2_ref[...]
    h2 = h2 * (1.0 + mod_ref[0, 4:5, :]) + mod_ref[0, 3:4, :]
    for j in range(ROW_TILES):
        h2t_ref[pl.ds(j, tm, stride=SUBLANES), :] = h2[:, j * LANES:(j + 1) * LANES]

    h_hi, h_lo = _split2(h2)
    both = _dot(h_hi, rw_ref[...])
    logits = both[:, :LANES] + both[:, LANES:] + _dot(h_lo, rw_ref[:, :LANES]) + rb_ref[...]
    lane = lax.broadcasted_iota(jnp.int32, (tm, LANES), 1)
    lane_f = lane.astype(F32)
    neg = jnp.float32(-jnp.inf)
    cur = jnp.where(lane < N_EXPERTS, logits, neg)
    tops, sels = [], []
    for _ in range(TOP_K):
        mk = jnp.max(cur, axis=-1, keepdims=True)
        ik = jnp.min(jnp.where(cur == mk, lane_f, float(LANES)), axis=-1, keepdims=True)
        sel = lane_f == ik
        cur = jnp.where(sel, neg, cur)
        tops.append(mk)
        sels.append((sel, ik.astype(jnp.int32)))
    es = [jnp.exp(t - tops[0]) for t in tops]
    inv = 1.0 / (es[0] + es[1] + es[2] + es[3])

    multi = jnp.zeros((tm, LANES), F32)
    for sel, _ in sels:
        multi = jnp.where(sel, 1.0, multi)
    before = _dot(ltri_ref[...], multi.astype(BF16)) + carry_ref[0:1, :]
    carry_ref[0:1, :] = before[tm - 1:tm, :] + multi[tm - 1:tm, :]

    sel_out = jnp.zeros((tm, LANES), jnp.int32)
    gate_out = jnp.zeros((tm, LANES), F32)
    for k, (sel, ik) in enumerate(sels):
        rank = jnp.sum(jnp.where(sel, before, 0.0), axis=-1, keepdims=True).astype(jnp.int32)
        sel_out = jnp.where(lane == k, ik * (1 << RANK_BITS) + rank, sel_out)
        gate_out = jnp.where(lane == k, es[k] * inv, gate_out)
    sel_ref[...] = sel_out
    gate_ref[...] = gate_out
    cnt_ref[...] = jnp.broadcast_to(carry_ref[0:1, :], cnt_ref.shape)


def _outproj(x2, cb, v, ysb, yfox, conv_w, og, w_out_b, mod3, g2, rw, rb, bd2, ltri, seq):
    t = x2.shape[0]
    tm = TM_PROJ
    tiles_per_seq = seq // tm
    row = lambda i: (i, 0)
    const = lambda i: (0, 0)
    return pl.pallas_call(
        functools.partial(_outproj_kernel, tiles_per_seq),
        grid=(t // tm,),
        in_specs=[pl.BlockSpec((tm, D_MODEL), row),
                  pl.BlockSpec((tm, CONV_CH), row),
                  pl.BlockSpec((tm, CONV_CH), row),
                  pl.BlockSpec((SUBLANES, CONV_CH), lambda i: (jnp.maximum(i * (tm // SUBLANES) - 1, 0), 0)),
                  pl.BlockSpec((tm, SB_DIM), row),
                  pl.BlockSpec((tm, FOX_DIM), row),
                  pl.BlockSpec((CONV_W, CONV_CH), const),
                  pl.BlockSpec((1, D_MIX), const),
                  pl.BlockSpec((D_MIX, D_MODEL), const),
                  pl.BlockSpec((1, N_MOD, D_MODEL), lambda i: (i // tiles_per_seq, 0, 0)),
                  pl.BlockSpec((1, D_MODEL), const),
                  pl.BlockSpec((D_MODEL, 2 * LANES), const),
                  pl.BlockSpec((1, LANES), const),
                  pl.BlockSpec((2 * LANES, 2 * LANES), const),
                  pl.BlockSpec((tm, tm), const)],
        out_specs=[pl.BlockSpec((tm, D_MODEL), row),
                   pl.BlockSpec((tm * SUBLANES, LANES), row),
                   pl.BlockSpec((tm, LANES), row),
                   pl.BlockSpec((tm, LANES), row),
                   pl.BlockSpec((SUBLANES, LANES), const)],
        out_shape=[jax.ShapeDtypeStruct((t, D_MODEL), F32),
                   jax.ShapeDtypeStruct((t * SUBLANES, LANES), F32),
                   jax.ShapeDtypeStruct((t, LANES), jnp.int32),
                   jax.ShapeDtypeStruct((t, LANES), F32),
                   jax.ShapeDtypeStruct((SUBLANES, LANES), F32)],
        scratch_shapes=[pltpu.VMEM((tm + SUBLANES, CONV_CH), F32), pltpu.VMEM((SUBLANES, LANES), F32)],
        compiler_params=_cparams(1),
        name="outproj_router",
    )(x2, cb, v, v, ysb, yfox, conv_w, og, w_out_b, mod3, g2, rw, rb, bd2, ltri)


def _zero_fill(meta_ref, zero_ref, xg_ref, zsem):
    n_tiles = xg_ref.shape[0] // (TM_GROUP * SUBLANES)
    zero_ref[...] = jnp.zeros_like(zero_ref)

    def sweep(wait):
        def per_expert(e, c):
            pos = meta_ref[N_EXPERTS + e]
            n = meta_ref[2 * N_EXPERTS + e]
            bit = TM_GROUP // 2
            while bit:
                @pl.when((n & bit) != 0)
                def _(pos=pos, bit=bit):
                    cp = pltpu.make_async_copy(
                        zero_ref.at[pl.ds(0, bit * SUBLANES), :],
                        xg_ref.at[pl.ds(pl.multiple_of(pos * SUBLANES, SUBLANES), bit * SUBLANES), :], zsem)
                    cp.wait() if wait else cp.start()
                pos = pos + (n & bit)
                bit //= 2
            return c

        lax.fori_loop(0, N_EXPERTS, per_expert, 0)

        def per_tile(j, c):
            rows = TM_GROUP * SUBLANES
            cp = pltpu.make_async_copy(zero_ref, xg_ref.at[pl.ds(pl.multiple_of(j * rows, rows), rows), :], zsem)
            cp.wait() if wait else cp.start()
            return c

        lax.fori_loop(meta_ref[3 * N_EXPERTS], n_tiles, per_tile, 0)

    sweep(False)
    sweep(True)


def _dispatch_kernel(meta_ref, pos_ref, h_ref, xg_ref, zero_ref, sem, zsem):
    tm = h_ref.shape[0] // SUBLANES

    @pl.when(pl.program_id(0) == 0)
    def _():
        _zero_fill(meta_ref, zero_ref, xg_ref, zsem)

    def issue(r, c):
        for k in range(TOP_K):
            _row_copy(h_ref, r, xg_ref, pos_ref[r * TOP_K + k], sem).start()
        return c

    lax.fori_loop(0, tm, issue, 0)
    n = tm * TOP_K * SUBLANES
    pltpu.make_async_copy(xg_ref.at[pl.ds(0, n), :], xg_ref.at[pl.ds(0, n), :], sem).wait()


def _dispatch(meta, pos, h2t, n_rows):
    t = h2t.shape[0] // SUBLANES
    tm = TM_DISPATCH
    grid_spec = pltpu.PrefetchScalarGridSpec(
        num_scalar_prefetch=1,
        grid=(t // tm,),
        in_specs=[pl.BlockSpec((tm * TOP_K,), lambda i, *_: (i,), memory_space=pltpu.SMEM),
                  pl.BlockSpec((tm * SUBLANES, LANES), lambda i, *_: (i, 0))],
        out_specs=pl.BlockSpec(memory_space=pl.ANY),
        scratch_shapes=[pltpu.VMEM((TM_GROUP * SUBLANES, LANES), F32),
                        pltpu.SemaphoreType.DMA(()), pltpu.SemaphoreType.DMA(())],
    )
    return pl.pallas_call(
        _dispatch_kernel,
        grid_spec=grid_spec,
        out_shape=jax.ShapeDtypeStruct((n_rows * SUBLANES, LANES), F32),
        compiler_params=_cparams(1),
        name="moe_dispatch",
    )(meta, pos, h2t)


def _expert_kernel(layer, te_ref, nv_ref, nxt_ref, run_ref, xg_ref, w1_hbm, b1_ref, w2_hbm, b2_ref, yg_ref,
                   wg_ref, wl_ref, w2b_ref, wt_ref, w1_ref, w2_ref, wsem):
    tmg = xg_ref.shape[0] // SUBLANES
    j = pl.program_id(0)
    nv = nv_ref[j]
    slot = run_ref[j] % 2

    def weight_copies(expert, to_slot):
        return (pltpu.make_async_copy(w1_hbm.at[layer, expert], w1_ref.at[to_slot], wsem.at[to_slot, 0]),
                pltpu.make_async_copy(w2_hbm.at[layer, expert], w2_ref.at[to_slot], wsem.at[to_slot, 1]))

    @pl.when(j == 0)
    def _():
        for cp in weight_copies(te_ref[0], slot):
            cp.start()

    @pl.when(jnp.logical_or(j == 0, te_ref[j] != te_ref[jnp.maximum(j - 1, 0)]))
    def _():
        for cp in weight_copies(te_ref[j], slot):
            cp.wait()

        @pl.when(nxt_ref[j] >= 0)
        def _():
            for cp in weight_copies(nxt_ref[j], 1 - slot):
                cp.start()

        ncol = wt_ref.shape[0] // ROW_TILES
        half = ncol // 2
        for n0 in range(0, 2 * D_EXPERT, ncol):
            wt = w1_ref[slot, :, n0:n0 + ncol].T
            for c in range(ROW_TILES):
                wt_ref[pl.ds(c, ncol, stride=ROW_TILES), :] = wt[:, c * LANES:(c + 1) * LANES]
            rows = slice(n0 // 2, n0 // 2 + half)
            for c in range(ROW_TILES):
                cols = slice(c * LANES, (c + 1) * LANES)
                wg_ref[rows, cols] = wt_ref[pl.ds(c, half, stride=2 * ROW_TILES), :].astype(BF16)
                wl_ref[rows, cols] = wt_ref[pl.ds(ROW_TILES + c, half, stride=2 * ROW_TILES), :].astype(BF16)
            w2b_ref[rows, :] = w2_ref[slot, rows, :].astype(BF16)

    @pl.when(nv == 0)
    def _():
        yg_ref[...] = jnp.zeros_like(yg_ref)

    @pl.when(nv > 0)
    def _():
        x = jnp.concatenate([xg_ref[pl.ds(c, tmg, stride=SUBLANES), :] for c in range(ROW_TILES)], axis=-1)
        x = x.astype(BF16)
        glu = jnp.minimum(_dot_nt(x, wg_ref[...]) + b1_ref[0, 0:1, :], SWIGLU_LIMIT)
        lin = jnp.clip(_dot_nt(x, wl_ref[...]) + b1_ref[0, 1:2, :], -SWIGLU_LIMIT, SWIGLU_LIMIT)
        a = glu * (1.0 / (1.0 + jnp.exp(-SWIGLU_ALPHA * glu))) * (lin + 1.0)
        y = _dot(a.astype(BF16), w2b_ref[...]) + b2_ref[0]
        for c in range(ROW_TILES):
            yg_ref[pl.ds(c, tmg, stride=SUBLANES), :] = y[:, c * LANES:(c + 1) * LANES]


def _experts(te, nv, xg, layer, w1, b1, w2, b2):
    tmg = TM_GROUP
    n_tiles = xg.shape[0] // (tmg * SUBLANES)
    tiles = jnp.arange(n_tiles, dtype=jnp.int32)
    differs = jnp.logical_and(tiles[None, :] > tiles[:, None], te[None, :] != te[:, None])
    nxt_tile = jnp.min(jnp.where(differs, tiles[None, :], n_tiles), axis=1)
    nxt = jnp.where(nxt_tile < n_tiles, te[jnp.minimum(nxt_tile, n_tiles - 1)], -1).astype(jnp.int32)
    run = jnp.cumsum(jnp.concatenate([jnp.zeros((1,), jnp.int32), (te[1:] != te[:-1]).astype(jnp.int32)]))
    per_tile = lambda j, te, *_: (j, 0)
    per_expert = lambda j, te, *_: (te[j], 0, 0)
    grid_spec = pltpu.PrefetchScalarGridSpec(
        num_scalar_prefetch=4,
        grid=(n_tiles,),
        in_specs=[pl.BlockSpec((tmg * SUBLANES, LANES), per_tile),
                  pl.BlockSpec(memory_space=pl.ANY),
                  pl.BlockSpec((1, 2, D_EXPERT), per_expert),
                  pl.BlockSpec(memory_space=pl.ANY),
                  pl.BlockSpec((1, 1, D_MODEL), per_expert)],
        out_specs=pl.BlockSpec((tmg * SUBLANES, LANES), per_tile),
        scratch_shapes=[pltpu.VMEM((D_EXPERT, D_MODEL), BF16), pltpu.VMEM((D_EXPERT, D_MODEL), BF16),
                        pltpu.VMEM((D_EXPERT, D_MODEL), BF16), pltpu.VMEM((2 * LANES * ROW_TILES, LANES), F32),
                        pltpu.VMEM((2, D_MODEL, 2 * D_EXPERT), F32), pltpu.VMEM((2, D_EXPERT, D_MODEL), F32),
                        pltpu.SemaphoreType.DMA((2, 2))],
    )
    return pl.pallas_call(
        functools.partial(_expert_kernel, layer),
        grid_spec=grid_spec,
        out_shape=jax.ShapeDtypeStruct(xg.shape, F32),
        compiler_params=_cparams(1, EXPERT_VMEM_LIMIT),
        name="moe_experts",
    )(te, nv, nxt, run.astype(jnp.int32), xg, w1, b1, w2, b2)


def _combine_kernel(pos_ref, pos_next_ref, gate_ref, x1_ref, mod_ref, yg_ref, o_ref, buf_ref, sem):
    tm = x1_ref.shape[0]
    i = pl.program_id(0)
    slot = i % 2

    @pl.when(i == 0)
    def _():
        _gather_rows(yg_ref, pos_ref, buf_ref, sem, 0, tm)

    @pl.when(i + 1 < pl.num_programs(0))
    def _():
        _gather_rows(yg_ref, pos_next_ref, buf_ref, sem, 1 - slot, tm)

    _wait_rows(buf_ref, sem, slot)
    _combine_rows(gate_ref, buf_ref, slot, tm, x1_ref, mod_ref[0, 5:6, :], o_ref)


def _combine(pos, gates, x1, mod3, yg, seq):
    t = x1.shape[0]
    tm = TM_COMBINE
    tiles_per_seq = seq // tm
    n = t // tm
    grid_spec = pltpu.PrefetchScalarGridSpec(
        num_scalar_prefetch=0,
        grid=(n,),
        in_specs=[pl.BlockSpec((tm * TOP_K,), lambda i: (i,), memory_space=pltpu.SMEM),
                  pl.BlockSpec((tm * TOP_K,), lambda i: (jnp.minimum(i + 1, n - 1),), memory_space=pltpu.SMEM),
                  pl.BlockSpec((tm, LANES), lambda i: (i, 0)),
                  pl.BlockSpec((tm, D_MODEL), lambda i: (i, 0)),
                  pl.BlockSpec((1, N_MOD, D_MODEL), lambda i: (i // tiles_per_seq, 0, 0)),
                  pl.BlockSpec(memory_space=pl.ANY)],
        out_specs=pl.BlockSpec((tm, D_MODEL), lambda i: (i, 0)),
        scratch_shapes=[pltpu.VMEM((2, TOP_K, tm * SUBLANES, LANES), F32), pltpu.SemaphoreType.DMA((2,))],
    )
    return pl.pallas_call(
        _combine_kernel,
        grid_spec=grid_spec,
        out_shape=jax.ShapeDtypeStruct((t, D_MODEL), F32),
        compiler_params=_cparams(1),
        name="moe_combine",
    )(pos, pos, gates, x1, mod3, yg)


def _block_diag_ones(n):
    i = np.arange(n)
    return jnp.asarray((i[:, None] // HEAD_DIM) == (i[None, :] // HEAD_DIM), dtype=BF16)


def _tri(n, strict, upper):
    i = np.arange(n)
    if upper:
        m = (i[:, None] > i[None, :]) if strict else (i[:, None] >= i[None, :])
        return jnp.asarray(m, dtype=BF16)
    m = (i[None, :] < i[:, None]) if strict else (i[None, :] <= i[:, None])
    return jnp.asarray(m, dtype=BF16)


def _tile2(g):
    return jnp.tile(g, SB_DIM // HEAD_DIM)


def _layer(x_parts, c, seq, layer, last, w_mlp1, w_mlp2, w_ada, norm1_g, b_ada, w_in, conv_w, sb_q_g, sb_k_g, fox_q_g, fox_k_g, fox_f_b,
           out_norm_g, w_out, norm2_g, router_w, router_b, b_mlp1, b_mlp2):
    t = x_parts[0].shape[0] if len(x_parts) == 1 else x_parts[2].shape[0]
    nb = t // seq
    scale = HEAD_DIM ** -0.5

    w_in_p = jnp.pad(w_in, ((0, 0), (0, D_IN_PAD - w_in.shape[1]))).astype(BF16)
    fb = jnp.pad(fox_f_b, (0, LANES - FOX_HEADS)).reshape(1, LANES)
    qscale = scale * LOG2_E
    gains = jnp.stack([_tile2(sb_q_g) * qscale, _tile2(sb_k_g), _tile2(fox_q_g) * qscale, _tile2(fox_k_g)])
    gains = jnp.pad(gains, ((0, SUBLANES - 4), (0, 0)))
    rw = jnp.pad(router_w, ((0, 0), (0, LANES - N_EXPERTS)))
    rw_hi = rw.astype(BF16)
    rw = jnp.concatenate([rw_hi, (rw - rw_hi.astype(F32)).astype(BF16)], axis=1)
    rb = jnp.pad(router_b, (0, LANES - N_EXPERTS)).reshape(1, LANES)
    b1 = jnp.stack([b_mlp1[:, 0::2], b_mlp1[:, 1::2]], axis=1)
    b2 = b_mlp2.reshape(N_EXPERTS, 1, D_MODEL)

    mod3 = _modulation(c, layer, w_ada, b_ada).reshape(nb, N_MOD, D_MODEL)

    outs = _inproj(x_parts, mod3, norm1_g.reshape(1, D_MODEL), w_in_p, _block_diag_ones(SB_DIM),
                   _tri(TM_PROJ, strict=False, upper=False), fb, gains, seq)
    x2 = x_parts[0] if len(x_parts) == 1 else outs[0]
    cb, v, sbq, sbk, sbv, fq, fk, fv, cum = outs[-9:]

    r3 = lambda a: a.reshape(nb, seq, SB_DIM)
    u = _tri(TB, strict=True, upper=True)
    zu = jnp.zeros_like(u)
    uu = jnp.concatenate([jnp.concatenate([u, zu], axis=1), jnp.concatenate([zu, u], axis=1)], axis=0)
    ysb = _sb_attention(r3(sbq), r3(sbk), r3(sbv), uu)

    n_kb = seq // TB
    f2 = cum[:, :FOX_HEADS, :].reshape(nb, FOX_DIM // LANES, 2, n_kb, TB).transpose(0, 1, 3, 2, 4)
    f2 = f2.reshape(nb, FOX_DIM // LANES, n_kb, 2 * TB)
    zbound = 2.0 * 1.02 * HEAD_DIM * qscale * jnp.max(jnp.abs(fox_q_g)) * jnp.max(jnp.abs(fox_k_g))
    fend = cum[:, :FOX_HEADS, TB - 1::TB]
    f_query = jnp.concatenate([fend[..., :1], fend[..., :-1]], axis=-1)
    blk = jnp.arange(n_kb)
    kept = jnp.logical_and(zbound + f_query[..., :, None] - fend[..., None, :] >= LOG2_NEGLIGIBLE,
                           blk[None, :] < blk[:, None])
    steps = kept.sum(-1).reshape(nb, N_PAIRS, 2, n_kb // 2, 2).max(axis=(2, 4))
    steps = steps.astype(jnp.int32).reshape(nb * N_PAIRS * (n_kb // 2))
    yfox = _fox_attention(r3(fq), r3(fk), r3(fv), f2, steps)

    x1, h2t, sel, gates, cnt = _outproj(
        x2, cb, v, ysb.reshape(t, SB_DIM), yfox.reshape(t, FOX_DIM), conv_w, out_norm_g.reshape(1, D_MIX),
        w_out.astype(BF16), mod3, norm2_g.reshape(1, D_MODEL), rw, rb, _block_diag_ones(2 * LANES),
        _tri(TM_PROJ, strict=True, upper=False), seq)

    counts = cnt[0, :N_EXPERTS].astype(jnp.int32)
    padded = ((counts + TM_GROUP - 1) // TM_GROUP) * TM_GROUP
    off_end = jnp.cumsum(padded)
    off = off_end - padded
    n_rows = t * TOP_K + N_EXPERTS * TM_GROUP
    n_tiles = n_rows // TM_GROUP
    start = jnp.arange(n_tiles, dtype=jnp.int32) * TM_GROUP
    te = jnp.minimum(jnp.sum(start[:, None] >= off_end[None, :], axis=1), N_EXPERTS - 1).astype(jnp.int32)
    nv = jnp.clip(off[te] + counts[te] - start, 0, TM_GROUP).astype(jnp.int32)
    meta = jnp.concatenate([off, off + counts, padded - counts, off_end[-1:] // TM_GROUP]).astype(jnp.int32)

    assert t <= 1 << RANK_BITS
    words = sel[:, :TOP_K].reshape(t * TOP_K // LANES, LANES)
    picked = jnp.where((words >> RANK_BITS)[..., None] == jnp.arange(N_EXPERTS), off, 0).sum(-1)
    pos = (picked + (words & ((1 << RANK_BITS) - 1))).astype(jnp.int32).reshape(t * TOP_K)
    xg = _dispatch(meta, pos, h2t, n_rows)
    yg = _experts(te, nv, xg, layer, w_mlp1, b1, w_mlp2, b2)
    return _combine(pos, gates, x1, mod3, yg, seq) if last else (pos, gates, x1, mod3, yg)


def kernel(x, c, norm1_g, w_ada, b_ada, w_in, conv_w, sb_q_g, sb_k_g, fox_q_g, fox_k_g, fox_f_b, out_norm_g, w_out,
           norm2_g, router_w, router_b, w_mlp1, b_mlp1, w_mlp2, b_mlp2):
    b, s, d = x.shape
    x_parts = (x.reshape(b * s, d),)
    params = (norm1_g, b_ada, w_in, conv_w, sb_q_g, sb_k_g, fox_q_g, fox_k_g, fox_f_b, out_norm_g, w_out,
              norm2_g, router_w, router_b, b_mlp1, b_mlp2)
    depth = norm1_g.shape[0]
    for layer in range(depth):
        x_parts = _layer(x_parts, c, s, layer, layer == depth - 1, w_mlp1, w_mlp2, w_ada, *(p[layer] for p in params))
    return x_parts.reshape(b, s, d)
```

```python
import functools

import numpy as np
import jax
import jax.numpy as jnp
from jax import lax
from jax.experimental import pallas as pl
from jax.experimental.pallas import tpu as pltpu

F32 = jnp.float32
BF16 = jnp.bfloat16

D_MODEL = 1024
HEAD_DIM = 64
CONV_CH = 256
CONV_W = 3
SB_DIM = 384
FOX_DIM = 384
FOX_HEADS = 6
N_PAIRS = 3
D_MIX = 1024
N_MOD = 6
N_EXPERTS = 32
TOP_K = 4
D_EXPERT = 1024
SWIGLU_ALPHA = 1.702
SWIGLU_LIMIT = 7.0
EPS = 1e-6

LANES = 128
SUBLANES = 8
ROW_TILES = D_MODEL // LANES
D_IN_PAD = 3 * CONV_CH + 3 * SB_DIM + 3 * FOX_DIM + LANES
COL_SBQ = 3 * CONV_CH
COL_SBK = COL_SBQ + SB_DIM
COL_SBV = COL_SBK + SB_DIM
COL_FQ = COL_SBV + SB_DIM
COL_FK = COL_FQ + FOX_DIM
COL_FV = COL_FK + FOX_DIM
COL_FLOG = COL_FV + FOX_DIM

MOD_TILE = 1024
TM_PROJ = 512
TQ = 256
TB = 128
TM_DISPATCH = 2048
TM_COMBINE = 512
TM_GROUP = 512
RANK_BITS = 20
LOG2_E = 1.4426950408889634
LOG2_NEGLIGIBLE = -40.0
VMEM_LIMIT = 48 * 1024 * 1024
INPROJ_VMEM_LIMIT = 56 * 1024 * 1024
EXPERT_VMEM_LIMIT = 58 * 1024 * 1024


def _cparams(n_axes, vmem=VMEM_LIMIT):
    return pltpu.CompilerParams(dimension_semantics=("arbitrary",) * n_axes, vmem_limit_bytes=vmem)


def _dot(a, b):
    return jnp.dot(a, b, preferred_element_type=F32)


def _dot_nt(a, b):
    return lax.dot_general(a, b, (((1,), (1,)), ((), ())), preferred_element_type=F32)


def _split2(x):
    hi = x.astype(BF16)
    lo = (x - hi.astype(F32)).astype(BF16)
    return hi, lo


def _row_copy(src, src_row, dst, dst_row, sem):
    return pltpu.make_async_copy(src.at[pl.ds(pl.multiple_of(src_row * SUBLANES, SUBLANES), SUBLANES), :],
                                 dst.at[pl.ds(pl.multiple_of(dst_row * SUBLANES, SUBLANES), SUBLANES), :], sem)


def _log_sigmoid(x):
    return jnp.minimum(x, 0.0) - jnp.log(1.0 + jnp.exp(-jnp.abs(x)))


def _mod_kernel(c_ref, w_ref, b_ref, o_ref):
    c = c_ref[...]
    s = c / (1.0 + jnp.exp(-c))
    o_ref[...] = _dot(s.astype(BF16), w_ref[0].astype(BF16)) + b_ref[...]


def _modulation(c, layer, w_ada, b_ada):
    b, d = c.shape
    n = w_ada.shape[2]
    tn = MOD_TILE
    return pl.pallas_call(
        _mod_kernel,
        grid=(n // tn,),
        in_specs=[pl.BlockSpec((b, d), lambda j: (0, 0)),
                  pl.BlockSpec((1, d, tn), lambda j: (layer, 0, j)),
                  pl.BlockSpec((1, tn), lambda j: (0, j))],
        out_specs=pl.BlockSpec((b, tn), lambda j: (0, j)),
        out_shape=jax.ShapeDtypeStruct((b, n), F32),
        compiler_params=_cparams(1),
        name="modulation",
    )(c, w_ada, b_ada.reshape(1, n))


def _gather_rows(yg_ref, idx_ref, buf_ref, sem, slot, n_rows):
    def issue(r, c):
        for k in range(TOP_K):
            _row_copy(yg_ref, idx_ref[r * TOP_K + k], buf_ref.at[slot, k], r, sem.at[slot]).start()
        return c

    lax.fori_loop(0, n_rows, issue, 0)


def _combine_rows(gate_ref, buf_ref, slot, n_rows, x1_ref, g2, out_ref):
    gates = gate_ref[...]
    gate_cols = [gates[:, k:k + 1] for k in range(TOP_K)]
    for c in range(ROW_TILES):
        acc = jnp.zeros((n_rows, LANES), F32)
        for k in range(TOP_K):
            acc = acc + gate_cols[k] * buf_ref[slot, k, pl.ds(c, n_rows, stride=SUBLANES), :]
        cols = slice(c * LANES, (c + 1) * LANES)
        out_ref[:, cols] = x1_ref[:, cols] + g2[:, cols] * acc


def _wait_rows(buf_ref, sem, slot):
    for k in range(TOP_K):
        pltpu.make_async_copy(buf_ref.at[slot, k], buf_ref.at[slot, k], sem.at[slot]).wait()


def _inproj_kernel(tiles_per_seq, fused, *refs):
    if fused:
        (pos_ref, pos_next_ref, gate_ref, x1_ref, modp_ref, yg_ref, mod_ref, g1_ref, w_ref, bd_ref, tri_ref, fb_ref,
         gains_ref, x_out_ref, cb_ref, v_ref, sbq_ref, sbk_ref, sbv_ref, fq_ref, fk_ref, fv_ref, cum_ref,
         carry_ref, x_ref, buf_ref, sem) = refs
        tm = x1_ref.shape[0]
        i = pl.program_id(0)
        slot = i % 2

        @pl.when(i == 0)
        def _():
            _gather_rows(yg_ref, pos_ref, buf_ref, sem, 0, tm)

        @pl.when(i + 1 < pl.num_programs(0))
        def _():
            _gather_rows(yg_ref, pos_next_ref, buf_ref, sem, 1 - slot, tm)

        _wait_rows(buf_ref, sem, slot)
        _combine_rows(gate_ref, buf_ref, slot, tm, x1_ref, modp_ref[0, 5:6, :], x_ref)
        x_out_ref[...] = x_ref[...]
    else:
        (x_ref, mod_ref, g1_ref, w_ref, bd_ref, tri_ref, fb_ref, gains_ref,
         cb_ref, v_ref, sbq_ref, sbk_ref, sbv_ref, fq_ref, fk_ref, fv_ref, cum_ref, carry_ref) = refs
        tm = x_ref.shape[0]
    x = x_ref[...]
    ms = jnp.mean(x * x, axis=-1, keepdims=True)
    h = (x * lax.rsqrt(ms + EPS)) * g1_ref[...]
    h = h * (1.0 + mod_ref[0, 1:2, :]) + mod_ref[0, 0:1, :]
    hb = h.astype(BF16)

    pc = _dot(hb, w_ref[:, 0:3 * CONV_CH])
    cb_ref[...] = pc[:, 0:CONV_CH]
    v_ref[...] = pc[:, CONV_CH:2 * CONV_CH] * pc[:, 2 * CONV_CH:3 * CONV_CH]

    normed = ((COL_SBQ, sbq_ref), (COL_SBK, sbk_ref), (COL_FQ, fq_ref), (COL_FK, fk_ref))
    proj = [_dot(hb, w_ref[:, col:col + SB_DIM]) for col, _ in normed]
    sbv_ref[...] = _dot(hb, w_ref[:, COL_SBV:COL_SBV + SB_DIM]).astype(BF16)
    fv_ref[...] = _dot(hb, w_ref[:, COL_FV:COL_FV + FOX_DIM]).astype(BF16)
    flog = _dot(hb, w_ref[:, COL_FLOG:COL_FLOG + LANES])
    def head_sumsq(p):
        sq = (p * p).astype(BF16)
        wide = 2 * LANES
        return jnp.concatenate([_dot(sq[:, :wide], bd_ref[:wide, :wide]), _dot(sq[:, wide:], bd_ref[wide:, wide:])], axis=-1)

    sumsq = [head_sumsq(p) for p in proj]
    for row, ((_, out_ref), p, ss) in enumerate(zip(normed, proj, sumsq)):
        out_ref[...] = (p * lax.rsqrt(ss * (1.0 / HEAD_DIM) + EPS) * gains_ref[row:row + 1, :]).astype(BF16)

    lf = _log_sigmoid(flog + fb_ref[...]) * LOG2_E
    t0 = lf.astype(BF16)
    r1 = lf - t0.astype(F32)
    t1 = r1.astype(BF16)
    t2 = (r1 - t1.astype(F32)).astype(BF16)
    tri = tri_ref[...]
    cs = _dot(tri, t0) + _dot(tri, t1) + _dot(tri, t2)

    @pl.when(pl.program_id(0) % tiles_per_seq == 0)
    def _():
        carry_ref[...] = jnp.zeros_like(carry_ref)

    cum = cs + carry_ref[0:1, :]
    carry_ref[0:1, :] = cum[tm - 1:tm, :]
    cum_ref[0] = cum.T[0:SUBLANES, :]


def _inproj(x_parts, mod3, g1, w_in_p, bd, tri, fb, gains, seq):
    fused = len(x_parts) > 1
    t = x_parts[2].shape[0] if fused else x_parts[0].shape[0]
    tm = TM_PROJ
    n = t // tm
    tiles_per_seq = seq // tm
    nb = t // seq
    row = lambda i: (i, 0)
    const = lambda i: (0, 0)
    per_seq = lambda i: (i // tiles_per_seq, 0, 0)
    outs = [jax.ShapeDtypeStruct((t, CONV_CH), F32), jax.ShapeDtypeStruct((t, CONV_CH), F32)]
    outs += [jax.ShapeDtypeStruct((t, SB_DIM), BF16)] * 6
    outs += [jax.ShapeDtypeStruct((nb, SUBLANES, seq), F32)]
    out_specs = [pl.BlockSpec((tm, CONV_CH), row)] * 2 + [pl.BlockSpec((tm, SB_DIM), row)] * 6
    out_specs += [pl.BlockSpec((1, SUBLANES, tm), lambda i: (i // tiles_per_seq, 0, i % tiles_per_seq))]
    in_specs = [pl.BlockSpec((1, N_MOD, D_MODEL), per_seq),
                pl.BlockSpec((1, D_MODEL), const),
                pl.BlockSpec((D_MODEL, D_IN_PAD), const),
                pl.BlockSpec((SB_DIM, SB_DIM), const),
                pl.BlockSpec((tm, tm), const),
                pl.BlockSpec((1, LANES), const),
                pl.BlockSpec((SUBLANES, SB_DIM), const)]
    scratch = [pltpu.VMEM((SUBLANES, LANES), F32)]
    if fused:
        pos, gates, x1, mod3_prev, yg = x_parts
        operands = (pos, pos, gates, x1, mod3_prev, yg)
        in_specs = [pl.BlockSpec((tm * TOP_K,), lambda i: (i,), memory_space=pltpu.SMEM),
                    pl.BlockSpec((tm * TOP_K,), lambda i: (jnp.minimum(i + 1, n - 1),), memory_space=pltpu.SMEM),
                    pl.BlockSpec((tm, LANES), row),
                    pl.BlockSpec((tm, D_MODEL), row),
                    pl.BlockSpec((1, N_MOD, D_MODEL), per_seq),
                    pl.BlockSpec(memory_space=pl.ANY)] + in_specs
        outs = [jax.ShapeDtypeStruct((t, D_MODEL), F32)] + outs
        out_specs = [pl.BlockSpec((tm, D_MODEL), row)] + out_specs
        scratch += [pltpu.VMEM((tm, D_MODEL), F32), pltpu.VMEM((2, TOP_K, tm * SUBLANES, LANES), F32),
                    pltpu.SemaphoreType.DMA((2,))]
    else:
        operands = x_parts
        in_specs = [pl.BlockSpec((tm, D_MODEL), row)] + in_specs
    return pl.pallas_call(
        functools.partial(_inproj_kernel, tiles_per_seq, fused),
        grid=(n,),
        in_specs=in_specs,
        out_specs=out_specs,
        out_shape=outs,
        scratch_shapes=scratch,
        compiler_params=_cparams(1, INPROJ_VMEM_LIMIT),
        name="inproj",
    )(*operands, mod3, g1, w_in_p, bd, tri, fb, gains)


def _stack_heads(k_ref, v_ref, kk_ref, vv_ref):
    n_blocks = kk_ref.shape[1]
    lane = lax.broadcasted_iota(jnp.int32, (TB, LANES), 1)
    first = lane < HEAD_DIM

    def body(j, c):
        rows = pl.ds(pl.multiple_of(j * TB, TB), TB)
        for src, dst in ((k_ref, kk_ref), (v_ref, vv_ref)):
            for p in range(N_PAIRS):
                blk = src[0, rows, p * LANES:(p + 1) * LANES]
                zero = jnp.zeros_like(blk)
                dst[p, j, 0:TB, :] = jnp.where(first, blk, zero)
                dst[p, j, TB:2 * TB, :] = jnp.where(first, zero, blk)
        return c

    lax.fori_loop(0, n_blocks, body, 0)


def _per_head(a0, a1):
    return jnp.concatenate([jnp.broadcast_to(a0, (TB, TB)), jnp.broadcast_to(a1, (TB, TB))], axis=1)


def _pair_iota():
    row = lax.broadcasted_iota(jnp.int32, (TB, 2 * TB), 0)
    col = lax.broadcasted_iota(jnp.int32, (TB, 2 * TB), 1) % TB
    return row, col


_CHAINS = [(p, s) for p in range(N_PAIRS) for s in range(2)]


def _q_block(q_ref, p, s):
    return q_ref[0, s * TB:(s + 1) * TB, p * LANES:(p + 1) * LANES]


def _sb_kernel(q_ref, k_ref, v_ref, uu_ref, o_ref, kk_ref, vv_ref):
    qi = pl.program_id(1)

    @pl.when(qi == 0)
    def _():
        _stack_heads(k_ref, v_ref, kk_ref, vv_ref)

    uu = uu_ref[...]
    row, col = _pair_iota()
    strict = col < row

    def sweep(states, backs):
        items = [(c, b) for b in range(len(backs)) for c in range(len(_CHAINS))]
        kbs, keeps, z = {}, {}, {}
        for c, b in items:
            p, s = _CHAINS[c]
            back = backs[b]
            kb = 2 * qi + s - back
            if isinstance(back, int) and back == 0:
                keeps[c, b] = strict
            elif isinstance(back, int) and back == 1 and s == 1:
                keeps[c, b] = None
            else:
                keeps[c, b] = kb >= 0
            kbs[c, b] = jnp.maximum(kb, 0)
            z[c, b] = _dot_nt(_q_block(q_ref, p, s), kk_ref[p, kbs[c, b]])
        log_beta, log_1mb, sums = {}, {}, {}
        for it in items:
            log_beta[it] = jnp.minimum(z[it], 0.0) - jnp.log2(1.0 + jnp.exp2(-jnp.abs(z[it])))
            l1 = log_beta[it] - z[it]
            log_1mb[it] = l1 if keeps[it] is None else jnp.where(keeps[it], l1, 0.0)
            sums[it] = (jnp.sum(log_1mb[it][:, :TB], axis=-1, keepdims=True),
                        jnp.sum(log_1mb[it][:, TB:], axis=-1, keepdims=True))
        later = {}
        for it in items:
            hi, lo = _split2(log_1mb[it])
            later[it] = _dot(hi, uu) + _dot(lo, uu)
        states = list(states)
        for it in items:
            c = it[0]
            rs0, rs1, acc = states[c]
            w = jnp.exp2(log_beta[it] + later[it] + _per_head(rs0, rs1))
            if keeps[it] is not None:
                w = jnp.where(keeps[it], w, 0.0)
            acc = acc + _dot(w.astype(BF16), vv_ref[_CHAINS[c][0], kbs[it]])
            states[c] = (rs0 + sums[it][0], rs1 + sums[it][1], acc)
        return states

    def more(states):
        top = states[0][0]
        for st in states:
            top = jnp.maximum(top, jnp.maximum(st[0], st[1]))
        return (jnp.max(top) > LOG2_NEGLIGIBLE).astype(jnp.int32)

    zero = jnp.zeros((TB, 1), F32)
    states = sweep([(zero, zero, jnp.zeros((TB, LANES), F32))] * len(_CHAINS), (0, 1))

    def cond(c):
        return jnp.logical_and(2 * qi + 1 - c[0] >= 0, c[2] > 0)

    def body(c):
        back, states, _ = c
        states = sweep(states, (back,))
        return back + 1, states, more(states)

    _, states, _ = lax.while_loop(cond, body, (jnp.int32(2), states, more(states)))
    for (p, s), st in zip(_CHAINS, states):
        o_ref[0, s * TB:(s + 1) * TB, p * LANES:(p + 1) * LANES] = st[2]


def _sb_attention(q, k, v, uu):
    b, s, d = q.shape
    tq = TQ
    qspec = pl.BlockSpec((1, tq, d), lambda bi, qi: (bi, qi, 0))
    kvspec = pl.BlockSpec((1, s, d), lambda bi, qi: (bi, 0, 0))
    scratch = pltpu.VMEM((N_PAIRS, s // TB, 2 * TB, LANES), BF16)
    return pl.pallas_call(
        _sb_kernel,
        grid=(b, s // tq),
        in_specs=[qspec, kvspec, kvspec, pl.BlockSpec((2 * TB, 2 * TB), lambda bi, qi: (0, 0))],
        out_specs=qspec,
        out_shape=jax.ShapeDtypeStruct((b, s, d), F32),
        scratch_shapes=[scratch, scratch],
        compiler_params=_cparams(2),
        name="sb_attention",
    )(q, k, v, uu)


def _fox_kernel(steps_ref, q_ref, k_ref, v_ref, f2_ref, o_ref, kk_ref, vv_ref, sc_ref):
    bi = pl.program_id(0)
    qi = pl.program_id(1)

    @pl.when(qi == 0)
    def _():
        _stack_heads(k_ref, v_ref, kk_ref, vv_ref)

    row, col = _pair_iota()
    causal = col <= row
    lane = lax.broadcasted_iota(jnp.int32, (TB, LANES), 1)
    first = lane < HEAD_DIM

    def scores(p, s, back, steps):
        kb = 2 * qi + s - back
        sc = _dot_nt(_q_block(q_ref, p, s), kk_ref[p, jnp.maximum(kb, 0)]) - f2_ref[0, p, pl.ds(jnp.maximum(kb, 0), 1), :]
        return jnp.where(jnp.logical_and(kb >= 0, back <= steps), sc, -1e30)

    def head_max(a):
        return (jnp.max(a[:, :TB], axis=-1, keepdims=True), jnp.max(a[:, TB:], axis=-1, keepdims=True))

    def head_sum(a):
        return (jnp.sum(a[:, :TB], axis=-1, keepdims=True), jnp.sum(a[:, TB:], axis=-1, keepdims=True))

    for p in range(N_PAIRS):
        steps = steps_ref[(bi * N_PAIRS + p) * pl.num_programs(1) + qi]
        n_it = (steps + 1) // 2

        runmax = []
        for s in range(2):
            sc = jnp.where(causal, scores(p, s, 0, steps), -1e30)
            sc_ref[s, 0] = sc
            runmax.append(sc)

        def pass1(it, runmax):
            out = list(runmax)
            for u in (1, 2):
                back = 2 * it + u
                for s in range(2):
                    sc = scores(p, s, back, steps)
                    sc_ref[s, back] = sc
                    out[s] = jnp.maximum(out[s], sc)
            return out

        runmax = lax.fori_loop(0, n_it, pass1, runmax)
        tops = [_per_head(*head_max(runmax[s])) for s in range(2)]

        def weigh(s, back, runsum, acc):
            pr = jnp.exp2(sc_ref[s, back] - tops[s])
            kb = jnp.maximum(2 * qi + s - back, 0)
            return runsum + pr, acc + _dot(pr.astype(BF16), vv_ref[p, kb])

        state = [weigh(s, 0, jnp.zeros((TB, 2 * TB), F32), jnp.zeros((TB, LANES), F32)) for s in range(2)]

        def pass2(it, state):
            out = list(state)
            for u in (1, 2):
                for s in range(2):
                    out[s] = weigh(s, 2 * it + u, *out[s])
            return out

        state = lax.fori_loop(0, n_it, pass2, state)
        for s in range(2):
            l0, l1 = head_sum(state[s][0])
            o_ref[0, s * TB:(s + 1) * TB, p * LANES:(p + 1) * LANES] = state[s][1] / jnp.where(first, l0, l1)


def _fox_attention(q, k, v, f2, steps):
    b, s, d = q.shape
    tq = TQ
    n_kb = s // TB
    qspec = pl.BlockSpec((1, tq, d), lambda bi, qi, *_: (bi, qi, 0))
    kvspec = pl.BlockSpec((1, s, d), lambda bi, qi, *_: (bi, 0, 0))
    scratch = pltpu.VMEM((N_PAIRS, n_kb, 2 * TB, LANES), BF16)
    grid_spec = pltpu.PrefetchScalarGridSpec(
        num_scalar_prefetch=1,
        grid=(b, s // tq),
        in_specs=[qspec, kvspec, kvspec,
                  pl.BlockSpec((1, N_PAIRS, n_kb, 2 * TB), lambda bi, qi, *_: (bi, 0, 0, 0))],
        out_specs=qspec,
        scratch_shapes=[scratch, scratch, pltpu.VMEM((2, n_kb + 2, TB, 2 * TB), F32)],
    )
    return pl.pallas_call(
        _fox_kernel,
        grid_spec=grid_spec,
        out_shape=jax.ShapeDtypeStruct((b, s, d), F32),
        compiler_params=_cparams(2),
        name="fox_attention",
    )(steps, q, k, v, f2)


def _outproj_kernel(tiles_per_seq, x_ref, cb_ref, v_ref, vprev_ref, ysb_ref, yfox_ref, cw_ref, og_ref, wout_ref,
                    mod_ref, g2_ref, rw_ref, rb_ref, bd_ref, ltri_ref,
                    x1_ref, h2t_ref, sel_ref, gate_ref, cnt_ref, vext_ref, carry_ref):
    tm = x_ref.shape[0]
    i = pl.program_id(0)

    @pl.when(i == 0)
    def _():
        carry_ref[...] = jnp.zeros_like(carry_ref)

    first = (i % tiles_per_seq) == 0
    vext_ref[0:SUBLANES, :] = jnp.where(first, 0.0, vprev_ref[...])
    vext_ref[SUBLANES:SUBLANES + tm, :] = v_ref[...]
    conv = (cw_ref[0:1, :] * vext_ref[SUBLANES - 2:SUBLANES - 2 + tm, :]
            + cw_ref[1:2, :] * vext_ref[SUBLANES - 1:SUBLANES - 1 + tm, :]
            + cw_ref[2:3, :] * v_ref[...])
    y = jnp.concatenate([cb_ref[...] * conv, ysb_ref[...], yfox_ref[...]], axis=-1)

    ysq = (y * y).astype(BF16)
    bd = bd_ref[...]
    ss = jnp.concatenate([_dot(ysq[:, c:c + 2 * LANES], bd) for c in range(0, D_MIX, 2 * LANES)], axis=-1)
    yn = y * lax.rsqrt(ss * (1.0 / HEAD_DIM) + EPS) * og_ref[...]
    x1 = x_ref[...] + mod_ref[0, 2:3, :] * _dot(yn.astype(BF16), wout_ref[...])
    x1_ref[...] = x1

    ms = jnp.mean(x1 * x1, axis=-1, keepdims=True)
    h2 = (x1 * lax.rsqrt(ms + EPS)) * g2_ref[...]
    h2 = h2 * (1.0 + mod_ref[0, 4:5, :]) + mod_ref[0, 3:4, :]
    for j in range(ROW_TILES):
        h2t_ref[pl.ds(j, tm, stride=SUBLANES), :] = h2[:, j * LANES:(j + 1) * LANES]

    h_hi, h_lo = _split2(h2)
    both = _dot(h_hi, rw_ref[...])
    logits = both[:, :LANES] + both[:, LANES:] + _dot(h_lo, rw_ref[:, :LANES]) + rb_ref[...]
    lane = lax.broadcasted_iota(jnp.int32, (tm, LANES), 1)
    lane_f = lane.astype(F32)
    neg = jnp.float32(-jnp.inf)
    cur = jnp.where(lane < N_EXPERTS, logits, neg)
    tops, sels = [], []
    for _ in range(TOP_K):
        mk = jnp.max(cur, axis=-1, keepdims=True)
        ik = jnp.min(jnp.where(cur == mk, lane_f, float(LANES)), axis=-1, keepdims=True)
        sel = lane_f == ik
        cur = jnp.where(sel, neg, cur)
        tops.append(mk)
        sels.append((sel, ik.astype(jnp.int32)))
    es = [jnp.exp(t - tops[0]) for t in tops]
    inv = 1.0 / (es[0] + es[1] + es[2] + es[3])

    multi = jnp.zeros((tm, LANES), F32)
    for sel, _ in sels:
        multi = jnp.where(sel, 1.0, multi)
    before = _dot(ltri_ref[...], multi.astype(BF16)) + carry_ref[0:1, :]
    carry_ref[0:1, :] = before[tm - 1:tm, :] + multi[tm - 1:tm, :]

    sel_out = jnp.zeros((tm, LANES), jnp.int32)
    gate_out = jnp.zeros((tm, LANES), F32)
    for k, (sel, ik) in enumerate(sels):
        rank = jnp.sum(jnp.where(sel, before, 0.0), axis=-1, keepdims=True).astype(jnp.int32)
        sel_out = jnp.where(lane == k, ik * (1 << RANK_BITS) + rank, sel_out)
        gate_out = jnp.where(lane == k, es[k] * inv, gate_out)
    sel_ref[...] = sel_out
    gate_ref[...] = gate_out
    cnt_ref[...] = jnp.broadcast_to(carry_ref[0:1, :], cnt_ref.shape)


def _outproj(x2, cb, v, ysb, yfox, conv_w, og, w_out_b, mod3, g2, rw, rb, bd2, ltri, seq):
    t = x2.shape[0]
    tm = TM_PROJ
    tiles_per_seq = seq // tm
    row = lambda i: (i, 0)
    const = lambda i: (0, 0)
    return pl.pallas_call(
        functools.partial(_outproj_kernel, tiles_per_seq),
        grid=(t // tm,),
        in_specs=[pl.BlockSpec((tm, D_MODEL), row),
                  pl.BlockSpec((tm, CONV_CH), row),
                  pl.BlockSpec((tm, CONV_CH), row),
                  pl.BlockSpec((SUBLANES, CONV_CH), lambda i: (jnp.maximum(i * (tm // SUBLANES) - 1, 0), 0)),
                  pl.BlockSpec((tm, SB_DIM), row),
                  pl.BlockSpec((tm, FOX_DIM), row),
                  pl.BlockSpec((CONV_W, CONV_CH), const),
                  pl.BlockSpec((1, D_MIX), const),
                  pl.BlockSpec((D_MIX, D_MODEL), const),
                  pl.BlockSpec((1, N_MOD, D_MODEL), lambda i: (i // tiles_per_seq, 0, 0)),
                  pl.BlockSpec((1, D_MODEL), const),
                  pl.BlockSpec((D_MODEL, 2 * LANES), const),
                  pl.BlockSpec((1, LANES), const),
                  pl.BlockSpec((2 * LANES, 2 * LANES), const),
                  pl.BlockSpec((tm, tm), const)],
        out_specs=[pl.BlockSpec((tm, D_MODEL), row),
                   pl.BlockSpec((tm * SUBLANES, LANES), row),
                   pl.BlockSpec((tm, LANES), row),
                   pl.BlockSpec((tm, LANES), row),
                   pl.BlockSpec((SUBLANES, LANES), const)],
        out_shape=[jax.ShapeDtypeStruct((t, D_MODEL), F32),
                   jax.ShapeDtypeStruct((t * SUBLANES, LANES), F32),
                   jax.ShapeDtypeStruct((t, LANES), jnp.int32),
                   jax.ShapeDtypeStruct((t, LANES), F32),
                   jax.ShapeDtypeStruct((SUBLANES, LANES), F32)],
        scratch_shapes=[pltpu.VMEM((tm + SUBLANES, CONV_CH), F32), pltpu.VMEM((SUBLANES, LANES), F32)],
        compiler_params=_cparams(1),
        name="outproj_router",
    )(x2, cb, v, v, ysb, yfox, conv_w, og, w_out_b, mod3, g2, rw, rb, bd2, ltri)


def _zero_fill(meta_ref, zero_ref, xg_ref, zsem):
    n_tiles = xg_ref.shape[0] // (TM_GROUP * SUBLANES)
    zero_ref[...] = jnp.zeros_like(zero_ref)

    def sweep(wait):
        def per_expert(e, c):
            pos = meta_ref[N_EXPERTS + e]
            n = meta_ref[2 * N_EXPERTS + e]
            bit = TM_GROUP // 2
            while bit:
                @pl.when((n & bit) != 0)
                def _(pos=pos, bit=bit):
                    cp = pltpu.make_async_copy(
                        zero_ref.at[pl.ds(0, bit * SUBLANES), :],
                        xg_ref.at[pl.ds(pl.multiple_of(pos * SUBLANES, SUBLANES), bit * SUBLANES), :], zsem)
                    cp.wait() if wait else cp.start()
                pos = pos + (n & bit)
                bit //= 2
            return c

        lax.fori_loop(0, N_EXPERTS, per_expert, 0)

        def per_tile(j, c):
            rows = TM_GROUP * SUBLANES
            cp = pltpu.make_async_copy(zero_ref, xg_ref.at[pl.ds(pl.multiple_of(j * rows, rows), rows), :], zsem)
            cp.wait() if wait else cp.start()
            return c

        lax.fori_loop(meta_ref[3 * N_EXPERTS], n_tiles, per_tile, 0)

    sweep(False)
    sweep(True)


def _dispatch_kernel(meta_ref, pos_ref, h_ref, xg_ref, zero_ref, sem, zsem):
    tm = h_ref.shape[0] // SUBLANES

    @pl.when(pl.program_id(0) == 0)
    def _():
        _zero_fill(meta_ref, zero_ref, xg_ref, zsem)

    def issue(r, c):
        for k in range(TOP_K):
            _row_copy(h_ref, r, xg_ref, pos_ref[r * TOP_K + k], sem).start()
        return c

    lax.fori_loop(0, tm, issue, 0)
    n = tm * TOP_K * SUBLANES
    pltpu.make_async_copy(xg_ref.at[pl.ds(0, n), :], xg_ref.at[pl.ds(0, n), :], sem).wait()


def _dispatch(meta, pos, h2t, n_rows):
    t = h2t.shape[0] // SUBLANES
    tm = TM_DISPATCH
    grid_spec = pltpu.PrefetchScalarGridSpec(
        num_scalar_prefetch=1,
        grid=(t // tm,),
        in_specs=[pl.BlockSpec((tm * TOP_K,), lambda i, *_: (i,), memory_space=pltpu.SMEM),
                  pl.BlockSpec((tm * SUBLANES, LANES), lambda i, *_: (i, 0))],
        out_specs=pl.BlockSpec(memory_space=pl.ANY),
        scratch_shapes=[pltpu.VMEM((TM_GROUP * SUBLANES, LANES), F32),
                        pltpu.SemaphoreType.DMA(()), pltpu.SemaphoreType.DMA(())],
    )
    return pl.pallas_call(
        _dispatch_kernel,
        grid_spec=grid_spec,
        out_shape=jax.ShapeDtypeStruct((n_rows * SUBLANES, LANES), F32),
        compiler_params=_cparams(1),
        name="moe_dispatch",
    )(meta, pos, h2t)


def _expert_kernel(layer, te_ref, nv_ref, nxt_ref, run_ref, xg_ref, w1_hbm, b1_ref, w2_hbm, b2_ref, yg_ref,
                   wg_ref, wl_ref, w2b_ref, wt_ref, w1_ref, w2_ref, wsem):
    tmg = xg_ref.shape[0] // SUBLANES
    j = pl.program_id(0)
    nv = nv_ref[j]
    slot = run_ref[j] % 2

    def weight_copies(expert, to_slot):
        return (pltpu.make_async_copy(w1_hbm.at[layer, expert], w1_ref.at[to_slot], wsem.at[to_slot, 0]),
                pltpu.make_async_copy(w2_hbm.at[layer, expert], w2_ref.at[to_slot], wsem.at[to_slot, 1]))

    @pl.when(j == 0)
    def _():
        for cp in weight_copies(te_ref[0], slot):
            cp.start()

    @pl.when(jnp.logical_or(j == 0, te_ref[j] != te_ref[jnp.maximum(j - 1, 0)]))
    def _():
        for cp in weight_copies(te_ref[j], slot):
            cp.wait()

        @pl.when(nxt_ref[j] >= 0)
        def _():
            for cp in weight_copies(nxt_ref[j], 1 - slot):
                cp.start()

        ncol = wt_ref.shape[0] // ROW_TILES
        half = ncol // 2
        for n0 in range(0, 2 * D_EXPERT, ncol):
            wt = w1_ref[slot, :, n0:n0 + ncol].T
            for c in range(ROW_TILES):
                wt_ref[pl.ds(c, ncol, stride=ROW_TILES), :] = wt[:, c * LANES:(c + 1) * LANES]
            rows = slice(n0 // 2, n0 // 2 + half)
            for c in range(ROW_TILES):
                cols = slice(c * LANES, (c + 1) * LANES)
                wg_ref[rows, cols] = wt_ref[pl.ds(c, half, stride=2 * ROW_TILES), :].astype(BF16)
                wl_ref[rows, cols] = wt_ref[pl.ds(ROW_TILES + c, half, stride=2 * ROW_TILES), :].astype(BF16)
            w2b_ref[rows, :] = w2_ref[slot, rows, :].astype(BF16)

    @pl.when(nv == 0)
    def _():
        yg_ref[...] = jnp.zeros_like(yg_ref)

    @pl.when(nv > 0)
    def _():
        x = jnp.concatenate([xg_ref[pl.ds(c, tmg, stride=SUBLANES), :] for c in range(ROW_TILES)], axis=-1)
        x = x.astype(BF16)
        glu = jnp.minimum(_dot_nt(x, wg_ref[...]) + b1_ref[0, 0:1, :], SWIGLU_LIMIT)
        lin = jnp.clip(_dot_nt(x, wl_ref[...]) + b1_ref[0, 1:2, :], -SWIGLU_LIMIT, SWIGLU_LIMIT)
        a = glu * (1.0 / (1.0 + jnp.exp(-SWIGLU_ALPHA * glu))) * (lin + 1.0)
        y = _dot(a.astype(BF16), w2b_ref[...]) + b2_ref[0]
        for c in range(ROW_TILES):
            yg_ref[pl.ds(c, tmg, stride=SUBLANES), :] = y[:, c * LANES:(c + 1) * LANES]


def _experts(te, nv, xg, layer, w1, b1, w2, b2):
    tmg = TM_GROUP
    n_tiles = xg.shape[0] // (tmg * SUBLANES)
    tiles = jnp.arange(n_tiles, dtype=jnp.int32)
    differs = jnp.logical_and(tiles[None, :] > tiles[:, None], te[None, :] != te[:, None])
    nxt_tile = jnp.min(jnp.where(differs, tiles[None, :], n_tiles), axis=1)
    nxt = jnp.where(nxt_tile < n_tiles, te[jnp.minimum(nxt_tile, n_tiles - 1)], -1).astype(jnp.int32)
    run = jnp.cumsum(jnp.concatenate([jnp.zeros((1,), jnp.int32), (te[1:] != te[:-1]).astype(jnp.int32)]))
    per_tile = lambda j, te, *_: (j, 0)
    per_expert = lambda j, te, *_: (te[j], 0, 0)
    grid_spec = pltpu.PrefetchScalarGridSpec(
        num_scalar_prefetch=4,
        grid=(n_tiles,),
        in_specs=[pl.BlockSpec((tmg * SUBLANES, LANES), per_tile),
                  pl.BlockSpec(memory_space=pl.ANY),
                  pl.BlockSpec((1, 2, D_EXPERT), per_expert),
                  pl.BlockSpec(memory_space=pl.ANY),
                  pl.BlockSpec((1, 1, D_MODEL), per_expert)],
        out_specs=pl.BlockSpec((tmg * SUBLANES, LANES), per_tile),
        scratch_shapes=[pltpu.VMEM((D_EXPERT, D_MODEL), BF16), pltpu.VMEM((D_EXPERT, D_MODEL), BF16),
                        pltpu.VMEM((D_EXPERT, D_MODEL), BF16), pltpu.VMEM((2 * LANES * ROW_TILES, LANES), F32),
                        pltpu.VMEM((2, D_MODEL, 2 * D_EXPERT), F32), pltpu.VMEM((2, D_EXPERT, D_MODEL), F32),
                        pltpu.SemaphoreType.DMA((2, 2))],
    )
    return pl.pallas_call(
        functools.partial(_expert_kernel, layer),
        grid_spec=grid_spec,
        out_shape=jax.ShapeDtypeStruct(xg.shape, F32),
        compiler_params=_cparams(1, EXPERT_VMEM_LIMIT),
        name="moe_experts",
    )(te, nv, nxt, run.astype(jnp.int32), xg, w1, b1, w2, b2)


def _combine_kernel(pos_ref, pos_next_ref, gate_ref, x1_ref, mod_ref, yg_ref, o_ref, buf_ref, sem):
    tm = x1_ref.shape[0]
    i = pl.program_id(0)
    slot = i % 2

    @pl.when(i == 0)
    def _():
        _gather_rows(yg_ref, pos_ref, buf_ref, sem, 0, tm)

    @pl.when(i + 1 < pl.num_programs(0))
    def _():
        _gather_rows(yg_ref, pos_next_ref, buf_ref, sem, 1 - slot, tm)

    _wait_rows(buf_ref, sem, slot)
    _combine_rows(gate_ref, buf_ref, slot, tm, x1_ref, mod_ref[0, 5:6, :], o_ref)


def _combine(pos, gates, x1, mod3, yg, seq):
    t = x1.shape[0]
    tm = TM_COMBINE
    tiles_per_seq = seq // tm
    n = t // tm
    grid_spec = pltpu.PrefetchScalarGridSpec(
        num_scalar_prefetch=0,
        grid=(n,),
        in_specs=[pl.BlockSpec((tm * TOP_K,), lambda i: (i,), memory_space=pltpu.SMEM),
                  pl.BlockSpec((tm * TOP_K,), lambda i: (jnp.minimum(i + 1, n - 1),), memory_space=pltpu.SMEM),
                  pl.BlockSpec((tm, LANES), lambda i: (i, 0)),
                  pl.BlockSpec((tm, D_MODEL), lambda i: (i, 0)),
                  pl.BlockSpec((1, N_MOD, D_MODEL), lambda i: (i // tiles_per_seq, 0, 0)),
                  pl.BlockSpec(memory_space=pl.ANY)],
        out_specs=pl.BlockSpec((tm, D_MODEL), lambda i: (i, 0)),
        scratch_shapes=[pltpu.VMEM((2, TOP_K, tm * SUBLANES, LANES), F32), pltpu.SemaphoreType.DMA((2,))],
    )
    return pl.pallas_call(
        _combine_kernel,
        grid_spec=grid_spec,
        out_shape=jax.ShapeDtypeStruct((t, D_MODEL), F32),
        compiler_params=_cparams(1),
        name="moe_combine",
    )(pos, pos, gates, x1, mod3, yg)


def _block_diag_ones(n):
    i = np.arange(n)
    return jnp.asarray((i[:, None] // HEAD_DIM) == (i[None, :] // HEAD_DIM), dtype=BF16)


def _tri(n, strict, upper):
    i = np.arange(n)
    if upper:
        m = (i[:, None] > i[None, :]) if strict else (i[:, None] >= i[None, :])
        return jnp.asarray(m, dtype=BF16)
    m = (i[None, :] < i[:, None]) if strict else (i[None, :] <= i[:, None])
    return jnp.asarray(m, dtype=BF16)


def _tile2(g):
    return jnp.tile(g, SB_DIM // HEAD_DIM)


def _layer(x_parts, c, seq, layer, last, w_mlp1, w_mlp2, w_ada, norm1_g, b_ada, w_in, conv_w, sb_q_g, sb_k_g, fox_q_g, fox_k_g, fox_f_b,
           out_norm_g, w_out, norm2_g, router_w, router_b, b_mlp1, b_mlp2):
    t = x_parts[0].shape[0] if len(x_parts) == 1 else x_parts[2].shape[0]
    nb = t // seq
    scale = HEAD_DIM ** -0.5

    w_in_p = jnp.pad(w_in, ((0, 0), (0, D_IN_PAD - w_in.shape[1]))).astype(BF16)
    fb = jnp.pad(fox_f_b, (0, LANES - FOX_HEADS)).reshape(1, LANES)
    qscale = scale * LOG2_E
    gains = jnp.stack([_tile2(sb_q_g) * qscale, _tile2(sb_k_g), _tile2(fox_q_g) * qscale, _tile2(fox_k_g)])
    gains = jnp.pad(gains, ((0, SUBLANES - 4), (0, 0)))
    rw = jnp.pad(router_w, ((0, 0), (0, LANES - N_EXPERTS)))
    rw_hi = rw.astype(BF16)
    rw = jnp.concatenate([rw_hi, (rw - rw_hi.astype(F32)).astype(BF16)], axis=1)
    rb = jnp.pad(router_b, (0, LANES - N_EXPERTS)).reshape(1, LANES)
    b1 = jnp.stack([b_mlp1[:, 0::2], b_mlp1[:, 1::2]], axis=1)
    b2 = b_mlp2.reshape(N_EXPERTS, 1, D_MODEL)

    mod3 = _modulation(c, layer, w_ada, b_ada).reshape(nb, N_MOD, D_MODEL)

    outs = _inproj(x_parts, mod3, norm1_g.reshape(1, D_MODEL), w_in_p, _block_diag_ones(SB_DIM),
                   _tri(TM_PROJ, strict=False, upper=False), fb, gains, seq)
    x2 = x_parts[0] if len(x_parts) == 1 else outs[0]
    cb, v, sbq, sbk, sbv, fq, fk, fv, cum = outs[-9:]

    r3 = lambda a: a.reshape(nb, seq, SB_DIM)
    u = _tri(TB, strict=True, upper=True)
    zu = jnp.zeros_like(u)
    uu = jnp.concatenate([jnp.concatenate([u, zu], axis=1), jnp.concatenate([zu, u], axis=1)], axis=0)
    ysb = _sb_attention(r3(sbq), r3(sbk), r3(sbv), uu)

    n_kb = seq // TB
    f2 = cum[:, :FOX_HEADS, :].reshape(nb, FOX_DIM // LANES, 2, n_kb, TB).transpose(0, 1, 3, 2, 4)
    f2 = f2.reshape(nb, FOX_DIM // LANES, n_kb, 2 * TB)
    zbound = 2.0 * 1.02 * HEAD_DIM * qscale * jnp.max(jnp.abs(fox_q_g)) * jnp.max(jnp.abs(fox_k_g))
    fend = cum[:, :FOX_HEADS, TB - 1::TB]
    f_query = jnp.concatenate([fend[..., :1], fend[..., :-1]], axis=-1)
    blk = jnp.arange(n_kb)
    kept = jnp.logical_and(zbound + f_query[..., :, None] - fend[..., None, :] >= LOG2_NEGLIGIBLE,
                           blk[None, :] < blk[:, None])
    steps = kept.sum(-1).reshape(nb, N_PAIRS, 2, n_kb // 2, 2).max(axis=(2, 4))
    steps = steps.astype(jnp.int32).reshape(nb * N_PAIRS * (n_kb // 2))
    yfox = _fox_attention(r3(fq), r3(fk), r3(fv), f2, steps)

    x1, h2t, sel, gates, cnt = _outproj(
        x2, cb, v, ysb.reshape(t, SB_DIM), yfox.reshape(t, FOX_DIM), conv_w, out_norm_g.reshape(1, D_MIX),
        w_out.astype(BF16), mod3, norm2_g.reshape(1, D_MODEL), rw, rb, _block_diag_ones(2 * LANES),
        _tri(TM_PROJ, strict=True, upper=False), seq)

    counts = cnt[0, :N_EXPERTS].astype(jnp.int32)
    padded = ((counts + TM_GROUP - 1) // TM_GROUP) * TM_GROUP
    off_end = jnp.cumsum(padded)
    off = off_end - padded
    n_rows = t * TOP_K + N_EXPERTS * TM_GROUP
    n_tiles = n_rows // TM_GROUP
    start = jnp.arange(n_tiles, dtype=jnp.int32) * TM_GROUP
    te = jnp.minimum(jnp.sum(start[:, None] >= off_end[None, :], axis=1), N_EXPERTS - 1).astype(jnp.int32)
    nv = jnp.clip(off[te] + counts[te] - start, 0, TM_GROUP).astype(jnp.int32)
    meta = jnp.concatenate([off, off + counts, padded - counts, off_end[-1:] // TM_GROUP]).astype(jnp.int32)

    assert t <= 1 << RANK_BITS
    words = sel[:, :TOP_K].reshape(t * TOP_K // LANES, LANES)
    picked = jnp.where((words >> RANK_BITS)[..., None] == jnp.arange(N_EXPERTS), off, 0).sum(-1)
    pos = (picked + (words & ((1 << RANK_BITS) - 1))).astype(jnp.int32).reshape(t * TOP_K)
    xg = _dispatch(meta, pos, h2t, n_rows)
    yg = _experts(te, nv, xg, layer, w_mlp1, b1, w_mlp2, b2)
    return _combine(pos, gates, x1, mod3, yg, seq) if last else (pos, gates, x1, mod3, yg)


def kernel(x, c, norm1_g, w_ada, b_ada, w_in, conv_w, sb_q_g, sb_k_g, fox_q_g, fox_k_g, fox_f_b, out_norm_g, w_out,
           norm2_g, router_w, router_b, w_mlp1, b_mlp1, w_mlp2, b_mlp2):
    b, s, d = x.shape
    x_parts = (x.reshape(b * s, d),)
    params = (norm1_g, b_ada, w_in, conv_w, sb_q_g, sb_k_g, fox_q_g, fox_k_g, fox_f_b, out_norm_g, w_out,
              norm2_g, router_w, router_b, b_mlp1, b_mlp2)
    depth = norm1_g.shape[0]
    for layer in range(depth):
        x_parts = _layer(x_parts, c, s, layer, layer == depth - 1, w_mlp1, w_mlp2, w_ada, *(p[layer] for p in params))
    return x_parts.reshape(b, s, d)
```
